```python
import math
import jax, jax.numpy as jnp
from jax import lax
import numpy as np

D_MODEL = 2048
BATCH = 4
SEQ = 2048
DEPTH = 2
DEC_BATCH = 128
DEC_SEQ = 1
PAST_LEN = 16384
PAGE_SIZE = 128

HEAD_DIM = 64
D_INNER = 2 * D_MODEL
N_HEADS = D_INNER // HEAD_DIM
N_GROUPS = 8
D_STATE = 128
CONV_K = 4
CONV_DIM = D_INNER + 2 * N_GROUPS * D_STATE
CHUNK = 128
D_POOL = D_MODEL
POOL_WINDOWS = (2, 4, 8, 16)
POOL_GROUPS = len(POOL_WINDOWS)
POOL_GC = D_POOL // POOL_GROUPS
POOL_BUF = max(POOL_WINDOWS) - 1
D_FF = 5504
FFN_K = 3
D_IN_TOTAL = D_INNER + CONV_DIM + N_HEADS + D_POOL + 2 * D_MODEL
IN_SPLITS = (D_INNER, D_INNER + CONV_DIM, D_INNER + CONV_DIM + N_HEADS,
             D_INNER + CONV_DIM + N_HEADS + D_POOL)
ALPHA = (2 * DEPTH) ** 0.25
BETA = (8 * DEPTH) ** -0.25
LN_EPS = 1e-5
RMS_EPS = 1e-5

kernel_name = "ssd_pool_gated_hybrid_deepnorm_step"


def layer_norm(x, g, b):
    xf = x.astype(jnp.float32)
    mu = jnp.mean(xf, axis=-1, keepdims=True)
    var = jnp.mean(jnp.square(xf - mu), axis=-1, keepdims=True)
    y = (xf - mu) * lax.rsqrt(var + LN_EPS) * g.astype(jnp.float32) + b.astype(jnp.float32)
    return y.astype(x.dtype)


def gated_rmsnorm(y, z, w):
    v = y.astype(jnp.float32) * jax.nn.silu(z.astype(jnp.float32))
    shp = v.shape
    v = v.reshape(shp[:-1] + (N_GROUPS, shp[-1] // N_GROUPS))
    v = v * lax.rsqrt(jnp.mean(jnp.square(v), axis=-1, keepdims=True) + RMS_EPS)
    return v.reshape(shp) * w.astype(jnp.float32)


def causal_dwconv(u, buf, w, b):
    k = w.shape[0]
    l = u.shape[1]
    ext = jnp.concatenate([buf.astype(u.dtype), u], axis=1)
    y = b
    for j in range(k):
        y = y + ext[:, j:j + l] * w[j]
    return y, ext[:, -(k - 1):]


def ssd_scan(xs, dt, a, bm, cm, s0):
    bsz, l, h, p = xs.shape
    g, n = bm.shape[2], bm.shape[3]
    r = h // g
    q = min(CHUNK, l)
    nc = -(-l // q)
    pad = nc * q - l
    if pad:
        pw4 = ((0, 0), (0, pad), (0, 0), (0, 0))
        xs = jnp.pad(xs, pw4)
        bm = jnp.pad(bm, pw4)
        cm = jnp.pad(cm, pw4)
        dt = jnp.pad(dt, ((0, 0), (0, pad), (0, 0)))
    xs = xs.astype(jnp.float32).reshape(bsz, nc, q, g, r, p)
    dt = dt.reshape(bsz, nc, q, g, r)
    bm = bm.astype(jnp.float32).reshape(bsz, nc, q, g, n)
    cm = cm.astype(jnp.float32).reshape(bsz, nc, q, g, n)
    a_cum = jnp.cumsum(dt * a.reshape(g, r), axis=2)
    xdt = xs * dt[..., None]
    causal = jnp.tril(jnp.ones((q, q), dtype=bool))[:, :, None, None]
    seg = a_cum[:, :, :, None] - a_cum[:, :, None]
    decay = jnp.where(causal, jnp.exp(jnp.where(causal, seg, 0.0)), 0.0)
    cb = jnp.einsum('bcign,bcjgn->bcijg', cm, bm)
    y_diag = jnp.einsum('bcijgr,bcjgrp->bcigrp', cb[..., None] * decay, xdt)
    end_decay = jnp.exp(a_cum[:, :, -1:] - a_cum)
    chunk_states = jnp.einsum('bcjgn,bcjgrp->bcgrpn', bm, xdt * end_decay[..., None])
    chunk_decay = jnp.exp(a_cum[:, :, -1])

    def step(s, inp):
        st, dc = inp
        return s * dc[..., None, None] + st, s

    s_init = s0.astype(jnp.float32).reshape(bsz, g, r, p, n)
    s_final, s_prev = lax.scan(step, s_init,
                               (jnp.moveaxis(chunk_states, 1, 0), jnp.moveaxis(chunk_decay, 1, 0)))
    s_prev = jnp.moveaxis(s_prev, 0, 1)
    y_off = jnp.einsum('bcign,bcgrpn->bcigrp', cm, s_prev) * jnp.exp(a_cum)[..., None]
    y = (y_diag + y_off).reshape(bsz, nc * q, h, p)[:, :l]
    return y, s_final.reshape(bsz, h, p, n)


def pool_mix(u, buf, start_pos, w_pool, scale):
    bsz, l, c = u.shape
    pb = buf.shape[1]
    ext = jnp.concatenate([buf.astype(u.dtype), u], axis=1)
    cs = jnp.concatenate([jnp.zeros((bsz, 1, c), jnp.float32),
                          jnp.cumsum(ext.astype(jnp.float32), axis=1)], axis=1)
    end = cs[:, pb + 1:]
    pos = start_pos + jnp.arange(l)
    means = []
    for gi, win in enumerate(POOL_WINDOWS):
        sl = slice(gi * POOL_GC, (gi + 1) * POOL_GC)
        win_sum = end[..., sl] - cs[:, pb + 1 - win: pb + 1 - win + l, sl]
        cnt = jnp.minimum(win, pos + 1).astype(jnp.float32)
        means.append(win_sum / cnt[None, :, None])
    pooled = jnp.concatenate(means, axis=-1) - u.astype(jnp.float32)
    pooled = pooled.reshape(bsz, l, POOL_GROUPS, POOL_GC)
    out = jnp.einsum('blgc,gcd->blgd', pooled, w_pool.astype(jnp.float32)).reshape(bsz, l, c)
    out = out * scale.astype(jnp.float32)
    return out.astype(u.dtype), ext[:, -pb:]


def layer(x, s_ssm, s_conv, s_pool, s_ffn, start_pos, lw):
    (w_in, b_gate, conv_w, conv_b, dt_bias, a_log, d_skip, norm_w, w_br, w_pool, pool_scale,
     w_out, ln1_g, ln1_b, w_up, fconv_w, fconv_b, w_down, ln2_g, ln2_b) = lw
    bsz, l, _ = x.shape
    proj = x @ w_in
    z, xbc, dt_raw, pool_in, gate_raw = jnp.split(proj, IN_SPLITS, axis=-1)
    xbc, new_conv = causal_dwconv(xbc, s_conv, conv_w, conv_b)
    xbc = jax.nn.silu(xbc)
    xs, bm, cm = jnp.split(xbc, (D_INNER, D_INNER + N_GROUPS * D_STATE), axis=-1)
    xs = xs.reshape(bsz, l, N_HEADS, HEAD_DIM)
    bm = bm.reshape(bsz, l, N_GROUPS, D_STATE)
    cm = cm.reshape(bsz, l, N_GROUPS, D_STATE)
    dt = jax.nn.softplus(dt_raw.astype(jnp.float32) + dt_bias.astype(jnp.float32))
    a = -jnp.exp(a_log.astype(jnp.float32))
    y, new_ssm = ssd_scan(xs, dt, a, bm, cm, s_ssm)
    y = y + d_skip.astype(jnp.float32)[:, None] * xs.astype(jnp.float32)
    y = gated_rmsnorm(y.reshape(bsz, l, D_INNER), z, norm_w).astype(x.dtype)
    y_a = y @ w_br
    y_b, new_pool = pool_mix(pool_in, s_pool, start_pos, w_pool, pool_scale)
    g_a, g_b = jnp.split(jax.nn.sigmoid(gate_raw + b_gate), 2, axis=-1)
    mix = (g_a * y_a + g_b * y_b) @ w_out
    x1 = layer_norm(ALPHA * x + mix, ln1_g, ln1_b)
    h, new_ffn = causal_dwconv(x1 @ w_up, s_ffn, fconv_w, fconv_b)
    hg, hv = jnp.split(h, 2, axis=-1)
    x2 = layer_norm(ALPHA * x1 + (jax.nn.silu(hg) * hv) @ w_down, ln2_g, ln2_b)
    return x2, new_ssm.astype(x.dtype), new_conv, new_pool, new_ffn


def setup_inputs(seed: int = 0) -> dict:
    key = jax.random.key(seed)
    ks = jax.random.split(key, 32)
    f32 = jnp.float32
    nrm = lambda k, shp, s: jax.random.normal(k, shp, f32) * s
    dt0 = jnp.exp(jax.random.uniform(ks[10], (DEPTH, N_HEADS), f32)
                  * (math.log(0.1) - math.log(0.001)) + math.log(0.001))
    return {
        "x_prompt": nrm(ks[0], (BATCH, SEQ, D_MODEL), 1.0),
        "x_sample": nrm(ks[1], (DEC_BATCH, DEC_SEQ, D_MODEL), 1.0),
        "state_ssm": nrm(ks[2], (DEPTH, DEC_BATCH, N_HEADS, HEAD_DIM, D_STATE), 0.1),
        "state_ssd_conv": nrm(ks[3], (DEPTH, DEC_BATCH, CONV_K - 1, CONV_DIM), 1.0),
        "state_pool": nrm(ks[4], (DEPTH, DEC_BATCH, POOL_BUF, D_POOL), 1.0),
        "state_ffn_conv": nrm(ks[5], (DEPTH, DEC_BATCH, FFN_K - 1, 2 * D_FF), 1.0),
        "w_in": nrm(ks[6], (DEPTH, D_MODEL, D_IN_TOTAL), D_MODEL ** -0.5),
        "b_gate": nrm(ks[7], (DEPTH, 2 * D_MODEL), 0.02),
        "conv_w": nrm(ks[8], (DEPTH, CONV_K, CONV_DIM), CONV_K ** -0.5),
        "conv_b": nrm(ks[9], (DEPTH, CONV_DIM), 0.02),
        "dt_bias": jnp.log(jnp.expm1(dt0)),
        "a_log": jnp.log(jax.random.uniform(ks[11], (DEPTH, N_HEADS), f32, 1.0, 16.0)),
        "d_skip": 1.0 + nrm(ks[12], (DEPTH, N_HEADS), 0.02),
        "ssd_norm_w": 1.0 + nrm(ks[13], (DEPTH, D_INNER), 0.02),
        "w_ssd_branch": nrm(ks[14], (DEPTH, D_INNER, D_MODEL), D_INNER ** -0.5),
        "w_pool": nrm(ks[15], (DEPTH, POOL_GROUPS, POOL_GC, POOL_GC), POOL_GC ** -0.5),
        "pool_scale": 1.0 + nrm(ks[16], (DEPTH, D_POOL), 0.1),
        "w_out": nrm(ks[17], (DEPTH, D_MODEL, D_MODEL), BETA * D_MODEL ** -0.5),
        "ln1_g": 1.0 + nrm(ks[18], (DEPTH, D_MODEL), 0.02),
        "ln1_b": nrm(ks[19], (DEPTH, D_MODEL), 0.02),
        "w_up": nrm(ks[20], (DEPTH, D_MODEL, 2 * D_FF), D_MODEL ** -0.5),
        "ffn_conv_w": nrm(ks[21], (DEPTH, FFN_K, 2 * D_FF), FFN_K ** -0.5),
        "ffn_conv_b": nrm(ks[22], (DEPTH, 2 * D_FF), 0.02),
        "w_down": nrm(ks[23], (DEPTH, D_FF, D_MODEL), BETA * D_FF ** -0.5),
        "ln2_g": 1.0 + nrm(ks[24], (DEPTH, D_MODEL), 0.02),
        "ln2_b": nrm(ks[25], (DEPTH, D_MODEL), 0.02),
    }


def reference(x_prompt, x_sample, state_ssm, state_ssd_conv, state_pool, state_ffn_conv,
              w_in, b_gate, conv_w, conv_b, dt_bias, a_log, d_skip, ssd_norm_w, w_ssd_branch,
              w_pool, pool_scale, w_out, ln1_g, ln1_b, w_up, ffn_conv_w, ffn_conv_b, w_down,
              ln2_g, ln2_b):
    yp, ys = x_prompt, x_sample
    bp = x_prompt.shape[0]
    dtp = x_prompt.dtype
    p_ssm, p_conv, p_pool, p_ffn = [], [], [], []
    s_ssm, s_conv, s_pool, s_ffn = [], [], [], []
    for i in range(DEPTH):
        lw = (w_in[i], b_gate[i], conv_w[i], conv_b[i], dt_bias[i], a_log[i], d_skip[i],
              ssd_norm_w[i], w_ssd_branch[i], w_pool[i], pool_scale[i], w_out[i],
              ln1_g[i], ln1_b[i], w_up[i], ffn_conv_w[i], ffn_conv_b[i], w_down[i],
              ln2_g[i], ln2_b[i])
        yp, a1, a2, a3, a4 = layer(
            yp,
            jnp.zeros((bp, N_HEADS, HEAD_DIM, D_STATE), dtp),
            jnp.zeros((bp, CONV_K - 1, CONV_DIM), dtp),
            jnp.zeros((bp, POOL_BUF, D_POOL), dtp),
            jnp.zeros((bp, FFN_K - 1, 2 * D_FF), dtp),
            0, lw)
        ys, b1, b2, b3, b4 = layer(ys, state_ssm[i], state_ssd_conv[i], state_pool[i],
                                   state_ffn_conv[i], PAST_LEN, lw)
        p_ssm.append(a1); p_conv.append(a2); p_pool.append(a3); p_ffn.append(a4)
        s_ssm.append(b1); s_conv.append(b2); s_pool.append(b3); s_ffn.append(b4)
    return (yp, ys,
            jnp.stack(p_ssm), jnp.stack(p_conv), jnp.stack(p_pool), jnp.stack(p_ffn),
            jnp.stack(s_ssm), jnp.stack(s_conv), jnp.stack(s_pool), jnp.stack(s_ffn))
```

```python
import functools

import jax
import jax.numpy as jnp
from jax import lax
from jax.experimental import pallas as pl
from jax.experimental.pallas import tpu as pltpu

F32 = jnp.float32
BF16 = jnp.bfloat16

D_MODEL = 2048
HEAD_DIM = 64
D_INNER = 2 * D_MODEL
N_HEADS = D_INNER // HEAD_DIM
N_GROUPS = 8
HEADS_PER_GROUP = N_HEADS // N_GROUPS
GROUP_W = D_INNER // N_GROUPS
D_STATE = 128
CONV_K = 4
BC_W = 2 * N_GROUPS * D_STATE
CONV_DIM = D_INNER + BC_W
CHUNK = 128
D_POOL = D_MODEL
POOL_WINDOWS = (2, 4, 8, 16)
POOL_GC = D_POOL // len(POOL_WINDOWS)
POOL_BUF = max(POOL_WINDOWS) - 1
D_FF = 5504
FFN_K = 3
DEPTH = 2
PAST_LEN = 16384
ALPHA = (2 * DEPTH) ** 0.25
LN_EPS = 1e-5
RMS_EPS = 1e-5

LANES = 128
SUBLANES = 8
D_FF_PAD = 5632
FFN_TN = 512
DT_PAD = LANES
OFF_Z = 0
OFF_XS = D_INNER
OFF_BC = 2 * D_INNER
OFF_POOL = 2 * D_INNER + BC_W
OFF_GATE = OFF_POOL + D_POOL
N_MAIN = OFF_GATE + 2 * D_MODEL
VMEM_LIMIT = 56 * 1024 * 1024
NEG_BIG = -1e30


def _cparams(n_axes):
    return pltpu.CompilerParams(dimension_semantics=("arbitrary",) * n_axes,
                                vmem_limit_bytes=VMEM_LIMIT)


def _sigmoid(x):
    return 1.0 / (1.0 + jnp.exp(-x))


def _silu(x):
    return x * _sigmoid(x)


def _softplus(x):
    return jnp.maximum(x, 0.0) + jnp.log(1.0 + jnp.exp(-jnp.abs(x)))


def _layer_norm(r, g, b):
    mu = jnp.mean(r, axis=-1, keepdims=True)
    d = r - mu
    var = jnp.mean(d * d, axis=-1, keepdims=True)
    return d * lax.rsqrt(var + LN_EPS) * g + b


def _matmul_kernel(x_ref, w_ref, o_ref):
    o_ref[...] = jnp.dot(x_ref[...], w_ref[...], preferred_element_type=F32).astype(o_ref.dtype)


def _matmul(x, w, out_dtype, tm, tn):
    m, k = x.shape
    n = w.shape[1]
    return pl.pallas_call(
        _matmul_kernel,
        grid=(m // tm, n // tn),
        in_specs=[pl.BlockSpec((tm, k), lambda i, j: (i, 0)),
                  pl.BlockSpec((k, tn), lambda i, j: (0, j))],
        out_specs=pl.BlockSpec((tm, tn), lambda i, j: (i, j)),
        out_shape=jax.ShapeDtypeStruct((m, n), out_dtype),
        compiler_params=_cparams(2),
    )(x, w)


def _split3(v):
    hi = v.astype(BF16)
    r1 = v - hi.astype(F32)
    mid = r1.astype(BF16)
    lo = (r1 - mid.astype(F32)).astype(BF16)
    return hi, mid, lo


def _ssd_prompt_kernel(xs_ref, bc_ref, z_ref, dt_ref, cw_ref, cb_ref, dtb_ref, alog_ref,
                       dsk_ref, nw_ref, y_ref, st_ref,
                       ext_s, xs_s, bt_s, c_s, y_s, state_s, acol_s, dtcol_s, arow_s):
    c = pl.program_id(1)
    n_chunks = pl.num_programs(1)
    q = CHUNK

    @pl.when(c == 0)
    def _():
        ext_s[0:SUBLANES, :] = jnp.zeros((SUBLANES, CONV_DIM), F32)
        state_s[...] = jnp.zeros(state_s.shape, F32)

    ext_s[SUBLANES:SUBLANES + q, 0:D_INNER] = xs_ref[...]
    ext_s[SUBLANES:SUBLANES + q, D_INNER:CONV_DIM] = bc_ref[...]
    for t in range(CONV_DIM // GROUP_W):
        cols = slice(t * GROUP_W, (t + 1) * GROUP_W)
        acc = cb_ref[:, cols]
        for j in range(CONV_K):
            lo = SUBLANES - (CONV_K - 1) + j
            acc = acc + ext_s[lo:lo + q, cols] * cw_ref[j:j + 1, cols]
        act = _silu(acc)
        if t < N_GROUPS:
            xs_s[t] = act
        else:
            for u in range(GROUP_W // D_STATE):
                part = act[:, u * D_STATE:(u + 1) * D_STATE]
                gi = (t - N_GROUPS) * (GROUP_W // D_STATE) + u
                if gi < N_GROUPS:
                    bt_s[gi] = part.T.astype(BF16)
                else:
                    c_s[gi - N_GROUPS] = part.astype(BF16)
    ext_s[0:SUBLANES, :] = ext_s[q:q + SUBLANES, :]

    dt = _softplus(dt_ref[...] + dtb_ref[...])
    a_neg = -jnp.exp(alog_ref[...])
    d_a = dt * a_neg
    row = lax.broadcasted_iota(jnp.int32, (q, q), 0)
    col = lax.broadcasted_iota(jnp.int32, (q, q), 1)
    causal = row >= col
    tril = jnp.where(causal, 1.0, 0.0).astype(BF16)
    hi, mid, lo3 = _split3(d_a)
    a_cum = (jnp.dot(tril, hi, preferred_element_type=F32)
             + jnp.dot(tril, mid, preferred_element_type=F32)
             + jnp.dot(tril, lo3, preferred_element_type=F32))
    arow_s[...] = a_cum.T
    for g in range(N_GROUPS):
        sh = (LANES - HEADS_PER_GROUP * g) % LANES
        acol_s[g] = a_cum if sh == 0 else pltpu.roll(a_cum, sh, 1)
        dtcol_s[g] = dt if sh == 0 else pltpu.roll(dt, sh, 1)

    lane = lax.broadcasted_iota(jnp.int32, (q, LANES), 1)
    lo_half = lane < HEAD_DIM

    def group_body(g, carry):
        acol = acol_s[g]
        dtcol = dtcol_s[g]
        c_g = c_s[g]
        bt_g = bt_s[g]
        st_g = state_s[g]
        cb = jnp.dot(c_g, bt_g, preferred_element_type=F32)
        y_off_g = jnp.dot(c_g, st_g.astype(BF16), preferred_element_type=F32)
        xdtd_parts = []
        cdec_parts = []
        for k in range(HEADS_PER_GROUP // 2):
            lhs_parts = []
            a_b = []
            for e in range(2):
                hh = 2 * k + e
                a_col = jnp.broadcast_to(acol[:, hh:hh + 1], (q, q))
                a_row = arow_s[pl.ds(g * HEADS_PER_GROUP + hh, 1), :]
                seg = jnp.where(causal, a_col - a_row, NEG_BIG)
                lhs_parts.append((cb * jnp.exp(seg)).astype(BF16))
                a_b.append(a_col)
            lhs = jnp.concatenate(lhs_parts, axis=1)
            a_pair = jnp.where(lo_half, a_b[0], a_b[1])
            dt_pair = jnp.where(
                lo_half,
                jnp.broadcast_to(dtcol[:, 2 * k:2 * k + 1], (q, LANES)),
                jnp.broadcast_to(dtcol[:, 2 * k + 1:2 * k + 2], (q, LANES)))
            cols = slice(k * LANES, (k + 1) * LANES)
            xs_pair = xs_s[g, :, cols]
            xdt = xs_pair * dt_pair
            xdt_bf = xdt.astype(BF16)
            zero = jnp.zeros_like(xdt_bf)
            rhs = jnp.concatenate([jnp.where(lo_half, xdt_bf, zero),
                                   jnp.where(lo_half, zero, xdt_bf)], axis=0)
            y_diag = jnp.dot(lhs, rhs, preferred_element_type=F32)
            y_s[g, :, cols] = y_diag + y_off_g[:, cols] * jnp.exp(a_pair)
            a_last = a_pair[q - 1:q, :]
            xdtd_parts.append((xdt * jnp.exp(a_last - a_pair)).astype(BF16))
            cdec_parts.append(jnp.exp(a_last))
        xdtd = jnp.concatenate(xdtd_parts, axis=1)
        cdec = jnp.concatenate(cdec_parts, axis=1)
        state_s[g] = st_g * cdec + jnp.dot(bt_g, xdtd, preferred_element_type=F32)
        return carry

    lax.fori_loop(0, N_GROUPS, group_body, 0)

    for g in range(N_GROUPS):
        cols = slice(g * GROUP_W, (g + 1) * GROUP_W)
        yv = y_s[g] + dsk_ref[:, cols] * xs_s[g]
        v = yv * _silu(z_ref[:, cols])
        ms = jnp.mean(v * v, axis=-1, keepdims=True)
        y_ref[:, cols] = (v * lax.rsqrt(ms + RMS_EPS) * nw_ref[:, cols]).astype(y_ref.dtype)

    @pl.when(c == n_chunks - 1)
    def _():
        for g in range(N_GROUPS):
            st_ref[0, g] = state_s[g].T


def _ssd_prompt(proj, dt_raw, lw, bsz, seq):
    n_chunks = seq // CHUNK
    rows = lambda b, c: b * n_chunks + c
    small = lambda shape: pl.BlockSpec(shape, lambda b, c: (0, 0))
    y, st = pl.pallas_call(
        _ssd_prompt_kernel,
        grid=(bsz, n_chunks),
        in_specs=[
            pl.BlockSpec((CHUNK, D_INNER), lambda b, c: (rows(b, c), OFF_XS // D_INNER)),
            pl.BlockSpec((CHUNK, BC_W), lambda b, c: (rows(b, c), OFF_BC // BC_W)),
            pl.BlockSpec((CHUNK, D_INNER), lambda b, c: (rows(b, c), OFF_Z // D_INNER)),
            pl.BlockSpec((CHUNK, DT_PAD), lambda b, c: (rows(b, c), 0)),
            small((CONV_K, CONV_DIM)), small((1, CONV_DIM)), small((1, DT_PAD)),
            small((1, DT_PAD)), small((1, D_INNER)), small((1, D_INNER)),
        ],
        out_specs=[
            pl.BlockSpec((CHUNK, D_INNER), lambda b, c: (rows(b, c), 0)),
            pl.BlockSpec((1, N_GROUPS, GROUP_W, D_STATE), lambda b, c: (b, 0, 0, 0)),
        ],
        out_shape=[jax.ShapeDtypeStruct((bsz * seq, D_INNER), BF16),
                   jax.ShapeDtypeStruct((bsz, N_GROUPS, GROUP_W, D_STATE), F32)],
        scratch_shapes=[
            pltpu.VMEM((SUBLANES + CHUNK, CONV_DIM), F32),
            pltpu.VMEM((N_GROUPS, CHUNK, GROUP_W), F32),
            pltpu.VMEM((N_GROUPS, D_STATE, CHUNK), BF16),
            pltpu.VMEM((N_GROUPS, CHUNK, D_STATE), BF16),
            pltpu.VMEM((N_GROUPS, CHUNK, GROUP_W), F32),
            pltpu.VMEM((N_GROUPS, D_STATE, GROUP_W), F32),
            pltpu.VMEM((N_GROUPS, CHUNK, LANES), F32),
            pltpu.VMEM((N_GROUPS, CHUNK, LANES), F32),
            pltpu.VMEM((LANES, CHUNK), F32),
        ],
        compiler_params=_cparams(2),
    )(proj, proj, proj, dt_raw, lw["conv_w"], lw["conv_b"], lw["dt_bias"], lw["a_log"],
      lw["d_skip_x"], lw["norm_w"])
    return y, st.reshape(bsz, N_HEADS, HEAD_DIM, D_STATE)


def _ssd_sample_prep_kernel(xs_ref, bc_ref, dt_ref, cst_ref, cw_ref, cb_ref, dtb_ref, alog_ref,
                            ex_ref, xs_o, b_o, c_o, dect_o, xdtt_o):
    def conv(u, lo, hi):
        acc = cb_ref[:, lo:hi]
        for j in range(CONV_K - 1):
            acc = acc + cst_ref[j][:, lo:hi] * cw_ref[j:j + 1, lo:hi]
        acc = acc + u * cw_ref[CONV_K - 1:CONV_K, lo:hi]
        return _silu(acc)

    xs = conv(xs_ref[...], 0, D_INNER)
    bc = conv(bc_ref[...], D_INNER, CONV_DIM)
    xs_o[...] = xs
    b_o[...] = bc[:, 0:N_GROUPS * D_STATE]
    c_o[...] = bc[:, N_GROUPS * D_STATE:BC_W]
    dt = _softplus(dt_ref[...] + dtb_ref[...])
    d_a = dt * (-jnp.exp(alog_ref[...]))
    ex = ex_ref[...]

    def expand(v):
        hi, mid, lo3 = _split3(v)
        return (jnp.dot(hi, ex, preferred_element_type=F32)
                + jnp.dot(mid, ex, preferred_element_type=F32)
                + jnp.dot(lo3, ex, preferred_element_type=F32))

    dec = jnp.exp(expand(d_a))
    xdt = expand(dt) * xs
    dect_o[...] = dec.T
    xdtt_o[...] = xdt.T


def _ssd_sample_prep(proj, dt_raw, conv_state_t, lw):
    s = proj.shape[0]
    full = lambda shape: pl.BlockSpec(shape, lambda i: (0,) * len(shape))
    return pl.pallas_call(
        _ssd_sample_prep_kernel,
        grid=(1,),
        in_specs=[
            pl.BlockSpec((s, D_INNER), lambda i: (0, OFF_XS // D_INNER)),
            pl.BlockSpec((s, BC_W), lambda i: (0, OFF_BC // BC_W)),
            full((s, DT_PAD)), full((CONV_K - 1, s, CONV_DIM)),
            full((CONV_K, CONV_DIM)), full((1, CONV_DIM)), full((1, DT_PAD)), full((1, DT_PAD)),
            full((DT_PAD, D_INNER)),
        ],
        out_specs=[full((s, D_INNER)), full((s, N_GROUPS * D_STATE)), full((s, N_GROUPS * D_STATE)),
                   full((D_INNER, s)), full((D_INNER, s))],
        out_shape=[jax.ShapeDtypeStruct((s, D_INNER), F32),
                   jax.ShapeDtypeStruct((s, N_GROUPS * D_STATE), F32),
                   jax.ShapeDtypeStruct((s, N_GROUPS * D_STATE), F32),
                   jax.ShapeDtypeStruct((D_INNER, s), F32),
                   jax.ShapeDtypeStruct((D_INNER, s), F32)],
        compiler_params=_cparams(1),
    )(proj, proj, dt_raw, conv_state_t, lw["conv_w"], lw["conv_b"], lw["dt_bias"], lw["a_log"],
      lw["head_expand"])


def _ssd_sample_step_kernel(st_ref, dect_ref, xdtt_ref, b_ref, c_ref, st_o, yt_o):
    s = pl.program_id(0)
    n_s = dect_ref.shape[1]
    lane = lax.broadcasted_iota(jnp.int32, (GROUP_W, n_s), 1)
    onehot = lane == s

    @pl.when(s == 0)
    def _():
        yt_o[...] = jnp.zeros(yt_o.shape, F32)

    b_all = b_ref[pl.ds(s, 1), :]
    c_all = c_ref[pl.ds(s, 1), :]
    for g in range(N_GROUPS):
        rows = slice(g * GROUP_W, (g + 1) * GROUP_W)
        cols = slice(g * D_STATE, (g + 1) * D_STATE)
        dec = jnp.sum(jnp.where(onehot, dect_ref[rows, :], 0.0), axis=-1, keepdims=True)
        xdt = jnp.sum(jnp.where(onehot, xdtt_ref[rows, :], 0.0), axis=-1, keepdims=True)
        b_row = b_all[:, cols]
        c_row = c_all[:, cols]
        st_new = st_ref[0, rows, :] * dec + xdt * b_row
        st_o[0, rows, :] = st_new
        y_col = jnp.sum(st_new * c_row, axis=-1, keepdims=True)
        yt_o[rows, :] = yt_o[rows, :] + jnp.where(onehot, y_col, 0.0)


def _ssd_sample_step(state, dect, xdtt, b_m, c_m):
    s = state.shape[0]
    st3 = state.reshape(s, D_INNER, D_STATE)
    full = lambda shape: pl.BlockSpec(shape, lambda i: (0,) * len(shape))
    st_new, yt = pl.pallas_call(
        _ssd_sample_step_kernel,
        grid=(s,),
        in_specs=[pl.BlockSpec((1, D_INNER, D_STATE), lambda i: (i, 0, 0)),
                  full((D_INNER, s)), full((D_INNER, s)),
                  full((s, N_GROUPS * D_STATE)), full((s, N_GROUPS * D_STATE))],
        out_specs=[pl.BlockSpec((1, D_INNER, D_STATE), lambda i: (i, 0, 0)),
                   full((D_INNER, s))],
        out_shape=[jax.ShapeDtypeStruct((s, D_INNER, D_STATE), F32),
                   jax.ShapeDtypeStruct((D_INNER, s), F32)],
        compiler_params=_cparams(1),
    )(st3, dect, xdtt, b_m, c_m)
    return st_new.reshape(state.shape), yt


def _ssd_sample_finish_kernel(yt_ref, xs_ref, z_ref, dsk_ref, nw_ref, y_ref):
    y = yt_ref[...].T
    for g in range(N_GROUPS):
        cols = slice(g * GROUP_W, (g + 1) * GROUP_W)
        yv = y[:, cols] + dsk_ref[:, cols] * xs_ref[:, cols]
        v = yv * _silu(z_ref[:, cols])
        ms = jnp.mean(v * v, axis=-1, keepdims=True)
        y_ref[:, cols] = (v * lax.rsqrt(ms + RMS_EPS) * nw_ref[:, cols]).astype(y_ref.dtype)


def _ssd_sample_finish(yt, xs, proj, lw):
    s = xs.shape[0]
    full = lambda shape: pl.BlockSpec(shape, lambda i: (0,) * len(shape))
    return pl.pallas_call(
        _ssd_sample_finish_kernel,
        grid=(1,),
        in_specs=[full((D_INNER, s)), full((s, D_INNER)),
                  pl.BlockSpec((s, D_INNER), lambda i: (0, OFF_Z // D_INNER)),
                  full((1, D_INNER)), full((1, D_INNER))],
        out_specs=full((s, D_INNER)),
        out_shape=jax.ShapeDtypeStruct((s, D_INNER), BF16),
        compiler_params=_cparams(1),
    )(yt, xs, proj, lw["d_skip_x"], lw["norm_w"])


POOL_TP = 256
POOL_HALO = 16


def _pool_prompt_kernel(u_ref, o_ref, ext_s):
    c = pl.program_id(1)
    tp = POOL_TP

    @pl.when(c == 0)
    def _():
        ext_s[0:POOL_HALO, :] = jnp.zeros((POOL_HALO, D_POOL), F32)

    ext_s[POOL_HALO:POOL_HALO + tp, :] = u_ref[...]
    pos = c * tp + lax.broadcasted_iota(jnp.int32, (tp, 1), 0)
    for gi, win in enumerate(POOL_WINDOWS):
        cols = slice(gi * POOL_GC, (gi + 1) * POOL_GC)
        u = ext_s[POOL_HALO:POOL_HALO + tp, cols]
        tot = u
        for k in range(1, win):
            tot = tot + ext_s[POOL_HALO - k:POOL_HALO - k + tp, cols]
        cnt = jnp.minimum(win, pos + 1).astype(F32)
        o_ref[:, cols] = (tot / cnt - u).astype(o_ref.dtype)
    ext_s[0:POOL_HALO, :] = ext_s[tp:tp + POOL_HALO, :]


def _pool_prompt(proj, bsz, seq):
    nt = seq // POOL_TP
    return pl.pallas_call(
        _pool_prompt_kernel,
        grid=(bsz, nt),
        in_specs=[pl.BlockSpec((POOL_TP, D_POOL), lambda b, c: (b * nt + c, OFF_POOL // D_POOL))],
        out_specs=pl.BlockSpec((POOL_TP, D_POOL), lambda b, c: (b * nt + c, 0)),
        out_shape=jax.ShapeDtypeStruct((bsz * seq, D_POOL), BF16),
        scratch_shapes=[pltpu.VMEM((POOL_HALO + POOL_TP, D_POOL), F32)],
        compiler_params=_cparams(2),
    )(proj)


def _pool_sample_kernel(u_ref, buf_ref, o_ref):
    for gi, win in enumerate(POOL_WINDOWS):
        cols = slice(gi * POOL_GC, (gi + 1) * POOL_GC)
        u = u_ref[:, cols]
        tot = u
        for k in range(1, win):
            tot = tot + buf_ref[POOL_BUF - k][:, cols]
        cnt = float(min(win, PAST_LEN + 1))
        o_ref[:, cols] = (tot / cnt - u).astype(o_ref.dtype)


def _pool_sample(proj, pool_state_t):
    s = proj.shape[0]
    return pl.pallas_call(
        _pool_sample_kernel,
        grid=(1,),
        in_specs=[pl.BlockSpec((s, D_POOL), lambda i: (0, OFF_POOL // D_POOL)),
                  pl.BlockSpec((POOL_BUF, s, D_POOL), lambda i: (0, 0, 0))],
        out_specs=pl.BlockSpec((s, D_POOL), lambda i: (0, 0)),
        out_shape=jax.ShapeDtypeStruct((s, D_POOL), BF16),
        compiler_params=_cparams(1),
    )(proj, pool_state_t)


def _pool_mm_kernel(p_ref, w_ref, sc_ref, o_ref):
    acc = jnp.dot(p_ref[...], w_ref[0], preferred_element_type=F32)
    o_ref[...] = acc * sc_ref[...]


def _pool_mm(pooled, w_pool, scale, tm):
    m = pooled.shape[0]
    ng = len(POOL_WINDOWS)
    return pl.pallas_call(
        _pool_mm_kernel,
        grid=(m // tm, ng),
        in_specs=[pl.BlockSpec((tm, POOL_GC), lambda i, g: (i, g)),
                  pl.BlockSpec((1, POOL_GC, POOL_GC), lambda i, g: (g, 0, 0)),
                  pl.BlockSpec((1, POOL_GC), lambda i, g: (0, g))],
        out_specs=pl.BlockSpec((tm, POOL_GC), lambda i, g: (i, g)),
        out_shape=jax.ShapeDtypeStruct((m, D_POOL), F32),
        compiler_params=_cparams(2),
    )(pooled, w_pool, scale)


def _branch_merge_kernel(y_ref, w_ref, ga_ref, gb_ref, ba_ref, bb_ref, yb_ref, o_ref):
    y_a = jnp.dot(y_ref[...], w_ref[...], preferred_element_type=F32)
    g_a = _sigmoid(ga_ref[...] + ba_ref[...])
    g_b = _sigmoid(gb_ref[...] + bb_ref[...])
    o_ref[...] = (g_a * y_a + g_b * yb_ref[...]).astype(o_ref.dtype)


def _branch_merge(y, w_br, proj, b_gate, y_b, tm, tn):
    m = y.shape[0]
    nj = D_MODEL // tn
    ga0 = OFF_GATE // tn
    return pl.pallas_call(
        _branch_merge_kernel,
        grid=(m // tm, nj),
        in_specs=[pl.BlockSpec((tm, D_INNER), lambda i, j: (i, 0)),
                  pl.BlockSpec((D_INNER, tn), lambda i, j: (0, j)),
                  pl.BlockSpec((tm, tn), lambda i, j: (i, ga0 + j)),
                  pl.BlockSpec((tm, tn), lambda i, j: (i, ga0 + nj + j)),
                  pl.BlockSpec((1, tn), lambda i, j: (0, j)),
                  pl.BlockSpec((1, tn), lambda i, j: (0, nj + j)),
                  pl.BlockSpec((tm, tn), lambda i, j: (i, j))],
        out_specs=pl.BlockSpec((tm, tn), lambda i, j: (i, j)),
        out_shape=jax.ShapeDtypeStruct((m, D_MODEL), BF16),
        compiler_params=_cparams(2),
    )(y, w_br, proj, proj, b_gate, b_gate, y_b)


def _proj_ln_kernel(m_ref, w_ref, x_ref, g_ref, b_ref, o_ref, obf_ref):
    acc = jnp.dot(m_ref[...], w_ref[...], preferred_element_type=F32)
    y = _layer_norm(ALPHA * x_ref[...] + acc, g_ref[...], b_ref[...])
    o_ref[...] = y
    obf_ref[...] = y.astype(BF16)


def _proj_ln(mix, w, x, g, b, tm):
    m, k = mix.shape
    return pl.pallas_call(
        _proj_ln_kernel,
        grid=(m // tm,),
        in_specs=[pl.BlockSpec((tm, k), lambda i: (i, 0)),
                  pl.BlockSpec((k, D_MODEL), lambda i: (0, 0)),
                  pl.BlockSpec((tm, D_MODEL), lambda i: (i, 0)),
                  pl.BlockSpec((1, D_MODEL), lambda i: (0, 0)),
                  pl.BlockSpec((1, D_MODEL), lambda i: (0, 0))],
        out_specs=[pl.BlockSpec((tm, D_MODEL), lambda i: (i, 0)),
                   pl.BlockSpec((tm, D_MODEL), lambda i: (i, 0))],
        out_shape=[jax.ShapeDtypeStruct((m, D_MODEL), F32),
                   jax.ShapeDtypeStruct((m, D_MODEL), BF16)],
        compiler_params=_cparams(1),
    )(mix, w, x, g, b)


FFN_HALO = SUBLANES


def _ffn_conv_gate(hg_s, hv_s, cw_g, cw_v, cb_g, cb_v, tm):
    def conv(h_s, cw, cb):
        acc = cb[...]
        for j in range(FFN_K):
            lo = FFN_HALO - (FFN_K - 1) + j
            acc = acc + h_s[lo:lo + tm, :] * cw[j:j + 1, :]
        return acc
    return _silu(conv(hg_s, cw_g, cb_g)) * conv(hv_s, cw_v, cb_v)


def _ffn_up_prompt_kernel(x_ref, wg_ref, wv_ref, cwg_ref, cwv_ref, cbg_ref, cbv_ref,
                          a_ref, tg_ref, tv_ref, hg_s, hv_s, *, tiles_per_seq):
    i = pl.program_id(1)
    tm = x_ref.shape[0]

    @pl.when(i % tiles_per_seq == 0)
    def _():
        hg_s[0:FFN_HALO, :] = jnp.zeros((FFN_HALO, FFN_TN), F32)
        hv_s[0:FFN_HALO, :] = jnp.zeros((FFN_HALO, FFN_TN), F32)

    x = x_ref[...]
    hg_s[FFN_HALO:FFN_HALO + tm, :] = jnp.dot(x, wg_ref[...], preferred_element_type=F32)
    hv_s[FFN_HALO:FFN_HALO + tm, :] = jnp.dot(x, wv_ref[...], preferred_element_type=F32)
    a_ref[...] = _ffn_conv_gate(hg_s, hv_s, cwg_ref, cwv_ref, cbg_ref, cbv_ref, tm).astype(a_ref.dtype)
    tail_g = hg_s[tm:tm + FFN_HALO, :]
    tail_v = hv_s[tm:tm + FFN_HALO, :]
    hg_s[0:FFN_HALO, :] = tail_g
    hv_s[0:FFN_HALO, :] = tail_v
    tg_ref[0] = tail_g
    tv_ref[0] = tail_v


def _ffn_up_prompt(x_bf, lw, seq, tm):
    m = x_bf.shape[0]
    nj = D_FF_PAD // FFN_TN
    ni = m // tm
    kern = functools.partial(_ffn_up_prompt_kernel, tiles_per_seq=seq // tm)
    wspec_g = pl.BlockSpec((D_MODEL, FFN_TN), lambda j, i: (0, j))
    wspec_v = pl.BlockSpec((D_MODEL, FFN_TN), lambda j, i: (0, nj + j))
    cspec_g = lambda r: pl.BlockSpec((r, FFN_TN), lambda j, i: (0, j))
    cspec_v = lambda r: pl.BlockSpec((r, FFN_TN), lambda j, i: (0, nj + j))
    return pl.pallas_call(
        kern,
        grid=(nj, ni),
        in_specs=[pl.BlockSpec((tm, D_MODEL), lambda j, i: (i, 0)),
                  wspec_g, wspec_v, cspec_g(FFN_K), cspec_v(FFN_K), cspec_g(1), cspec_v(1)],
        out_specs=[pl.BlockSpec((tm, FFN_TN), lambda j, i: (i, j)),
                   pl.BlockSpec((1, FFN_HALO, FFN_TN), lambda j, i: (i, 0, j)),
                   pl.BlockSpec((1, FFN_HALO, FFN_TN), lambda j, i: (i, 0, j))],
        out_shape=[jax.ShapeDtypeStruct((m, D_FF_PAD), BF16),
                   jax.ShapeDtypeStruct((ni, FFN_HALO, D_FF_PAD), F32),
                   jax.ShapeDtypeStruct((ni, FFN_HALO, D_FF_PAD), F32)],
        scratch_shapes=[pltpu.VMEM((FFN_HALO + tm, FFN_TN), F32),
                        pltpu.VMEM((FFN_HALO + tm, FFN_TN), F32)],
        compiler_params=_cparams(2),
    )(x_bf, lw["w_up"], lw["w_up"], lw["fconv_w"], lw["fconv_w"], lw["fconv_b"], lw["fconv_b"])


def _ffn_up_sample_kernel(x_ref, wg_ref, wv_ref, sg_ref, sv_ref, cwg_ref, cwv_ref, cbg_ref, cbv_ref,
                          a_ref, hg_ref, hv_ref):
    x = x_ref[...]
    hg = jnp.dot(x, wg_ref[...], preferred_element_type=F32)
    hv = jnp.dot(x, wv_ref[...], preferred_element_type=F32)
    hg_ref[...] = hg
    hv_ref[...] = hv

    def conv(h, st, cw, cb):
        acc = cb[...]
        for j in range(FFN_K - 1):
            acc = acc + st[j] * cw[j:j + 1, :]
        return acc + h * cw[FFN_K - 1:FFN_K, :]

    a_ref[...] = (_silu(conv(hg, sg_ref, cwg_ref, cbg_ref))
                  * conv(hv, sv_ref, cwv_ref, cbv_ref)).astype(a_ref.dtype)


def _ffn_up_sample(x_bf, ffn_state_t, lw):
    s = x_bf.shape[0]
    nj = D_FF_PAD // FFN_TN
    g_blk = lambda r: pl.BlockSpec((r, FFN_TN), lambda j: (0, j))
    v_blk = lambda r: pl.BlockSpec((r, FFN_TN), lambda j: (0, nj + j))
    return pl.pallas_call(
        _ffn_up_sample_kernel,
        grid=(nj,),
        in_specs=[pl.BlockSpec((s, D_MODEL), lambda j: (0, 0)),
                  g_blk(D_MODEL), v_blk(D_MODEL),
                  pl.BlockSpec((FFN_K - 1, s, FFN_TN), lambda j: (0, 0, j)),
                  pl.BlockSpec((FFN_K - 1, s, FFN_TN), lambda j: (0, 0, nj + j)),
                  g_blk(FFN_K), v_blk(FFN_K), g_blk(1), v_blk(1)],
        out_specs=[pl.BlockSpec((s, FFN_TN), lambda j: (0, j)),
                   pl.BlockSpec((s, FFN_TN), lambda j: (0, j)),
                   pl.BlockSpec((s, FFN_TN), lambda j: (0, j))],
        out_shape=[jax.ShapeDtypeStruct((s, D_FF_PAD), BF16),
                   jax.ShapeDtypeStruct((s, D_FF_PAD), F32),
                   jax.ShapeDtypeStruct((s, D_FF_PAD), F32)],
        compiler_params=_cparams(1),
    )(x_bf, lw["w_up"], lw["w_up"], ffn_state_t, ffn_state_t,
      lw["fconv_w"], lw["fconv_w"], lw["fconv_b"], lw["fconv_b"])


DOWN_TK = 512


def _down_ln_kernel(a_ref, w_ref, x_ref, g_ref, b_ref, o_ref, obf_ref, acc_s):
    k = pl.program_id(1)

    @pl.when(k == 0)
    def _():
        acc_s[...] = ALPHA * x_ref[...]

    acc_s[...] += jnp.dot(a_ref[...], w_ref[...], preferred_element_type=F32)

    @pl.when(k == pl.num_programs(1) - 1)
    def _():
        y = _layer_norm(acc_s[...], g_ref[...], b_ref[...])
        o_ref[...] = y
        obf_ref[...] = y.astype(BF16)


def _down_ln(a, w, x, g, b, tm):
    m = a.shape[0]
    nk = D_FF_PAD // DOWN_TK
    return pl.pallas_call(
        _down_ln_kernel,
        grid=(m // tm, nk),
        in_specs=[pl.BlockSpec((tm, DOWN_TK), lambda i, k: (i, k)),
                  pl.BlockSpec((DOWN_TK, D_MODEL), lambda i, k: (k, 0)),
                  pl.BlockSpec((tm, D_MODEL), lambda i, k: (i, 0)),
                  pl.BlockSpec((1, D_MODEL), lambda i, k: (0, 0)),
                  pl.BlockSpec((1, D_MODEL), lambda i, k: (0, 0))],
        out_specs=[pl.BlockSpec((tm, D_MODEL), lambda i, k: (i, 0)),
                   pl.BlockSpec((tm, D_MODEL), lambda i, k: (i, 0))],
        out_shape=[jax.ShapeDtypeStruct((m, D_MODEL), F32),
                   jax.ShapeDtypeStruct((m, D_MODEL), BF16)],
        scratch_shapes=[pltpu.VMEM((tm, D_MODEL), F32)],
        compiler_params=_cparams(2),
    )(a, w, x, g, b)


def _pad_ff(v):
    pad = [(0, 0)] * (v.ndim - 1) + [(0, D_FF_PAD - D_FF)]
    return jnp.concatenate([jnp.pad(v[..., :D_FF], pad), jnp.pad(v[..., D_FF:], pad)], axis=-1)


def _unpad_ff(v):
    return jnp.concatenate([v[..., :D_FF], v[..., D_FF_PAD:D_FF_PAD + D_FF]], axis=-1)


def _prep_layer(w_in, b_gate, conv_w, conv_b, dt_bias, a_log, d_skip, norm_w, w_br, w_pool,
                pool_scale, w_out, ln1_g, ln1_b, w_up, fconv_w, fconv_b, w_down, ln2_g, ln2_b):
    o_xbc = D_INNER
    o_dt = o_xbc + CONV_DIM
    o_pool = o_dt + N_HEADS
    o_gate = o_pool + D_POOL
    w_main = jnp.concatenate([w_in[:, :o_dt], w_in[:, o_pool:]], axis=1).astype(BF16)
    w_dt = jnp.pad(w_in[:, o_dt:o_pool], ((0, 0), (0, DT_PAD - N_HEADS))).astype(BF16)
    pad_h = lambda v: jnp.pad(v, (0, DT_PAD - N_HEADS)).reshape(1, DT_PAD)
    head_of_channel = jnp.arange(D_INNER) // HEAD_DIM
    head_expand = (jnp.arange(DT_PAD)[:, None] == head_of_channel[None, :]).astype(BF16)
    return dict(
        w_main=w_main, w_dt=w_dt, b_gate=b_gate.reshape(1, -1),
        conv_w=conv_w, conv_b=conv_b.reshape(1, -1),
        dt_bias=pad_h(dt_bias), a_log=pad_h(a_log),
        d_skip_x=jnp.repeat(d_skip, HEAD_DIM).reshape(1, -1),
        norm_w=norm_w.reshape(1, -1), head_expand=head_expand,
        w_br=w_br.astype(BF16), w_pool=w_pool.astype(BF16), pool_scale=pool_scale.reshape(1, -1),
        w_out=w_out.astype(BF16), ln1_g=ln1_g.reshape(1, -1), ln1_b=ln1_b.reshape(1, -1),
        w_up=_pad_ff(w_up).astype(BF16), fconv_w=_pad_ff(fconv_w),
        fconv_b=_pad_ff(fconv_b).reshape(1, -1),
        w_down=jnp.pad(w_down, ((0, D_FF_PAD - D_FF), (0, 0))).astype(BF16),
        ln2_g=ln2_g.reshape(1, -1), ln2_b=ln2_b.reshape(1, -1),
    )


def _raw_xbc(proj):
    return proj[:, OFF_XS:OFF_XS + CONV_DIM]


def _layer_prompt(x, x_bf, lw, bsz, seq):
    proj = _matmul(x_bf, lw["w_main"], F32, 1024, 1024)
    dt_raw = _matmul(x_bf, lw["w_dt"], F32, 1024, DT_PAD)
    y, new_ssm = _ssd_prompt(proj, dt_raw, lw, bsz, seq)
    pooled = _pool_prompt(proj, bsz, seq)
    y_b = _pool_mm(pooled, lw["w_pool"], lw["pool_scale"], 1024)
    mix = _branch_merge(y, lw["w_br"], proj, lw["b_gate"], y_b, 512, 512)
    x1, x1_bf = _proj_ln(mix, lw["w_out"], x, lw["ln1_g"], lw["ln1_b"], 256)
    tm_up = 1024
    act, tail_g, tail_v = _ffn_up_prompt(x1_bf, lw, seq, tm_up)
    x2, x2_bf = _down_ln(act, lw["w_down"], x1, lw["ln2_g"], lw["ln2_b"], 512)
    p3 = proj.reshape(bsz, seq, N_MAIN)
    new_conv = p3[:, seq - (CONV_K - 1):, OFF_XS:OFF_XS + CONV_DIM]
    new_pool = p3[:, seq - POOL_BUF:, OFF_POOL:OFF_POOL + D_POOL]
    tps = seq // tm_up
    last = slice(tps - 1, None, tps)
    tail = jnp.concatenate([tail_g[last, :, :D_FF], tail_v[last, :, :D_FF]], axis=-1)
    new_ffn = tail[:, FFN_HALO - (FFN_K - 1):, :]
    return x2, x2_bf, new_ssm, new_conv, new_pool, new_ffn


def _layer_sample(x, x_bf, s_ssm, s_conv, s_pool, s_ffn, lw):
    s = x.shape[0]
    proj = _matmul(x_bf, lw["w_main"], F32, s, 1024)
    dt_raw = _matmul(x_bf, lw["w_dt"], F32, s, DT_PAD)
    xs, b_m, c_m, dect, xdtt = _ssd_sample_prep(proj, dt_raw, jnp.swapaxes(s_conv, 0, 1), lw)
    new_ssm, yt = _ssd_sample_step(s_ssm, dect, xdtt, b_m, c_m)
    y = _ssd_sample_finish(yt, xs, proj, lw)
    pooled = _pool_sample(proj, jnp.swapaxes(s_pool, 0, 1))
    y_b = _pool_mm(pooled, lw["w_pool"], lw["pool_scale"], s)
    mix = _branch_merge(y, lw["w_br"], proj, lw["b_gate"], y_b, s, 512)
    x1, x1_bf = _proj_ln(mix, lw["w_out"], x, lw["ln1_g"], lw["ln1_b"], s)
    act, h_g, h_v = _ffn_up_sample(x1_bf, _pad_ff(jnp.swapaxes(s_ffn, 0, 1)), lw)
    x2, x2_bf = _down_ln(act, lw["w_down"], x1, lw["ln2_g"], lw["ln2_b"], s)
    new_conv = jnp.concatenate([s_conv[:, 1:], _raw_xbc(proj)[:, None, :]], axis=1)
    new_pool = jnp.concatenate([s_pool[:, 1:], proj[:, None, OFF_POOL:OFF_POOL + D_POOL]], axis=1)
    h_new = jnp.concatenate([h_g[:, :D_FF], h_v[:, :D_FF]], axis=-1)
    new_ffn = jnp.concatenate([s_ffn[:, 1:], h_new[:, None, :]], axis=1)
    return x2, x2_bf, new_ssm, new_conv, new_pool, new_ffn


def kernel(x_prompt, x_sample, state_ssm, state_ssd_conv, state_pool, state_ffn_conv, w_in, b_gate, conv_w, conv_b, dt_bias, a_log, d_skip, ssd_norm_w, w_ssd_branch, w_pool, pool_scale, w_out, ln1_g, ln1_b, w_up, ffn_conv_w, ffn_conv_b, w_down, ln2_g, ln2_b):
    bsz, seq, _ = x_prompt.shape
    n_s = x_sample.shape[0]
    assert x_sample.shape[1] == 1 and seq % 1024 == 0
    xp = x_prompt.reshape(bsz * seq, D_MODEL)
    xs = x_sample.reshape(n_s, D_MODEL)
    xp_bf, xs_bf = xp.astype(BF16), xs.astype(BF16)
    outs_p, outs_s = [], []
    for i in range(DEPTH):
        lw = _prep_layer(w_in[i], b_gate[i], conv_w[i], conv_b[i], dt_bias[i], a_log[i], d_skip[i],
                         ssd_norm_w[i], w_ssd_branch[i], w_pool[i], pool_scale[i], w_out[i],
                         ln1_g[i], ln1_b[i], w_up[i], ffn_conv_w[i], ffn_conv_b[i], w_down[i],
                         ln2_g[i], ln2_b[i])
        xp, xp_bf, *op = _layer_prompt(xp, xp_bf, lw, bsz, seq)
        xs, xs_bf, *os_ = _layer_sample(xs, xs_bf, state_ssm[i], state_ssd_conv[i], state_pool[i],
                                        state_ffn_conv[i], lw)
        outs_p.append(op)
        outs_s.append(os_)
    stack = lambda outs, k: jnp.stack([o[k] for o in outs])
    return (xp.reshape(bsz, seq, D_MODEL), xs.reshape(n_s, 1, D_MODEL),
            stack(outs_p, 0), stack(outs_p, 1), stack(outs_p, 2), stack(outs_p, 3),
            stack(outs_s, 0), stack(outs_s, 1), stack(outs_s, 2), stack(outs_s, 3))
```

```python
import functools

import jax
import jax.numpy as jnp
from jax import lax
from jax.experimental import pallas as pl
from jax.experimental.pallas import tpu as pltpu

F32 = jnp.float32
BF16 = jnp.bfloat16

D_MODEL = 2048
HEAD_DIM = 64
D_INNER = 2 * D_MODEL
N_HEADS = D_INNER // HEAD_DIM
N_GROUPS = 8
HEADS_PER_GROUP = N_HEADS // N_GROUPS
GROUP_W = D_INNER // N_GROUPS
D_STATE = 128
CONV_K = 4
BC_W = 2 * N_GROUPS * D_STATE
CONV_DIM = D_INNER + BC_W
CHUNK = 128
D_POOL = D_MODEL
POOL_WINDOWS = (2, 4, 8, 16)
POOL_GC = D_POOL // len(POOL_WINDOWS)
POOL_BUF = max(POOL_WINDOWS) - 1
D_FF = 5504
FFN_K = 3
DEPTH = 2
PAST_LEN = 16384
ALPHA = (2 * DEPTH) ** 0.25
LN_EPS = 1e-5
RMS_EPS = 1e-5

LANES = 128
SUBLANES = 8
D_FF_PAD = 5632
FFN_TN = 512
DT_PAD = LANES
OFF_Z = 0
OFF_XS = D_INNER
OFF_BC = 2 * D_INNER
OFF_POOL = 2 * D_INNER + BC_W
OFF_GATE = OFF_POOL + D_POOL
N_MAIN = OFF_GATE + 2 * D_MODEL
VMEM_LIMIT = 56 * 1024 * 1024
NEG_BIG = -1e30


def _cparams(n_axes):
    return pltpu.CompilerParams(dimension_semantics=("arbitrary",) * n_axes,
                                vmem_limit_bytes=VMEM_LIMIT)


def _sigmoid(x):
    return 1.0 / (1.0 + jnp.exp(-x))


def _silu(x):
    return x * _sigmoid(x)


def _softplus(x):
    return jnp.maximum(x, 0.0) + jnp.log(1.0 + jnp.exp(-jnp.abs(x)))


def _layer_norm(r, g, b):
    mu = jnp.mean(r, axis=-1, keepdims=True)
    d = r - mu
    var = jnp.mean(d * d, axis=-1, keepdims=True)
    return d * lax.rsqrt(var + LN_EPS) * g + b


def _matmul_kernel(x_ref, w_ref, o_ref):
    o_ref[...] = jnp.dot(x_ref[...], w_ref[...], preferred_element_type=F32).astype(o_ref.dtype)


def _matmul(x, w, out_dtype, tm, tn):
    m, k = x.shape
    n = w.shape[1]
    return pl.pallas_call(
        _matmul_kernel,
        grid=(m // tm, n // tn),
        in_specs=[pl.BlockSpec((tm, k), lambda i, j: (i, 0)),
                  pl.BlockSpec((k, tn), lambda i, j: (0, j))],
        out_specs=pl.BlockSpec((tm, tn), lambda i, j: (i, j)),
        out_shape=jax.ShapeDtypeStruct((m, n), out_dtype),
        compiler_params=_cparams(2),
        name="matmul",
    )(x, w)


IN_TN = 1024
J_XBC = OFF_XS // IN_TN
J_POOL = OFF_POOL // IN_TN
J_GATE = OFF_GATE // IN_TN
J_END = N_MAIN // IN_TN
CONV_HALO = SUBLANES


def _in_proj_kernel(x_ref, wa_ref, wb_ref, cw_ref, cb_ref, bg_ref, o_ref, tail_ref, w_s, h_s,
                    *, tiles_per_seq, conv):
    j = pl.program_id(0)
    i = pl.program_id(1)
    tm = x_ref.shape[0]

    @pl.when((i == 0) & (j < J_POOL))
    def _():
        w_s[...] = wa_ref[...].astype(BF16)

    @pl.when((i == 0) & (j >= J_POOL))
    def _():
        w_s[...] = wb_ref[...].astype(BF16)

    def mm():
        return jnp.dot(x_ref[...], w_s[...], preferred_element_type=F32)

    is_xbc = (j >= J_XBC) & (j < J_POOL)

    @pl.when(jnp.logical_not(is_xbc))
    def _():
        tail_ref[0] = jnp.zeros((CONV_HALO, IN_TN), F32)

    @pl.when(j < J_XBC)
    def _():
        o_ref[...] = _silu(mm())

    @pl.when(is_xbc)
    def _():
        if conv:
            @pl.when(i % tiles_per_seq == 0)
            def _():
                h_s[0:CONV_HALO, :] = jnp.zeros((CONV_HALO, IN_TN), F32)

            h_s[CONV_HALO:CONV_HALO + tm, :] = mm()
            acc = cb_ref[...]
            for k in range(CONV_K):
                lo = CONV_HALO - (CONV_K - 1) + k
                acc = acc + h_s[lo:lo + tm, :] * cw_ref[k:k + 1, :]
            o_ref[...] = _silu(acc)
            tail = h_s[tm:tm + CONV_HALO, :]
            h_s[0:CONV_HALO, :] = tail
            tail_ref[0] = tail
        else:
            o_ref[...] = mm()
            tail_ref[0] = jnp.zeros((CONV_HALO, IN_TN), F32)

    @pl.when((j >= J_POOL) & (j < J_GATE))
    def _():
        o_ref[...] = mm()

    @pl.when(j >= J_GATE)
    def _():
        o_ref[...] = _sigmoid(mm() + bg_ref[...])


def _in_proj(x_bf, lw, tm, tiles_per_seq, conv):
    m = x_bf.shape[0]
    ni = m // tm
    n_conv_tiles = CONV_DIM // IN_TN
    clamp = lambda v, lo, hi: jnp.minimum(jnp.maximum(v, lo), hi)
    conv_tile = lambda j: clamp(j - J_XBC, 0, n_conv_tiles - 1)
    kern = functools.partial(_in_proj_kernel, tiles_per_seq=tiles_per_seq, conv=conv)
    return pl.pallas_call(
        kern,
        grid=(J_END, ni),
        in_specs=[pl.BlockSpec((tm, D_MODEL), lambda j, i: (i, 0)),
                  pl.BlockSpec((D_MODEL, IN_TN), lambda j, i: (0, jnp.minimum(j, J_POOL - 1))),
                  pl.BlockSpec((D_MODEL, IN_TN), lambda j, i: (0, jnp.maximum(j - J_POOL, 0)),
                               pipeline_mode=pl.Buffered(1)),
                  pl.BlockSpec((CONV_K, IN_TN), lambda j, i: (0, conv_tile(j))),
                  pl.BlockSpec((1, IN_TN), lambda j, i: (0, conv_tile(j))),
                  pl.BlockSpec((1, IN_TN), lambda j, i: (0, clamp(j - J_GATE, 0, J_END - J_GATE - 1)))],
        out_specs=[pl.BlockSpec((tm, IN_TN), lambda j, i: (i, j)),
                   pl.BlockSpec((1, CONV_HALO, IN_TN), lambda j, i: (i, 0, j))],
        out_shape=[jax.ShapeDtypeStruct((m, N_MAIN), F32),
                   jax.ShapeDtypeStruct((ni, CONV_HALO, N_MAIN), F32)],
        scratch_shapes=[pltpu.VMEM((D_MODEL, IN_TN), BF16),
                        pltpu.VMEM((CONV_HALO + tm, IN_TN), F32)],
        compiler_params=_cparams(2),
        name="in_proj",
    )(x_bf, lw["w_in"], lw["w_pg"], lw["conv_w"], lw["conv_b"], lw["b_gate"])


def _split3(v):
    hi = v.astype(BF16)
    r1 = v - hi.astype(F32)
    mid = r1.astype(BF16)
    lo = (r1 - mid.astype(F32)).astype(BF16)
    return hi, mid, lo


def _ssd_prompt_kernel(xs_ref, bc_ref, zs_ref, dt_ref, dtb_ref, alog_ref, dsk_ref, nw_ref,
                       y_ref, st_ref,
                       xs_s, bt_s, c_s, y_s, state_s, acol_s, arow_s, dtrow_s):
    c = pl.program_id(1)
    n_chunks = pl.num_programs(1)
    q = CHUNK

    @pl.when(c == 0)
    def _():
        state_s[...] = jnp.zeros(state_s.shape, F32)

    for g in range(N_GROUPS):
        xs_s[g] = xs_ref[:, g * GROUP_W:(g + 1) * GROUP_W]
        bt_s[g] = bc_ref[:, g * D_STATE:(g + 1) * D_STATE].T
        c_s[g] = bc_ref[:, (N_GROUPS + g) * D_STATE:(N_GROUPS + g + 1) * D_STATE].astype(BF16)

    dt = _softplus(dt_ref[...] + dtb_ref[...])
    a_neg = -jnp.exp(alog_ref[...])
    d_a = dt * a_neg
    row = lax.broadcasted_iota(jnp.int32, (q, q), 0)
    col = lax.broadcasted_iota(jnp.int32, (q, q), 1)
    causal = row >= col
    tril = jnp.where(causal, 1.0, 0.0).astype(BF16)
    hi, mid, lo3 = _split3(d_a)
    a_cum = (jnp.dot(tril, hi, preferred_element_type=F32)
             + jnp.dot(tril, mid, preferred_element_type=F32)
             + jnp.dot(tril, lo3, preferred_element_type=F32))
    arow_s[...] = a_cum.T
    dtrow_s[...] = dt.T
    for g in range(N_GROUPS):
        sh = (LANES - HEADS_PER_GROUP * g) % LANES
        acol_s[g] = a_cum if sh == 0 else pltpu.roll(a_cum, sh, 1)

    lane = lax.broadcasted_iota(jnp.int32, (q, LANES), 1)
    lo_half = lane < HEAD_DIM

    def group_body(g, carry):
        acol = acol_s[g]
        c_g = c_s[g]
        bt_g = bt_s[g]
        cb = jnp.dot(c_g, bt_g.astype(BF16), preferred_element_type=F32)
        y_off_g = jnp.dot(c_g, state_s[g].astype(BF16), preferred_element_type=F32)
        for k in range(HEADS_PER_GROUP // 2):
            l_parts, b_parts, a_b = [], [], []
            for e in range(2):
                hh = 2 * k + e
                head = g * HEADS_PER_GROUP + hh
                a_col = jnp.broadcast_to(acol[:, hh:hh + 1], (q, q))
                a_row = arow_s[pl.ds(head, 1), :]
                dt_row = dtrow_s[pl.ds(head, 1), :]
                seg = jnp.where(causal, a_col - a_row, NEG_BIG)
                l_parts.append((cb * jnp.exp(seg) * dt_row).astype(BF16))
                w_row = dt_row * jnp.exp(a_col[q - 1:q, :] - a_row)
                b_parts.append((bt_g * w_row).astype(BF16))
                a_b.append(a_col)
            lhs = jnp.concatenate([jnp.concatenate(l_parts, axis=1),
                                   jnp.concatenate(b_parts, axis=1)], axis=0)
            cols = slice(k * LANES, (k + 1) * LANES)
            xs_bf = xs_s[g, :, cols].astype(BF16)
            zero = jnp.zeros_like(xs_bf)
            rhs = jnp.concatenate([jnp.where(lo_half, xs_bf, zero),
                                   jnp.where(lo_half, zero, xs_bf)], axis=0)
            res = jnp.dot(lhs, rhs, preferred_element_type=F32)
            a_pair = jnp.where(lo_half, a_b[0], a_b[1])
            y_s[g, :, cols] = res[0:q] + y_off_g[:, cols] * jnp.exp(a_pair)
            cdec = jnp.exp(a_pair[q - 1:q, :])
            state_s[g, :, cols] = state_s[g, :, cols] * cdec + res[q:2 * q]
        return carry

    lax.fori_loop(0, N_GROUPS, group_body, 0)

    for g in range(N_GROUPS):
        cols = slice(g * GROUP_W, (g + 1) * GROUP_W)
        yv = y_s[g] + dsk_ref[:, cols] * xs_s[g]
        v = yv * zs_ref[:, cols]
        ms = jnp.mean(v * v, axis=-1, keepdims=True)
        y_ref[:, cols] = (v * lax.rsqrt(ms + RMS_EPS) * nw_ref[:, cols]).astype(y_ref.dtype)

    @pl.when(c == n_chunks - 1)
    def _():
        for g in range(N_GROUPS):
            st_ref[0, g] = state_s[g].T


def _ssd_prompt(proj, dt_raw, lw, bsz, seq):
    n_chunks = seq // CHUNK
    rows = lambda b, c: b * n_chunks + c
    small = lambda shape: pl.BlockSpec(shape, lambda b, c: (0, 0))
    y, st = pl.pallas_call(
        _ssd_prompt_kernel,
        grid=(bsz, n_chunks),
        in_specs=[
            pl.BlockSpec((CHUNK, D_INNER), lambda b, c: (rows(b, c), OFF_XS // D_INNER)),
            pl.BlockSpec((CHUNK, BC_W), lambda b, c: (rows(b, c), OFF_BC // BC_W)),
            pl.BlockSpec((CHUNK, D_INNER), lambda b, c: (rows(b, c), OFF_Z // D_INNER)),
            pl.BlockSpec((CHUNK, DT_PAD), lambda b, c: (rows(b, c), 0)),
            small((1, DT_PAD)), small((1, DT_PAD)), small((1, D_INNER)), small((1, D_INNER)),
        ],
        out_specs=[
            pl.BlockSpec((CHUNK, D_INNER), lambda b, c: (rows(b, c), 0)),
            pl.BlockSpec((1, N_GROUPS, GROUP_W, D_STATE), lambda b, c: (b, 0, 0, 0)),
        ],
        out_shape=[jax.ShapeDtypeStruct((bsz * seq, D_INNER), BF16),
                   jax.ShapeDtypeStruct((bsz, N_GROUPS, GROUP_W, D_STATE), F32)],
        scratch_shapes=[
            pltpu.VMEM((N_GROUPS, CHUNK, GROUP_W), F32),
            pltpu.VMEM((N_GROUPS, D_STATE, CHUNK), F32),
            pltpu.VMEM((N_GROUPS, CHUNK, D_STATE), BF16),
            pltpu.VMEM((N_GROUPS, CHUNK, GROUP_W), F32),
            pltpu.VMEM((N_GROUPS, D_STATE, GROUP_W), F32),
            pltpu.VMEM((N_GROUPS, CHUNK, LANES), F32),
            pltpu.VMEM((LANES, CHUNK), F32),
            pltpu.VMEM((LANES, CHUNK), F32),
        ],
        compiler_params=_cparams(2),
        name="ssd_prompt",
    )(proj, proj, proj, dt_raw, lw["dt_bias"], lw["a_log"], lw["d_skip_x"], lw["norm_w"])
    return y, st.reshape(bsz, N_HEADS, HEAD_DIM, D_STATE)


def _ssd_sample_prep_kernel(xs_ref, bc_ref, dt_ref, cst_ref, cw_ref, cb_ref, dtb_ref, alog_ref,
                            ex_ref, xs_o, b_o, c_o, decht_o, xdtt_o):
    def conv(u, lo, hi):
        acc = cb_ref[:, lo:hi]
        for j in range(CONV_K - 1):
            acc = acc + cst_ref[j][:, lo:hi] * cw_ref[j:j + 1, lo:hi]
        acc = acc + u * cw_ref[CONV_K - 1:CONV_K, lo:hi]
        return _silu(acc)

    xs = conv(xs_ref[...], 0, D_INNER)
    bc = conv(bc_ref[...], D_INNER, CONV_DIM)
    xs_o[...] = xs
    b_o[...] = bc[:, 0:N_GROUPS * D_STATE]
    c_o[...] = bc[:, N_GROUPS * D_STATE:BC_W]
    dt = _softplus(dt_ref[...] + dtb_ref[...])
    d_a = dt * (-jnp.exp(alog_ref[...]))
    ex = ex_ref[...]

    def expand(v):
        hi, mid, lo3 = _split3(v)
        return (jnp.dot(hi, ex, preferred_element_type=F32)
                + jnp.dot(mid, ex, preferred_element_type=F32)
                + jnp.dot(lo3, ex, preferred_element_type=F32))

    xdt = expand(dt) * xs
    decht_o[...] = jnp.exp(d_a).T
    xdtt_o[...] = xdt.T


def _ssd_sample_prep(proj, dt_raw, conv_state_t, lw):
    s = proj.shape[0]
    full = lambda shape: pl.BlockSpec(shape, lambda i: (0,) * len(shape))
    return pl.pallas_call(
        _ssd_sample_prep_kernel,
        grid=(1,),
        in_specs=[
            pl.BlockSpec((s, D_INNER), lambda i: (0, OFF_XS // D_INNER)),
            pl.BlockSpec((s, BC_W), lambda i: (0, OFF_BC // BC_W)),
            full((s, DT_PAD)), full((CONV_K - 1, s, CONV_DIM)),
            full((CONV_K, CONV_DIM)), full((1, CONV_DIM)), full((1, DT_PAD)), full((1, DT_PAD)),
            full((DT_PAD, D_INNER)),
        ],
        out_specs=[full((s, D_INNER)), full((s, N_GROUPS * D_STATE)), full((s, N_GROUPS * D_STATE)),
                   full((DT_PAD, s)), full((D_INNER, s))],
        out_shape=[jax.ShapeDtypeStruct((s, D_INNER), F32),
                   jax.ShapeDtypeStruct((s, N_GROUPS * D_STATE), F32),
                   jax.ShapeDtypeStruct((s, N_GROUPS * D_STATE), F32),
                   jax.ShapeDtypeStruct((DT_PAD, s), F32),
                   jax.ShapeDtypeStruct((D_INNER, s), F32)],
        compiler_params=_cparams(1),
        name="ssd_sample_prep",
    )(proj, proj, dt_raw, conv_state_t, lw["conv_w"], lw["conv_b"], lw["dt_bias"], lw["a_log"],
      lw["head_expand"])


def _ssd_sample_step_kernel(st_ref, decht_ref, xdtt_ref, b_ref, c_ref, *rest, fill_other_layers):
    st_o, yt_o = rest[-2:]
    s = pl.program_id(0)
    n_s = xdtt_ref.shape[1]
    onehot = lax.broadcasted_iota(jnp.int32, (GROUP_W, n_s), 1) == s
    onehot_h = lax.broadcasted_iota(jnp.int32, (DT_PAD, n_s), 1) == s

    @pl.when(s == 0)
    def _():
        yt_o[...] = jnp.zeros(yt_o.shape, F32)

    if fill_other_layers:
        st_o[1:] = jnp.zeros((st_o.shape[0] - 1,) + st_o.shape[1:], F32)

    dech = jnp.sum(jnp.where(onehot_h, decht_ref[...], 0.0), axis=-1, keepdims=True)
    b_all = b_ref[pl.ds(s, 1), :]
    c_all = c_ref[pl.ds(s, 1), :]
    for g in range(N_GROUPS):
        rows = slice(g * GROUP_W, (g + 1) * GROUP_W)
        cols = slice(g * D_STATE, (g + 1) * D_STATE)
        xdt = jnp.sum(jnp.where(onehot, xdtt_ref[rows, :], 0.0), axis=-1, keepdims=True)
        b_row = b_all[:, cols]
        c_row = c_all[:, cols]
        upd = xdt * b_row
        parts = []
        for hh in range(HEADS_PER_GROUP):
            h = g * HEADS_PER_GROUP + hh
            r_h = slice(h * HEAD_DIM, (h + 1) * HEAD_DIM)
            parts.append(st_ref[0, 0, r_h, :] * dech[h:h + 1, :]
                         + upd[hh * HEAD_DIM:(hh + 1) * HEAD_DIM, :])
        st_new = jnp.concatenate(parts, axis=0)
        st_o[0, 0, rows, :] = st_new
        y_col = jnp.sum(st_new * c_row, axis=-1, keepdims=True)
        yt_o[rows, :] = yt_o[rows, :] + jnp.where(onehot, y_col, 0.0)


def _ssd_sample_step(state_all, layer, prev_out, decht, xdtt, b_m, c_m):
    depth = state_all.shape[0]
    s = xdtt.shape[1]
    full = lambda shape: pl.BlockSpec(shape, lambda i: (0,) * len(shape))
    in_specs = [pl.BlockSpec((1, 1, D_INNER, D_STATE), lambda i: (layer, i, 0, 0)),
                full((DT_PAD, s)), full((D_INNER, s)),
                full((s, N_GROUPS * D_STATE)), full((s, N_GROUPS * D_STATE))]
    args = [state_all, decht, xdtt, b_m, c_m]
    if prev_out is None:
        assert layer == 0
        aliases = {}
        st_out = pl.BlockSpec((depth, 1, D_INNER, D_STATE), lambda i: (0, i, 0, 0))
    else:
        in_specs.append(pl.BlockSpec(memory_space=pl.ANY))
        args.append(prev_out)
        aliases = {len(args) - 1: 0}
        st_out = pl.BlockSpec((1, 1, D_INNER, D_STATE), lambda i: (layer, i, 0, 0))
    return pl.pallas_call(
        functools.partial(_ssd_sample_step_kernel, fill_other_layers=prev_out is None),
        grid=(s,),
        in_specs=in_specs,
        out_specs=[st_out, full((D_INNER, s))],
        out_shape=[jax.ShapeDtypeStruct(state_all.shape, F32),
                   jax.ShapeDtypeStruct((D_INNER, s), F32)],
        input_output_aliases=aliases,
        compiler_params=_cparams(1),
        name="ssd_sample_step",
    )(*args)


def _ssd_sample_finish_kernel(yt_ref, xs_ref, zs_ref, dsk_ref, nw_ref, y_ref):
    y = yt_ref[...].T
    for g in range(N_GROUPS):
        cols = slice(g * GROUP_W, (g + 1) * GROUP_W)
        yv = y[:, cols] + dsk_ref[:, cols] * xs_ref[:, cols]
        v = yv * zs_ref[:, cols]
        ms = jnp.mean(v * v, axis=-1, keepdims=True)
        y_ref[:, cols] = (v * lax.rsqrt(ms + RMS_EPS) * nw_ref[:, cols]).astype(y_ref.dtype)


def _ssd_sample_finish(yt, xs, proj, lw):
    s = xs.shape[0]
    full = lambda shape: pl.BlockSpec(shape, lambda i: (0,) * len(shape))
    return pl.pallas_call(
        _ssd_sample_finish_kernel,
        grid=(1,),
        in_specs=[full((D_INNER, s)), full((s, D_INNER)),
                  pl.BlockSpec((s, D_INNER), lambda i: (0, OFF_Z // D_INNER)),
                  full((1, D_INNER)), full((1, D_INNER))],
        out_specs=full((s, D_INNER)),
        out_shape=jax.ShapeDtypeStruct((s, D_INNER), BF16),
        compiler_params=_cparams(1),
        name="ssd_sample_finish",
    )(yt, xs, proj, lw["d_skip_x"], lw["norm_w"])


POOL_TP = 256
POOL_HALO = 16


def _pool_prompt_kernel(u_ref, o_ref, ext_s):
    c = pl.program_id(1)
    tp = POOL_TP

    @pl.when(c == 0)
    def _():
        ext_s[0:POOL_HALO, :] = jnp.zeros((POOL_HALO, D_POOL), F32)

    ext_s[POOL_HALO:POOL_HALO + tp, :] = u_ref[...]
    pos = c * tp + lax.broadcasted_iota(jnp.int32, (tp, 1), 0)
    for gi, win in enumerate(POOL_WINDOWS):
        cols = slice(gi * POOL_GC, (gi + 1) * POOL_GC)
        u = ext_s[POOL_HALO:POOL_HALO + tp, cols]
        tot = u
        for k in range(1, win):
            tot = tot + ext_s[POOL_HALO - k:POOL_HALO - k + tp, cols]
        cnt = jnp.minimum(win, pos + 1).astype(F32)
        o_ref[:, cols] = (tot / cnt - u).astype(o_ref.dtype)
    ext_s[0:POOL_HALO, :] = ext_s[tp:tp + POOL_HALO, :]


def _pool_prompt(proj, bsz, seq):
    nt = seq // POOL_TP
    return pl.pallas_call(
        _pool_prompt_kernel,
        grid=(bsz, nt),
        in_specs=[pl.BlockSpec((POOL_TP, D_POOL), lambda b, c: (b * nt + c, OFF_POOL // D_POOL))],
        out_specs=pl.BlockSpec((POOL_TP, D_POOL), lambda b, c: (b * nt + c, 0)),
        out_shape=jax.ShapeDtypeStruct((bsz * seq, D_POOL), BF16),
        scratch_shapes=[pltpu.VMEM((POOL_HALO + POOL_TP, D_POOL), F32)],
        compiler_params=_cparams(2),
        name="pool_prompt",
    )(proj)


def _pool_sample_kernel(u_ref, buf_ref, o_ref):
    for gi, win in enumerate(POOL_WINDOWS):
        cols = slice(gi * POOL_GC, (gi + 1) * POOL_GC)
        u = u_ref[:, cols]
        tot = u
        for k in range(1, win):
            tot = tot + buf_ref[POOL_BUF - k][:, cols]
        cnt = float(min(win, PAST_LEN + 1))
        o_ref[:, cols] = (tot / cnt - u).astype(o_ref.dtype)


def _pool_sample(proj, pool_state_t):
    s = proj.shape[0]
    return pl.pallas_call(
        _pool_sample_kernel,
        grid=(1,),
        in_specs=[pl.BlockSpec((s, D_POOL), lambda i: (0, OFF_POOL // D_POOL)),
                  pl.BlockSpec((POOL_BUF, s, D_POOL), lambda i: (0, 0, 0))],
        out_specs=pl.BlockSpec((s, D_POOL), lambda i: (0, 0)),
        out_shape=jax.ShapeDtypeStruct((s, D_POOL), BF16),
        compiler_params=_cparams(1),
        name="pool_sample",
    )(proj, pool_state_t)


def _pool_mm_kernel(p_ref, w_ref, sc_ref, o_ref):
    acc = jnp.dot(p_ref[...], w_ref[0], preferred_element_type=F32)
    o_ref[...] = acc * sc_ref[...]


def _pool_mm(pooled, w_pool, scale, tm):
    m = pooled.shape[0]
    ng = len(POOL_WINDOWS)
    return pl.pallas_call(
        _pool_mm_kernel,
        grid=(m // tm, ng),
        in_specs=[pl.BlockSpec((tm, POOL_GC), lambda i, g: (i, g)),
                  pl.BlockSpec((1, POOL_GC, POOL_GC), lambda i, g: (g, 0, 0)),
                  pl.BlockSpec((1, POOL_GC), lambda i, g: (0, g))],
        out_specs=pl.BlockSpec((tm, POOL_GC), lambda i, g: (i, g)),
        out_shape=jax.ShapeDtypeStruct((m, D_POOL), F32),
        compiler_params=_cparams(2),
        name="pool_mm",
    )(pooled, w_pool, scale)


def _branch_merge_kernel(y_ref, w_ref, ga_ref, gb_ref, yb_ref, o_ref, w_s):
    @pl.when(pl.program_id(1) == 0)
    def _():
        w_s[...] = w_ref[...].astype(BF16)

    y_a = jnp.dot(y_ref[...], w_s[...], preferred_element_type=F32)
    o_ref[...] = (ga_ref[...] * y_a + gb_ref[...] * yb_ref[...]).astype(o_ref.dtype)


def _branch_merge(y, w_br, proj, y_b, tm, tn):
    m = y.shape[0]
    nj = D_MODEL // tn
    ga0 = OFF_GATE // tn
    return pl.pallas_call(
        _branch_merge_kernel,
        grid=(nj, m // tm),
        in_specs=[pl.BlockSpec((tm, D_INNER), lambda j, i: (i, 0)),
                  pl.BlockSpec((D_INNER, tn), lambda j, i: (0, j)),
                  pl.BlockSpec((tm, tn), lambda j, i: (i, ga0 + j)),
                  pl.BlockSpec((tm, tn), lambda j, i: (i, ga0 + nj + j)),
                  pl.BlockSpec((tm, tn), lambda j, i: (i, j))],
        out_specs=pl.BlockSpec((tm, tn), lambda j, i: (i, j)),
        out_shape=jax.ShapeDtypeStruct((m, D_MODEL), BF16),
        scratch_shapes=[pltpu.VMEM((D_INNER, tn), BF16)],
        compiler_params=_cparams(2),
        name="branch_merge",
    )(y, w_br, proj, proj, y_b)


def _proj_ln_kernel(m_ref, w_ref, x_ref, g_ref, b_ref, o_ref, obf_ref):
    acc = jnp.dot(m_ref[...], w_ref[...], preferred_element_type=F32)
    y = _layer_norm(ALPHA * x_ref[...] + acc, g_ref[...], b_ref[...])
    o_ref[...] = y
    obf_ref[...] = y.astype(BF16)


def _proj_ln(mix, w, x, g, b, tm):
    m, k = mix.shape
    return pl.pallas_call(
        _proj_ln_kernel,
        grid=(m // tm,),
        in_specs=[pl.BlockSpec((tm, k), lambda i: (i, 0)),
                  pl.BlockSpec((k, D_MODEL), lambda i: (0, 0)),
                  pl.BlockSpec((tm, D_MODEL), lambda i: (i, 0)),
                  pl.BlockSpec((1, D_MODEL), lambda i: (0, 0)),
                  pl.BlockSpec((1, D_MODEL), lambda i: (0, 0))],
        out_specs=[pl.BlockSpec((tm, D_MODEL), lambda i: (i, 0)),
                   pl.BlockSpec((tm, D_MODEL), lambda i: (i, 0))],
        out_shape=[jax.ShapeDtypeStruct((m, D_MODEL), F32),
                   jax.ShapeDtypeStruct((m, D_MODEL), BF16)],
        compiler_params=_cparams(1),
        name="proj_ln",
    )(mix, w, x, g, b)


FFN_HALO = SUBLANES


def _ffn_conv_gate(hg_s, hv_s, cw_g, cw_v, cb_g, cb_v, tm):
    def conv(h_s, cw, cb):
        acc = cb[...]
        for j in range(FFN_K):
            lo = FFN_HALO - (FFN_K - 1) + j
            acc = acc + h_s[lo:lo + tm, :] * cw[j:j + 1, :]
        return acc
    return _silu(conv(hg_s, cw_g, cb_g)) * conv(hv_s, cw_v, cb_v)


def _ffn_up_prompt_kernel(x_ref, wg_ref, wv_ref, cwg_ref, cwv_ref, cbg_ref, cbv_ref,
                          a_ref, tg_ref, tv_ref, hg_s, hv_s, *, tiles_per_seq):
    i = pl.program_id(1)
    tm = x_ref.shape[0]

    @pl.when(i % tiles_per_seq == 0)
    def _():
        hg_s[0:FFN_HALO, :] = jnp.zeros((FFN_HALO, FFN_TN), F32)
        hv_s[0:FFN_HALO, :] = jnp.zeros((FFN_HALO, FFN_TN), F32)

    x = x_ref[...]
    hg_s[FFN_HALO:FFN_HALO + tm, :] = jnp.dot(x, wg_ref[...], preferred_element_type=F32)
    hv_s[FFN_HALO:FFN_HALO + tm, :] = jnp.dot(x, wv_ref[...], preferred_element_type=F32)
    a_ref[...] = _ffn_conv_gate(hg_s, hv_s, cwg_ref, cwv_ref, cbg_ref, cbv_ref, tm).astype(a_ref.dtype)
    tail_g = hg_s[tm:tm + FFN_HALO, :]
    tail_v = hv_s[tm:tm + FFN_HALO, :]
    hg_s[0:FFN_HALO, :] = tail_g
    hv_s[0:FFN_HALO, :] = tail_v
    tg_ref[0] = tail_g
    tv_ref[0] = tail_v


def _ffn_up_prompt(x_bf, lw, seq, tm):
    m = x_bf.shape[0]
    nj = D_FF_PAD // FFN_TN
    ni = m // tm
    kern = functools.partial(_ffn_up_prompt_kernel, tiles_per_seq=seq // tm)
    wspec_g = pl.BlockSpec((D_MODEL, FFN_TN), lambda j, i: (0, j))
    wspec_v = pl.BlockSpec((D_MODEL, FFN_TN), lambda j, i: (0, nj + j))
    cspec_g = lambda r: pl.BlockSpec((r, FFN_TN), lambda j, i: (0, j))
    cspec_v = lambda r: pl.BlockSpec((r, FFN_TN), lambda j, i: (0, nj + j))
    return pl.pallas_call(
        kern,
        grid=(nj, ni),
        in_specs=[pl.BlockSpec((tm, D_MODEL), lambda j, i: (i, 0)),
                  wspec_g, wspec_v, cspec_g(FFN_K), cspec_v(FFN_K), cspec_g(1), cspec_v(1)],
        out_specs=[pl.BlockSpec((tm, FFN_TN), lambda j, i: (i, j)),
                   pl.BlockSpec((1, FFN_HALO, FFN_TN), lambda j, i: (i, 0, j)),
                   pl.BlockSpec((1, FFN_HALO, FFN_TN), lambda j, i: (i, 0, j))],
        out_shape=[jax.ShapeDtypeStruct((m, D_FF_PAD), BF16),
                   jax.ShapeDtypeStruct((ni, FFN_HALO, D_FF_PAD), F32),
                   jax.ShapeDtypeStruct((ni, FFN_HALO, D_FF_PAD), F32)],
        scratch_shapes=[pltpu.VMEM((FFN_HALO + tm, FFN_TN), F32),
                        pltpu.VMEM((FFN_HALO + tm, FFN_TN), F32)],
        compiler_params=_cparams(2),
        name="ffn_up_prompt",
    )(x_bf, lw["w_up"], lw["w_up"], lw["fconv_w"], lw["fconv_w"], lw["fconv_b"], lw["fconv_b"])


def _ffn_up_sample_kernel(x_ref, wg_ref, wv_ref, sg_ref, sv_ref, cwg_ref, cwv_ref, cbg_ref, cbv_ref,
                          a_ref, hg_ref, hv_ref):
    x = x_ref[...]
    hg = jnp.dot(x, wg_ref[...], preferred_element_type=F32)
    hv = jnp.dot(x, wv_ref[...], preferred_element_type=F32)
    hg_ref[...] = hg
    hv_ref[...] = hv

    def conv(h, st, cw, cb):
        acc = cb[...]
        for j in range(FFN_K - 1):
            acc = acc + st[j] * cw[j:j + 1, :]
        return acc + h * cw[FFN_K - 1:FFN_K, :]

    a_ref[...] = (_silu(conv(hg, sg_ref, cwg_ref, cbg_ref))
                  * conv(hv, sv_ref, cwv_ref, cbv_ref)).astype(a_ref.dtype)


def _ffn_up_sample(x_bf, ffn_state_t, lw):
    s = x_bf.shape[0]
    nj = D_FF_PAD // FFN_TN
    g_blk = lambda r: pl.BlockSpec((r, FFN_TN), lambda j: (0, j))
    v_blk = lambda r: pl.BlockSpec((r, FFN_TN), lambda j: (0, nj + j))
    return pl.pallas_call(
        _ffn_up_sample_kernel,
        grid=(nj,),
        in_specs=[pl.BlockSpec((s, D_MODEL), lambda j: (0, 0)),
                  g_blk(D_MODEL), v_blk(D_MODEL),
                  pl.BlockSpec((FFN_K - 1, s, FFN_TN), lambda j: (0, 0, j)),
                  pl.BlockSpec((FFN_K - 1, s, FFN_TN), lambda j: (0, 0, nj + j)),
                  g_blk(FFN_K), v_blk(FFN_K), g_blk(1), v_blk(1)],
        out_specs=[pl.BlockSpec((s, FFN_TN), lambda j: (0, j)),
                   pl.BlockSpec((s, FFN_TN), lambda j: (0, j)),
                   pl.BlockSpec((s, FFN_TN), lambda j: (0, j))],
        out_shape=[jax.ShapeDtypeStruct((s, D_FF_PAD), BF16),
                   jax.ShapeDtypeStruct((s, D_FF_PAD), F32),
                   jax.ShapeDtypeStruct((s, D_FF_PAD), F32)],
        compiler_params=_cparams(1),
        name="ffn_up_sample",
    )(x_bf, lw["w_up"], lw["w_up"], ffn_state_t, ffn_state_t,
      lw["fconv_w"], lw["fconv_w"], lw["fconv_b"], lw["fconv_b"])


DOWN_TK = 512


def _down_ln_kernel(a_ref, w_ref, x_ref, g_ref, b_ref, o_ref, obf_ref, acc_s):
    k = pl.program_id(1)

    @pl.when(k == 0)
    def _():
        acc_s[...] = ALPHA * x_ref[...]

    acc_s[...] += jnp.dot(a_ref[...], w_ref[...], preferred_element_type=F32)

    @pl.when(k == pl.num_programs(1) - 1)
    def _():
        y = _layer_norm(acc_s[...], g_ref[...], b_ref[...])
        o_ref[...] = y
        obf_ref[...] = y.astype(BF16)


def _down_ln(a, w, x, g, b, tm):
    m = a.shape[0]
    nk = D_FF_PAD // DOWN_TK
    return pl.pallas_call(
        _down_ln_kernel,
        grid=(m // tm, nk),
        in_specs=[pl.BlockSpec((tm, DOWN_TK), lambda i, k: (i, k)),
                  pl.BlockSpec((DOWN_TK, D_MODEL), lambda i, k: (k, 0)),
                  pl.BlockSpec((tm, D_MODEL), lambda i, k: (i, 0)),
                  pl.BlockSpec((1, D_MODEL), lambda i, k: (0, 0)),
                  pl.BlockSpec((1, D_MODEL), lambda i, k: (0, 0))],
        out_specs=[pl.BlockSpec((tm, D_MODEL), lambda i, k: (i, 0)),
                   pl.BlockSpec((tm, D_MODEL), lambda i, k: (i, 0))],
        out_shape=[jax.ShapeDtypeStruct((m, D_MODEL), F32),
                   jax.ShapeDtypeStruct((m, D_MODEL), BF16)],
        scratch_shapes=[pltpu.VMEM((tm, D_MODEL), F32)],
        compiler_params=_cparams(2),
        name="down_ln",
    )(a, w, x, g, b)


def _pad_ff(v):
    pad = [(0, 0)] * (v.ndim - 1) + [(0, D_FF_PAD - D_FF)]
    return jnp.concatenate([jnp.pad(v[..., :D_FF], pad), jnp.pad(v[..., D_FF:], pad)], axis=-1)


def _unpad_ff(v):
    return jnp.concatenate([v[..., :D_FF], v[..., D_FF_PAD:D_FF_PAD + D_FF]], axis=-1)


def _prep_layer(w_in, b_gate, conv_w, conv_b, dt_bias, a_log, d_skip, norm_w, w_br, w_pool,
                pool_scale, w_out, ln1_g, ln1_b, w_up, fconv_w, fconv_b, w_down, ln2_g, ln2_b):
    o_xbc = D_INNER
    o_dt = o_xbc + CONV_DIM
    o_pool = o_dt + N_HEADS
    o_gate = o_pool + D_POOL
    w_pg = w_in[:, o_pool:]
    w_dt = jnp.pad(w_in[:, o_dt:o_pool], ((0, 0), (0, DT_PAD - N_HEADS))).astype(BF16)
    pad_h = lambda v: jnp.pad(v, (0, DT_PAD - N_HEADS)).reshape(1, DT_PAD)
    head_of_channel = jnp.arange(D_INNER) // HEAD_DIM
    head_expand = (jnp.arange(DT_PAD)[:, None] == head_of_channel[None, :]).astype(BF16)
    return dict(
        w_in=w_in, w_pg=w_pg, w_dt=w_dt, b_gate=b_gate.reshape(1, -1),
        conv_w=conv_w, conv_b=conv_b.reshape(1, -1),
        dt_bias=pad_h(dt_bias), a_log=pad_h(a_log),
        d_skip_x=jnp.repeat(d_skip, HEAD_DIM).reshape(1, -1),
        norm_w=norm_w.reshape(1, -1), head_expand=head_expand,
        w_br=w_br, w_pool=w_pool.astype(BF16), pool_scale=pool_scale.reshape(1, -1),
        w_out=w_out.astype(BF16), ln1_g=ln1_g.reshape(1, -1), ln1_b=ln1_b.reshape(1, -1),
        w_up=_pad_ff(w_up).astype(BF16), fconv_w=_pad_ff(fconv_w),
        fconv_b=_pad_ff(fconv_b).reshape(1, -1),
        w_down=jnp.pad(w_down, ((0, D_FF_PAD - D_FF), (0, 0))).astype(BF16),
        ln2_g=ln2_g.reshape(1, -1), ln2_b=ln2_b.reshape(1, -1),
    )


def _raw_xbc(proj):
    return proj[:, OFF_XS:OFF_XS + CONV_DIM]


def _layer_prompt(x, x_bf, lw, bsz, seq):
    tm_in = 1024
    proj, xbc_tail = _in_proj(x_bf, lw, tm_in, seq // tm_in, conv=True)
    dt_raw = _matmul(x_bf, lw["w_dt"], F32, 1024, DT_PAD)
    y, new_ssm = _ssd_prompt(proj, dt_raw, lw, bsz, seq)
    pooled = _pool_prompt(proj, bsz, seq)
    y_b = _pool_mm(pooled, lw["w_pool"], lw["pool_scale"], 1024)
    mix = _branch_merge(y, lw["w_br"], proj, y_b, 512, 512)
    x1, x1_bf = _proj_ln(mix, lw["w_out"], x, lw["ln1_g"], lw["ln1_b"], 256)
    tm_up = 1024
    act, tail_g, tail_v = _ffn_up_prompt(x1_bf, lw, seq, tm_up)
    x2, x2_bf = _down_ln(act, lw["w_down"], x1, lw["ln2_g"], lw["ln2_b"], 512)
    p3 = proj.reshape(bsz, seq, N_MAIN)
    new_pool = p3[:, seq - POOL_BUF:, OFF_POOL:OFF_POOL + D_POOL]
    tps_in = seq // tm_in
    new_conv = xbc_tail[tps_in - 1::tps_in, CONV_HALO - (CONV_K - 1):, OFF_XS:OFF_XS + CONV_DIM]
    tps = seq // tm_up
    last = slice(tps - 1, None, tps)
    tail = jnp.concatenate([tail_g[last, :, :D_FF], tail_v[last, :, :D_FF]], axis=-1)
    new_ffn = tail[:, FFN_HALO - (FFN_K - 1):, :]
    return x2, x2_bf, new_ssm, new_conv, new_pool, new_ffn


def _layer_sample(x, x_bf, ssm_all, layer, ssm_prev_out, s_conv, s_pool, s_ffn, lw):
    s = x.shape[0]
    proj, _ = _in_proj(x_bf, lw, s, 1, conv=False)
    dt_raw = _matmul(x_bf, lw["w_dt"], F32, s, DT_PAD)
    xs, b_m, c_m, decht, xdtt = _ssd_sample_prep(proj, dt_raw, jnp.swapaxes(s_conv, 0, 1), lw)
    new_ssm, yt = _ssd_sample_step(ssm_all, layer, ssm_prev_out, decht, xdtt, b_m, c_m)
    y = _ssd_sample_finish(yt, xs, proj, lw)
    pooled = _pool_sample(proj, jnp.swapaxes(s_pool, 0, 1))
    y_b = _pool_mm(pooled, lw["w_pool"], lw["pool_scale"], s)
    mix = _branch_merge(y, lw["w_br"], proj, y_b, s, 512)
    x1, x1_bf = _proj_ln(mix, lw["w_out"], x, lw["ln1_g"], lw["ln1_b"], s)
    act, h_g, h_v = _ffn_up_sample(x1_bf, _pad_ff(jnp.swapaxes(s_ffn, 0, 1)), lw)
    x2, x2_bf = _down_ln(act, lw["w_down"], x1, lw["ln2_g"], lw["ln2_b"], s)
    new_conv = jnp.concatenate([s_conv[:, 1:], _raw_xbc(proj)[:, None, :]], axis=1)
    new_pool = jnp.concatenate([s_pool[:, 1:], proj[:, None, OFF_POOL:OFF_POOL + D_POOL]], axis=1)
    h_new = jnp.concatenate([h_g[:, :D_FF], h_v[:, :D_FF]], axis=-1)
    new_ffn = jnp.concatenate([s_ffn[:, 1:], h_new[:, None, :]], axis=1)
    return x2, x2_bf, new_ssm, new_conv, new_pool, new_ffn


def kernel(x_prompt, x_sample, state_ssm, state_ssd_conv, state_pool, state_ffn_conv, w_in, b_gate, conv_w, conv_b, dt_bias, a_log, d_skip, ssd_norm_w, w_ssd_branch, w_pool, pool_scale, w_out, ln1_g, ln1_b, w_up, ffn_conv_w, ffn_conv_b, w_down, ln2_g, ln2_b):
    bsz, seq, _ = x_prompt.shape
    n_s = x_sample.shape[0]
    assert x_sample.shape[1] == 1 and seq % 1024 == 0
    xp = x_prompt.reshape(bsz * seq, D_MODEL)
    xs = x_sample.reshape(n_s, D_MODEL)
    xp_bf, xs_bf = xp.astype(BF16), xs.astype(BF16)
    outs_p, outs_s = [], []
    ssm_all = state_ssm.reshape(DEPTH, n_s, D_INNER, D_STATE)
    ssm_out = None
    for i in range(DEPTH):
        lw = _prep_layer(w_in[i], b_gate[i], conv_w[i], conv_b[i], dt_bias[i], a_log[i], d_skip[i],
                         ssd_norm_w[i], w_ssd_branch[i], w_pool[i], pool_scale[i], w_out[i],
                         ln1_g[i], ln1_b[i], w_up[i], ffn_conv_w[i], ffn_conv_b[i], w_down[i],
                         ln2_g[i], ln2_b[i])
        xp, xp_bf, *op = _layer_prompt(xp, xp_bf, lw, bsz, seq)
        xs, xs_bf, ssm_out, *os_ = _layer_sample(xs, xs_bf, ssm_all, i, ssm_out, state_ssd_conv[i],
                                                 state_pool[i], state_ffn_conv[i], lw)
        outs_p.append(op)
        outs_s.append(os_)
    stack = lambda outs, k: jnp.stack([o[k] for o in outs])
    return (xp.reshape(bsz, seq, D_MODEL), xs.reshape(n_s, 1, D_MODEL),
            stack(outs_p, 0), stack(outs_p, 1), stack(outs_p, 2), stack(outs_p, 3),
            ssm_out.reshape(state_ssm.shape), stack(outs_s, 0), stack(outs_s, 1), stack(outs_s, 2))
```

```python
import functools

import jax
import jax.numpy as jnp
from jax import lax
from jax.experimental import pallas as pl
from jax.experimental.pallas import tpu as pltpu

F32 = jnp.float32
BF16 = jnp.bfloat16

D_MODEL = 2048
HEAD_DIM = 64
D_INNER = 2 * D_MODEL
N_HEADS = D_INNER // HEAD_DIM
N_GROUPS = 8
HEADS_PER_GROUP = N_HEADS // N_GROUPS
GROUP_W = D_INNER // N_GROUPS
D_STATE = 128
CONV_K = 4
BC_W = 2 * N_GROUPS * D_STATE
CONV_DIM = D_INNER + BC_W
CHUNK = 128
D_POOL = D_MODEL
POOL_WINDOWS = (2, 4, 8, 16)
POOL_GC = D_POOL // len(POOL_WINDOWS)
POOL_BUF = max(POOL_WINDOWS) - 1
D_FF = 5504
FFN_K = 3
DEPTH = 2
PAST_LEN = 16384
ALPHA = (2 * DEPTH) ** 0.25
LN_EPS = 1e-5
RMS_EPS = 1e-5

LANES = 128
SUBLANES = 8
D_FF_PAD = 5632
FFN_TN = 512
DT_PAD = LANES
OFF_Z = 0
OFF_XS = D_INNER
OFF_BC = 2 * D_INNER
OFF_POOL = 2 * D_INNER + BC_W
OFF_GATE = OFF_POOL + D_POOL
N_MAIN = OFF_GATE + 2 * D_MODEL
VMEM_LIMIT = 56 * 1024 * 1024
NEG_BIG = -1e30


def _cparams(n_axes):
    return pltpu.CompilerParams(dimension_semantics=("arbitrary",) * n_axes,
                                vmem_limit_bytes=VMEM_LIMIT)


def _sigmoid(x):
    return 1.0 / (1.0 + jnp.exp(-x))


def _silu(x):
    return x * _sigmoid(x)


def _softplus(x):
    return jnp.maximum(x, 0.0) + jnp.log(1.0 + jnp.exp(-jnp.abs(x)))


def _layer_norm(r, g, b):
    mu = jnp.mean(r, axis=-1, keepdims=True)
    d = r - mu
    var = jnp.mean(d * d, axis=-1, keepdims=True)
    return d * lax.rsqrt(var + LN_EPS) * g + b


def _dt_proj_kernel(x_ref, w_ref, o_ref):
    o_ref[...] = jnp.dot(x_ref[...], w_ref[...].astype(BF16), preferred_element_type=F32)


def _dt_proj(x_bf, w_in, layer, tm):
    m = x_bf.shape[0]
    return pl.pallas_call(
        _dt_proj_kernel,
        grid=(m // tm,),
        in_specs=[pl.BlockSpec((tm, D_MODEL), lambda i: (i, 0)),
                  pl.BlockSpec((None, D_MODEL, DT_PAD), lambda i: (layer, 0, DT_COL_BLOCK))],
        out_specs=pl.BlockSpec((tm, DT_PAD), lambda i: (i, 0)),
        out_shape=jax.ShapeDtypeStruct((m, DT_PAD), F32),
        compiler_params=_cparams(1),
        name="dt_proj",
    )(x_bf, w_in)


IN_TN = 1024
J_XBC = OFF_XS // IN_TN
J_POOL = OFF_POOL // IN_TN
J_GATE = OFF_GATE // IN_TN
J_END = N_MAIN // IN_TN
CONV_HALO = SUBLANES
PG_SHIFT = N_HEADS
DT_COL_BLOCK = (2 * D_INNER + BC_W) // DT_PAD


def _in_proj_kernel(x_ref, wa_ref, wb_ref, cw_ref, cb_ref, bg_ref, o_ref, tail_ref, w_s, h_s,
                    *, tiles_per_seq, conv):
    j = pl.program_id(0)
    i = pl.program_id(1)
    tm = x_ref.shape[0]

    @pl.when((i == 0) & (j < J_POOL))
    def _():
        w_s[...] = wa_ref[...].astype(BF16)

    @pl.when((i == 0) & (j >= J_POOL))
    def _():
        w_s[...] = jnp.concatenate([wa_ref[:, PG_SHIFT:], wb_ref[:, :PG_SHIFT]], axis=1).astype(BF16)

    def mm():
        return jnp.dot(x_ref[...], w_s[...], preferred_element_type=F32)

    is_xbc = (j >= J_XBC) & (j < J_POOL)

    @pl.when(jnp.logical_not(is_xbc))
    def _():
        tail_ref[0] = jnp.zeros((CONV_HALO, IN_TN), F32)

    @pl.when(j < J_XBC)
    def _():
        o_ref[...] = _silu(mm())

    @pl.when(is_xbc)
    def _():
        if conv:
            @pl.when(i % tiles_per_seq == 0)
            def _():
                h_s[0:CONV_HALO, :] = jnp.zeros((CONV_HALO, IN_TN), F32)

            h_s[CONV_HALO:CONV_HALO + tm, :] = mm()
            acc = cb_ref[...]
            for k in range(CONV_K):
                lo = CONV_HALO - (CONV_K - 1) + k
                acc = acc + h_s[lo:lo + tm, :] * cw_ref[k:k + 1, :]
            o_ref[...] = _silu(acc)
            tail = h_s[tm:tm + CONV_HALO, :]
            h_s[0:CONV_HALO, :] = tail
            tail_ref[0] = tail
        else:
            o_ref[...] = mm()
            tail_ref[0] = jnp.zeros((CONV_HALO, IN_TN), F32)

    @pl.when((j >= J_POOL) & (j < J_GATE))
    def _():
        o_ref[...] = mm()

    @pl.when(j >= J_GATE)
    def _():
        o_ref[...] = _sigmoid(mm() + bg_ref[...])


def _in_proj(x_bf, lw, layer, tm, tiles_per_seq, conv):
    m = x_bf.shape[0]
    ni = m // tm
    n_conv_tiles = CONV_DIM // IN_TN
    clamp = lambda v, lo, hi: jnp.minimum(jnp.maximum(v, lo), hi)
    conv_tile = lambda j: clamp(j - J_XBC, 0, n_conv_tiles - 1)
    kern = functools.partial(_in_proj_kernel, tiles_per_seq=tiles_per_seq, conv=conv)
    return pl.pallas_call(
        kern,
        grid=(J_END, ni),
        in_specs=[pl.BlockSpec((tm, D_MODEL), lambda j, i: (i, 0)),
                  pl.BlockSpec((None, D_MODEL, IN_TN), lambda j, i: (layer, 0, j)),
                  pl.BlockSpec((None, D_MODEL, IN_TN), lambda j, i: (layer, 0, jnp.maximum(j + 1, J_POOL)),
                               pipeline_mode=pl.Buffered(1)),
                  pl.BlockSpec((CONV_K, IN_TN), lambda j, i: (0, conv_tile(j))),
                  pl.BlockSpec((1, IN_TN), lambda j, i: (0, conv_tile(j))),
                  pl.BlockSpec((1, IN_TN), lambda j, i: (0, clamp(j - J_GATE, 0, J_END - J_GATE - 1)))],
        out_specs=[pl.BlockSpec((tm, IN_TN), lambda j, i: (i, j)),
                   pl.BlockSpec((1, CONV_HALO, IN_TN), lambda j, i: (i, 0, j))],
        out_shape=[jax.ShapeDtypeStruct((m, N_MAIN), F32),
                   jax.ShapeDtypeStruct((ni, CONV_HALO, N_MAIN), F32)],
        scratch_shapes=[pltpu.VMEM((D_MODEL, IN_TN), BF16),
                        pltpu.VMEM((CONV_HALO + tm, IN_TN), F32)],
        compiler_params=_cparams(2),
        name="in_proj",
    )(x_bf, lw["w_in"], lw["w_in"], lw["conv_w"], lw["conv_b"], lw["b_gate"])


def _split3(v):
    hi = v.astype(BF16)
    r1 = v - hi.astype(F32)
    mid = r1.astype(BF16)
    lo = (r1 - mid.astype(F32)).astype(BF16)
    return hi, mid, lo


def _ssd_prompt_kernel(xs_ref, bc_ref, zs_ref, dt_ref, dtb_ref, alog_ref, dsk_ref, nw_ref,
                       y_ref, st_ref,
                       xs_s, bt_s, c_s, y_s, state_s, acol_s, arow_s, dtrow_s):
    c = pl.program_id(1)
    n_chunks = pl.num_programs(1)
    q = CHUNK

    @pl.when(c == 0)
    def _():
        state_s[...] = jnp.zeros(state_s.shape, F32)

    for g in range(N_GROUPS):
        xs_s[g] = xs_ref[:, g * GROUP_W:(g + 1) * GROUP_W]
        bt_s[g] = bc_ref[:, g * D_STATE:(g + 1) * D_STATE].T
        c_s[g] = bc_ref[:, (N_GROUPS + g) * D_STATE:(N_GROUPS + g + 1) * D_STATE].astype(BF16)

    dt = _softplus(dt_ref[...] + dtb_ref[...])
    a_neg = -jnp.exp(alog_ref[...])
    d_a = dt * a_neg
    row = lax.broadcasted_iota(jnp.int32, (q, q), 0)
    col = lax.broadcasted_iota(jnp.int32, (q, q), 1)
    causal = row >= col
    tril = jnp.where(causal, 1.0, 0.0).astype(BF16)
    hi, mid, lo3 = _split3(d_a)
    a_cum = (jnp.dot(tril, hi, preferred_element_type=F32)
             + jnp.dot(tril, mid, preferred_element_type=F32)
             + jnp.dot(tril, lo3, preferred_element_type=F32))
    arow_s[...] = a_cum.T
    dtrow_s[...] = dt.T
    for g in range(N_GROUPS):
        sh = (LANES - HEADS_PER_GROUP * g) % LANES
        acol_s[g] = a_cum if sh == 0 else pltpu.roll(a_cum, sh, 1)

    lane = lax.broadcasted_iota(jnp.int32, (q, LANES), 1)
    lo_half = lane < HEAD_DIM

    def group_body(g, carry):
        acol = acol_s[g]
        c_g = c_s[g]
        bt_g = bt_s[g]
        cb = jnp.dot(c_g, bt_g.astype(BF16), preferred_element_type=F32)
        y_off_g = jnp.dot(c_g, state_s[g].astype(BF16), preferred_element_type=F32)
        for k in range(HEADS_PER_GROUP // 2):
            l_parts, b_parts, a_b = [], [], []
            for e in range(2):
                hh = 2 * k + e
                head = g * HEADS_PER_GROUP + hh
                a_col = jnp.broadcast_to(acol[:, hh:hh + 1], (q, q))
                a_row = arow_s[pl.ds(head, 1), :]
                dt_row = dtrow_s[pl.ds(head, 1), :]
                seg = jnp.where(causal, a_col - a_row, NEG_BIG)
                l_parts.append((cb * jnp.exp(seg) * dt_row).astype(BF16))
                w_row = dt_row * jnp.exp(a_col[q - 1:q, :] - a_row)
                b_parts.append((bt_g * w_row).astype(BF16))
                a_b.append(a_col)
            lhs = jnp.concatenate([jnp.concatenate(l_parts, axis=1),
                                   jnp.concatenate(b_parts, axis=1)], axis=0)
            cols = slice(k * LANES, (k + 1) * LANES)
            xs_bf = xs_s[g, :, cols].astype(BF16)
            zero = jnp.zeros_like(xs_bf)
            rhs = jnp.concatenate([jnp.where(lo_half, xs_bf, zero),
                                   jnp.where(lo_half, zero, xs_bf)], axis=0)
            res = jnp.dot(lhs, rhs, preferred_element_type=F32)
            a_pair = jnp.where(lo_half, a_b[0], a_b[1])
            y_s[g, :, cols] = res[0:q] + y_off_g[:, cols] * jnp.exp(a_pair)
            cdec = jnp.exp(a_pair[q - 1:q, :])
            state_s[g, :, cols] = state_s[g, :, cols] * cdec + res[q:2 * q]
        return carry

    lax.fori_loop(0, N_GROUPS, group_body, 0)

    for g in range(N_GROUPS):
        cols = slice(g * GROUP_W, (g + 1) * GROUP_W)
        yv = y_s[g] + dsk_ref[:, cols] * xs_s[g]
        v = yv * zs_ref[:, cols]
        ms = jnp.mean(v * v, axis=-1, keepdims=True)
        y_ref[:, cols] = (v * lax.rsqrt(ms + RMS_EPS) * nw_ref[:, cols]).astype(y_ref.dtype)

    @pl.when(c == n_chunks - 1)
    def _():
        for g in range(N_GROUPS):
            st_ref[0, g] = state_s[g].T


def _ssd_prompt(proj, dt_raw, lw, bsz, seq):
    n_chunks = seq // CHUNK
    rows = lambda b, c: b * n_chunks + c
    small = lambda shape: pl.BlockSpec(shape, lambda b, c: (0, 0))
    y, st = pl.pallas_call(
        _ssd_prompt_kernel,
        grid=(bsz, n_chunks),
        in_specs=[
            pl.BlockSpec((CHUNK, D_INNER), lambda b, c: (rows(b, c), OFF_XS // D_INNER)),
            pl.BlockSpec((CHUNK, BC_W), lambda b, c: (rows(b, c), OFF_BC // BC_W)),
            pl.BlockSpec((CHUNK, D_INNER), lambda b, c: (rows(b, c), OFF_Z // D_INNER)),
            pl.BlockSpec((CHUNK, DT_PAD), lambda b, c: (rows(b, c), 0)),
            small((1, DT_PAD)), small((1, DT_PAD)), small((1, D_INNER)), small((1, D_INNER)),
        ],
        out_specs=[
            pl.BlockSpec((CHUNK, D_INNER), lambda b, c: (rows(b, c), 0)),
            pl.BlockSpec((1, N_GROUPS, GROUP_W, D_STATE), lambda b, c: (b, 0, 0, 0)),
        ],
        out_shape=[jax.ShapeDtypeStruct((bsz * seq, D_INNER), BF16),
                   jax.ShapeDtypeStruct((bsz, N_GROUPS, GROUP_W, D_STATE), F32)],
        scratch_shapes=[
            pltpu.VMEM((N_GROUPS, CHUNK, GROUP_W), F32),
            pltpu.VMEM((N_GROUPS, D_STATE, CHUNK), F32),
            pltpu.VMEM((N_GROUPS, CHUNK, D_STATE), BF16),
            pltpu.VMEM((N_GROUPS, CHUNK, GROUP_W), F32),
            pltpu.VMEM((N_GROUPS, D_STATE, GROUP_W), F32),
            pltpu.VMEM((N_GROUPS, CHUNK, LANES), F32),
            pltpu.VMEM((LANES, CHUNK), F32),
            pltpu.VMEM((LANES, CHUNK), F32),
        ],
        compiler_params=_cparams(2),
        name="ssd_prompt",
    )(proj, proj, proj, dt_raw, lw["dt_bias"], lw["a_log"], lw["d_skip_x"], lw["norm_w"])
    return y, st.reshape(bsz, N_HEADS, HEAD_DIM, D_STATE)


def _ssd_sample_prep_kernel(xs_ref, bc_ref, dt_ref, cst_ref, cw_ref, cb_ref, dtb_ref, alog_ref,
                            ex_ref, xs_o, b_o, c_o, decht_o, xdtt_o):
    def conv(u, lo, hi):
        acc = cb_ref[:, lo:hi]
        for j in range(CONV_K - 1):
            acc = acc + cst_ref[j][:, lo:hi] * cw_ref[j:j + 1, lo:hi]
        acc = acc + u * cw_ref[CONV_K - 1:CONV_K, lo:hi]
        return _silu(acc)

    xs = conv(xs_ref[...], 0, D_INNER)
    bc = conv(bc_ref[...], D_INNER, CONV_DIM)
    xs_o[...] = xs
    b_o[...] = bc[:, 0:N_GROUPS * D_STATE]
    c_o[...] = bc[:, N_GROUPS * D_STATE:BC_W]
    dt = _softplus(dt_ref[...] + dtb_ref[...])
    d_a = dt * (-jnp.exp(alog_ref[...]))
    ex = ex_ref[...]

    def expand(v):
        hi, mid, lo3 = _split3(v)
        return (jnp.dot(hi, ex, preferred_element_type=F32)
                + jnp.dot(mid, ex, preferred_element_type=F32)
                + jnp.dot(lo3, ex, preferred_element_type=F32))

    xdt = expand(dt) * xs
    decht_o[...] = jnp.exp(d_a).T
    xdtt_o[...] = xdt.T


def _ssd_sample_prep(proj, dt_raw, conv_state_t, lw):
    s = proj.shape[0]
    full = lambda shape: pl.BlockSpec(shape, lambda i: (0,) * len(shape))
    return pl.pallas_call(
        _ssd_sample_prep_kernel,
        grid=(1,),
        in_specs=[
            pl.BlockSpec((s, D_INNER), lambda i: (0, OFF_XS // D_INNER)),
            pl.BlockSpec((s, BC_W), lambda i: (0, OFF_BC // BC_W)),
            full((s, DT_PAD)), full((CONV_K - 1, s, CONV_DIM)),
            full((CONV_K, CONV_DIM)), full((1, CONV_DIM)), full((1, DT_PAD)), full((1, DT_PAD)),
            full((DT_PAD, D_INNER)),
        ],
        out_specs=[full((s, D_INNER)), full((s, N_GROUPS * D_STATE)), full((s, N_GROUPS * D_STATE)),
                   full((DT_PAD, s)), full((D_INNER, s))],
        out_shape=[jax.ShapeDtypeStruct((s, D_INNER), F32),
                   jax.ShapeDtypeStruct((s, N_GROUPS * D_STATE), F32),
                   jax.ShapeDtypeStruct((s, N_GROUPS * D_STATE), F32),
                   jax.ShapeDtypeStruct((DT_PAD, s), F32),
                   jax.ShapeDtypeStruct((D_INNER, s), F32)],
        compiler_params=_cparams(1),
        name="ssd_sample_prep",
    )(proj, proj, dt_raw, conv_state_t, lw["conv_w"], lw["conv_b"], lw["dt_bias"], lw["a_log"],
      lw["head_expand"])


def _ssd_sample_step_kernel(st_ref, decht_ref, xdtt_ref, b_ref, c_ref, *rest, fill_other_layers):
    st_o, yt_o = rest[-2:]
    s = pl.program_id(0)
    n_s = xdtt_ref.shape[1]
    onehot = lax.broadcasted_iota(jnp.int32, (GROUP_W, n_s), 1) == s
    onehot_h = lax.broadcasted_iota(jnp.int32, (DT_PAD, n_s), 1) == s

    @pl.when(s == 0)
    def _():
        yt_o[...] = jnp.zeros(yt_o.shape, F32)

    if fill_other_layers:
        st_o[1:] = jnp.zeros((st_o.shape[0] - 1,) + st_o.shape[1:], F32)

    dech = jnp.sum(jnp.where(onehot_h, decht_ref[...], 0.0), axis=-1, keepdims=True)
    b_all = b_ref[pl.ds(s, 1), :]
    c_all = c_ref[pl.ds(s, 1), :]
    for g in range(N_GROUPS):
        rows = slice(g * GROUP_W, (g + 1) * GROUP_W)
        cols = slice(g * D_STATE, (g + 1) * D_STATE)
        xdt = jnp.sum(jnp.where(onehot, xdtt_ref[rows, :], 0.0), axis=-1, keepdims=True)
        b_row = b_all[:, cols]
        c_row = c_all[:, cols]
        upd = xdt * b_row
        parts = []
        for hh in range(HEADS_PER_GROUP):
            h = g * HEADS_PER_GROUP + hh
            r_h = slice(h * HEAD_DIM, (h + 1) * HEAD_DIM)
            parts.append(st_ref[0, 0, r_h, :] * dech[h:h + 1, :]
                         + upd[hh * HEAD_DIM:(hh + 1) * HEAD_DIM, :])
        st_new = jnp.concatenate(parts, axis=0)
        st_o[0, 0, rows, :] = st_new
        y_col = jnp.sum(st_new * c_row, axis=-1, keepdims=True)
        yt_o[rows, :] = yt_o[rows, :] + jnp.where(onehot, y_col, 0.0)


def _ssd_sample_step(state_all, layer, prev_out, decht, xdtt, b_m, c_m):
    depth = state_all.shape[0]
    s = xdtt.shape[1]
    full = lambda shape: pl.BlockSpec(shape, lambda i: (0,) * len(shape))
    in_specs = [pl.BlockSpec((1, 1, D_INNER, D_STATE), lambda i: (layer, i, 0, 0)),
                full((DT_PAD, s)), full((D_INNER, s)),
                full((s, N_GROUPS * D_STATE)), full((s, N_GROUPS * D_STATE))]
    args = [state_all, decht, xdtt, b_m, c_m]
    if prev_out is None:
        assert layer == 0
        aliases = {}
        st_out = pl.BlockSpec((depth, 1, D_INNER, D_STATE), lambda i: (0, i, 0, 0))
    else:
        in_specs.append(pl.BlockSpec(memory_space=pl.ANY))
        args.append(prev_out)
        aliases = {len(args) - 1: 0}
        st_out = pl.BlockSpec((1, 1, D_INNER, D_STATE), lambda i: (layer, i, 0, 0))
    return pl.pallas_call(
        functools.partial(_ssd_sample_step_kernel, fill_other_layers=prev_out is None),
        grid=(s,),
        in_specs=in_specs,
        out_specs=[st_out, full((D_INNER, s))],
        out_shape=[jax.ShapeDtypeStruct(state_all.shape, F32),
                   jax.ShapeDtypeStruct((D_INNER, s), F32)],
        input_output_aliases=aliases,
        compiler_params=_cparams(1),
        name="ssd_sample_step",
    )(*args)


def _ssd_sample_finish_kernel(yt_ref, xs_ref, zs_ref, dsk_ref, nw_ref, y_ref):
    y = yt_ref[...].T
    for g in range(N_GROUPS):
        cols = slice(g * GROUP_W, (g + 1) * GROUP_W)
        yv = y[:, cols] + dsk_ref[:, cols] * xs_ref[:, cols]
        v = yv * zs_ref[:, cols]
        ms = jnp.mean(v * v, axis=-1, keepdims=True)
        y_ref[:, cols] = (v * lax.rsqrt(ms + RMS_EPS) * nw_ref[:, cols]).astype(y_ref.dtype)


def _ssd_sample_finish(yt, xs, proj, lw):
    s = xs.shape[0]
    full = lambda shape: pl.BlockSpec(shape, lambda i: (0,) * len(shape))
    return pl.pallas_call(
        _ssd_sample_finish_kernel,
        grid=(1,),
        in_specs=[full((D_INNER, s)), full((s, D_INNER)),
                  pl.BlockSpec((s, D_INNER), lambda i: (0, OFF_Z // D_INNER)),
                  full((1, D_INNER)), full((1, D_INNER))],
        out_specs=full((s, D_INNER)),
        out_shape=jax.ShapeDtypeStruct((s, D_INNER), BF16),
        compiler_params=_cparams(1),
        name="ssd_sample_finish",
    )(yt, xs, proj, lw["d_skip_x"], lw["norm_w"])


POOL_TP = 256
POOL_HALO = 16


def _pool_prompt_kernel(u_ref, o_ref, ext_s):
    c = pl.program_id(1)
    tp = POOL_TP

    @pl.when(c == 0)
    def _():
        ext_s[0:POOL_HALO, :] = jnp.zeros((POOL_HALO, D_POOL), F32)

    ext_s[POOL_HALO:POOL_HALO + tp, :] = u_ref[...]
    pos = c * tp + lax.broadcasted_iota(jnp.int32, (tp, 1), 0)
    for gi, win in enumerate(POOL_WINDOWS):
        cols = slice(gi * POOL_GC, (gi + 1) * POOL_GC)
        u = ext_s[POOL_HALO:POOL_HALO + tp, cols]
        tot = u
        for k in range(1, win):
            tot = tot + ext_s[POOL_HALO - k:POOL_HALO - k + tp, cols]
        cnt = jnp.minimum(win, pos + 1).astype(F32)
        o_ref[:, cols] = (tot / cnt - u).astype(o_ref.dtype)
    ext_s[0:POOL_HALO, :] = ext_s[tp:tp + POOL_HALO, :]


def _pool_prompt(proj, bsz, seq):
    nt = seq // POOL_TP
    return pl.pallas_call(
        _pool_prompt_kernel,
        grid=(bsz, nt),
        in_specs=[pl.BlockSpec((POOL_TP, D_POOL), lambda b, c: (b * nt + c, OFF_POOL // D_POOL))],
        out_specs=pl.BlockSpec((POOL_TP, D_POOL), lambda b, c: (b * nt + c, 0)),
        out_shape=jax.ShapeDtypeStruct((bsz * seq, D_POOL), BF16),
        scratch_shapes=[pltpu.VMEM((POOL_HALO + POOL_TP, D_POOL), F32)],
        compiler_params=_cparams(2),
        name="pool_prompt",
    )(proj)


def _pool_sample_kernel(u_ref, buf_ref, o_ref):
    for gi, win in enumerate(POOL_WINDOWS):
        cols = slice(gi * POOL_GC, (gi + 1) * POOL_GC)
        u = u_ref[:, cols]
        tot = u
        for k in range(1, win):
            tot = tot + buf_ref[POOL_BUF - k][:, cols]
        cnt = float(min(win, PAST_LEN + 1))
        o_ref[:, cols] = (tot / cnt - u).astype(o_ref.dtype)


def _pool_sample(proj, pool_state_t):
    s = proj.shape[0]
    return pl.pallas_call(
        _pool_sample_kernel,
        grid=(1,),
        in_specs=[pl.BlockSpec((s, D_POOL), lambda i: (0, OFF_POOL // D_POOL)),
                  pl.BlockSpec((POOL_BUF, s, D_POOL), lambda i: (0, 0, 0))],
        out_specs=pl.BlockSpec((s, D_POOL), lambda i: (0, 0)),
        out_shape=jax.ShapeDtypeStruct((s, D_POOL), BF16),
        compiler_params=_cparams(1),
        name="pool_sample",
    )(proj, pool_state_t)


def _pool_mm_kernel(p_ref, w_ref, sc_ref, o_ref, w_s):
    @pl.when(pl.program_id(1) == 0)
    def _():
        w_s[...] = w_ref[...].astype(BF16)

    acc = jnp.dot(p_ref[...], w_s[...], preferred_element_type=F32)
    o_ref[...] = acc * sc_ref[...]


def _pool_mm(pooled, w_pool, layer, scale, tm):
    m = pooled.shape[0]
    ng = len(POOL_WINDOWS)
    return pl.pallas_call(
        _pool_mm_kernel,
        grid=(ng, m // tm),
        in_specs=[pl.BlockSpec((tm, POOL_GC), lambda g, i: (i, g)),
                  pl.BlockSpec((None, None, POOL_GC, POOL_GC), lambda g, i: (layer, g, 0, 0)),
                  pl.BlockSpec((1, POOL_GC), lambda g, i: (0, g))],
        out_specs=pl.BlockSpec((tm, POOL_GC), lambda g, i: (i, g)),
        out_shape=jax.ShapeDtypeStruct((m, D_POOL), F32),
        scratch_shapes=[pltpu.VMEM((POOL_GC, POOL_GC), BF16)],
        compiler_params=_cparams(2),
        name="pool_mm",
    )(pooled, w_pool, scale)


def _branch_merge_kernel(y_ref, w_ref, ga_ref, gb_ref, yb_ref, o_ref, w_s):
    @pl.when(pl.program_id(1) == 0)
    def _():
        w_s[...] = w_ref[...].astype(BF16)

    y_a = jnp.dot(y_ref[...], w_s[...], preferred_element_type=F32)
    o_ref[...] = (ga_ref[...] * y_a + gb_ref[...] * yb_ref[...]).astype(o_ref.dtype)


def _branch_merge(y, w_br, layer, proj, y_b, tm, tn):
    m = y.shape[0]
    nj = D_MODEL // tn
    ga0 = OFF_GATE // tn
    return pl.pallas_call(
        _branch_merge_kernel,
        grid=(nj, m // tm),
        in_specs=[pl.BlockSpec((tm, D_INNER), lambda j, i: (i, 0)),
                  pl.BlockSpec((None, D_INNER, tn), lambda j, i: (layer, 0, j)),
                  pl.BlockSpec((tm, tn), lambda j, i: (i, ga0 + j)),
                  pl.BlockSpec((tm, tn), lambda j, i: (i, ga0 + nj + j)),
                  pl.BlockSpec((tm, tn), lambda j, i: (i, j))],
        out_specs=pl.BlockSpec((tm, tn), lambda j, i: (i, j)),
        out_shape=jax.ShapeDtypeStruct((m, D_MODEL), BF16),
        scratch_shapes=[pltpu.VMEM((D_INNER, tn), BF16)],
        compiler_params=_cparams(2),
        name="branch_merge",
    )(y, w_br, proj, proj, y_b)


def _proj_ln_kernel(m_ref, w_ref, x_ref, g_ref, b_ref, o_ref, obf_ref, w_s):
    @pl.when(pl.program_id(0) == 0)
    def _():
        w_s[...] = w_ref[...].astype(BF16)

    acc = jnp.dot(m_ref[...], w_s[...], preferred_element_type=F32)
    y = _layer_norm(ALPHA * x_ref[...] + acc, g_ref[...], b_ref[...])
    o_ref[...] = y
    obf_ref[...] = y.astype(BF16)


def _proj_ln(mix, w, layer, x, g, b, tm):
    m, k = mix.shape
    return pl.pallas_call(
        _proj_ln_kernel,
        grid=(m // tm,),
        in_specs=[pl.BlockSpec((tm, k), lambda i: (i, 0)),
                  pl.BlockSpec((None, k, D_MODEL), lambda i: (layer, 0, 0),
                               pipeline_mode=pl.Buffered(1)),
                  pl.BlockSpec((tm, D_MODEL), lambda i: (i, 0)),
                  pl.BlockSpec((1, D_MODEL), lambda i: (0, 0)),
                  pl.BlockSpec((1, D_MODEL), lambda i: (0, 0))],
        out_specs=[pl.BlockSpec((tm, D_MODEL), lambda i: (i, 0)),
                   pl.BlockSpec((tm, D_MODEL), lambda i: (i, 0))],
        out_shape=[jax.ShapeDtypeStruct((m, D_MODEL), F32),
                   jax.ShapeDtypeStruct((m, D_MODEL), BF16)],
        scratch_shapes=[pltpu.VMEM((k, D_MODEL), BF16)],
        compiler_params=_cparams(1),
        name="proj_ln",
    )(mix, w, x, g, b)


FFN_HALO = SUBLANES
FFN_VAL_BLK = D_FF // FFN_TN
FFN_VAL_SHIFT = D_FF % FFN_TN
assert FFN_VAL_SHIFT % LANES == 0


def _ffn_weight_tiles(j, wg_ref, wva_ref, wvb_ref, wg_s, wv_s):
    col = j * FFN_TN + lax.broadcasted_iota(jnp.int32, (1, FFN_TN), 1)
    valid = col < D_FF
    wg_s[...] = jnp.where(valid, wg_ref[...], 0.0).astype(BF16)
    wv = jnp.concatenate([wva_ref[:, FFN_VAL_SHIFT:], wvb_ref[:, :FFN_VAL_SHIFT]], axis=1)
    wv_s[...] = jnp.where(valid, wv, 0.0).astype(BF16)


def _ffn_w_specs(layer, index_of):
    blk = (None, D_MODEL, FFN_TN)
    return [pl.BlockSpec(blk, index_of(lambda j: (layer, 0, j))),
            pl.BlockSpec(blk, index_of(lambda j: (layer, 0, FFN_VAL_BLK + j))),
            pl.BlockSpec(blk, index_of(lambda j: (layer, 0, FFN_VAL_BLK + j + 1)))]


def _ffn_conv_gate(hg_s, hv_s, cw_g, cw_v, cb_g, cb_v, tm):
    def conv(h_s, cw, cb):
        acc = cb[...]
        for j in range(FFN_K):
            lo = FFN_HALO - (FFN_K - 1) + j
            acc = acc + h_s[lo:lo + tm, :] * cw[j:j + 1, :]
        return acc
    return _silu(conv(hg_s, cw_g, cb_g)) * conv(hv_s, cw_v, cb_v)


def _ffn_up_prompt_kernel(x_ref, wg_ref, wva_ref, wvb_ref, cwg_ref, cwv_ref, cbg_ref, cbv_ref,
                          a_ref, tg_ref, tv_ref, wg_s, wv_s, hg_s, hv_s, *, tiles_per_seq):
    j = pl.program_id(0)
    i = pl.program_id(1)
    tm = x_ref.shape[0]

    @pl.when(i == 0)
    def _():
        _ffn_weight_tiles(j, wg_ref, wva_ref, wvb_ref, wg_s, wv_s)

    @pl.when(i % tiles_per_seq == 0)
    def _():
        hg_s[0:FFN_HALO, :] = jnp.zeros((FFN_HALO, FFN_TN), F32)
        hv_s[0:FFN_HALO, :] = jnp.zeros((FFN_HALO, FFN_TN), F32)

    x = x_ref[...]
    hg_s[FFN_HALO:FFN_HALO + tm, :] = jnp.dot(x, wg_s[...], preferred_element_type=F32)
    hv_s[FFN_HALO:FFN_HALO + tm, :] = jnp.dot(x, wv_s[...], preferred_element_type=F32)
    a_ref[...] = _ffn_conv_gate(hg_s, hv_s, cwg_ref, cwv_ref, cbg_ref, cbv_ref, tm).astype(a_ref.dtype)
    tail_g = hg_s[tm:tm + FFN_HALO, :]
    tail_v = hv_s[tm:tm + FFN_HALO, :]
    hg_s[0:FFN_HALO, :] = tail_g
    hv_s[0:FFN_HALO, :] = tail_v
    tg_ref[0] = tail_g
    tv_ref[0] = tail_v


def _ffn_up_prompt(x_bf, lw, layer, seq, tm):
    m = x_bf.shape[0]
    nj = D_FF_PAD // FFN_TN
    ni = m // tm
    kern = functools.partial(_ffn_up_prompt_kernel, tiles_per_seq=seq // tm)
    cspec_g = lambda r: pl.BlockSpec((r, FFN_TN), lambda j, i: (0, j))
    cspec_v = lambda r: pl.BlockSpec((r, FFN_TN), lambda j, i: (0, nj + j))
    w_specs = _ffn_w_specs(layer, lambda f: (lambda j, i: f(j)))
    return pl.pallas_call(
        kern,
        grid=(nj, ni),
        in_specs=[pl.BlockSpec((tm, D_MODEL), lambda j, i: (i, 0)), *w_specs,
                  cspec_g(FFN_K), cspec_v(FFN_K), cspec_g(1), cspec_v(1)],
        out_specs=[pl.BlockSpec((tm, FFN_TN), lambda j, i: (i, j)),
                   pl.BlockSpec((1, FFN_HALO, FFN_TN), lambda j, i: (i, 0, j)),
                   pl.BlockSpec((1, FFN_HALO, FFN_TN), lambda j, i: (i, 0, j))],
        out_shape=[jax.ShapeDtypeStruct((m, D_FF_PAD), BF16),
                   jax.ShapeDtypeStruct((ni, FFN_HALO, D_FF_PAD), F32),
                   jax.ShapeDtypeStruct((ni, FFN_HALO, D_FF_PAD), F32)],
        scratch_shapes=[pltpu.VMEM((D_MODEL, FFN_TN), BF16),
                        pltpu.VMEM((D_MODEL, FFN_TN), BF16),
                        pltpu.VMEM((FFN_HALO + tm, FFN_TN), F32),
                        pltpu.VMEM((FFN_HALO + tm, FFN_TN), F32)],
        compiler_params=_cparams(2),
        name="ffn_up_prompt",
    )(x_bf, lw["w_up"], lw["w_up"], lw["w_up"],
      lw["fconv_w"], lw["fconv_w"], lw["fconv_b"], lw["fconv_b"])


def _ffn_up_sample_kernel(x_ref, wg_ref, wva_ref, wvb_ref, sg_ref, sv_ref, cwg_ref, cwv_ref,
                          cbg_ref, cbv_ref, a_ref, hg_ref, hv_ref, wg_s, wv_s):
    _ffn_weight_tiles(pl.program_id(0), wg_ref, wva_ref, wvb_ref, wg_s, wv_s)
    x = x_ref[...]
    hg = jnp.dot(x, wg_s[...], preferred_element_type=F32)
    hv = jnp.dot(x, wv_s[...], preferred_element_type=F32)
    hg_ref[...] = hg
    hv_ref[...] = hv

    def conv(h, st, cw, cb):
        acc = cb[...]
        for j in range(FFN_K - 1):
            acc = acc + st[j] * cw[j:j + 1, :]
        return acc + h * cw[FFN_K - 1:FFN_K, :]

    a_ref[...] = (_silu(conv(hg, sg_ref, cwg_ref, cbg_ref))
                  * conv(hv, sv_ref, cwv_ref, cbv_ref)).astype(a_ref.dtype)


def _ffn_up_sample(x_bf, ffn_state_t, lw, layer):
    s = x_bf.shape[0]
    nj = D_FF_PAD // FFN_TN
    g_blk = lambda r: pl.BlockSpec((r, FFN_TN), lambda j: (0, j))
    v_blk = lambda r: pl.BlockSpec((r, FFN_TN), lambda j: (0, nj + j))
    w_specs = _ffn_w_specs(layer, lambda f: f)
    return pl.pallas_call(
        _ffn_up_sample_kernel,
        grid=(nj,),
        in_specs=[pl.BlockSpec((s, D_MODEL), lambda j: (0, 0)), *w_specs,
                  pl.BlockSpec((FFN_K - 1, s, FFN_TN), lambda j: (0, 0, j)),
                  pl.BlockSpec((FFN_K - 1, s, FFN_TN), lambda j: (0, 0, nj + j)),
                  g_blk(FFN_K), v_blk(FFN_K), g_blk(1), v_blk(1)],
        out_specs=[pl.BlockSpec((s, FFN_TN), lambda j: (0, j)),
                   pl.BlockSpec((s, FFN_TN), lambda j: (0, j)),
                   pl.BlockSpec((s, FFN_TN), lambda j: (0, j))],
        out_shape=[jax.ShapeDtypeStruct((s, D_FF_PAD), BF16),
                   jax.ShapeDtypeStruct((s, D_FF_PAD), F32),
                   jax.ShapeDtypeStruct((s, D_FF_PAD), F32)],
        scratch_shapes=[pltpu.VMEM((D_MODEL, FFN_TN), BF16),
                        pltpu.VMEM((D_MODEL, FFN_TN), BF16)],
        compiler_params=_cparams(1),
        name="ffn_up_sample",
    )(x_bf, lw["w_up"], lw["w_up"], lw["w_up"], ffn_state_t, ffn_state_t,
      lw["fconv_w"], lw["fconv_w"], lw["fconv_b"], lw["fconv_b"])


DOWN_TK = 512


def _down_ln_kernel(a_ref, w_ref, x_ref, g_ref, b_ref, o_ref, obf_ref):
    k = pl.program_id(1)

    @pl.when(k == 0)
    def _():
        o_ref[...] = ALPHA * x_ref[...]

    row = k * DOWN_TK + lax.broadcasted_iota(jnp.int32, (DOWN_TK, 1), 0)
    w = jnp.where(row < D_FF, w_ref[...], 0.0).astype(BF16)
    o_ref[...] += jnp.dot(a_ref[...], w, preferred_element_type=F32)

    @pl.when(k == pl.num_programs(1) - 1)
    def _():
        y = _layer_norm(o_ref[...], g_ref[...], b_ref[...])
        o_ref[...] = y
        obf_ref[...] = y.astype(BF16)


def _down_ln(a, w, layer, x, g, b, tm):
    m = a.shape[0]
    nk = D_FF_PAD // DOWN_TK
    return pl.pallas_call(
        _down_ln_kernel,
        grid=(m // tm, nk),
        in_specs=[pl.BlockSpec((tm, DOWN_TK), lambda i, k: (i, k)),
                  pl.BlockSpec((None, DOWN_TK, D_MODEL), lambda i, k: (layer, k, 0)),
                  pl.BlockSpec((tm, D_MODEL), lambda i, k: (i, 0), pipeline_mode=pl.Buffered(1)),
                  pl.BlockSpec((1, D_MODEL), lambda i, k: (0, 0)),
                  pl.BlockSpec((1, D_MODEL), lambda i, k: (0, 0))],
        out_specs=[pl.BlockSpec((tm, D_MODEL), lambda i, k: (i, 0)),
                   pl.BlockSpec((tm, D_MODEL), lambda i, k: (i, 0))],
        out_shape=[jax.ShapeDtypeStruct((m, D_MODEL), F32),
                   jax.ShapeDtypeStruct((m, D_MODEL), BF16)],
        compiler_params=_cparams(2),
        name="down_ln",
    )(a, w, x, g, b)


def _pad_ff(v):
    pad = [(0, 0)] * (v.ndim - 1) + [(0, D_FF_PAD - D_FF)]
    return jnp.concatenate([jnp.pad(v[..., :D_FF], pad), jnp.pad(v[..., D_FF:], pad)], axis=-1)


def _unpad_ff(v):
    return jnp.concatenate([v[..., :D_FF], v[..., D_FF_PAD:D_FF_PAD + D_FF]], axis=-1)


def _prep_layer(big, b_gate, conv_w, conv_b, dt_bias, a_log, d_skip, norm_w, pool_scale,
                ln1_g, ln1_b, fconv_w, fconv_b, ln2_g, ln2_b):
    pad_h = lambda v: jnp.pad(v, (0, DT_PAD - N_HEADS)).reshape(1, DT_PAD)
    head_of_channel = jnp.arange(D_INNER) // HEAD_DIM
    head_expand = (jnp.arange(DT_PAD)[:, None] == head_of_channel[None, :]).astype(BF16)
    return dict(
        big, b_gate=b_gate.reshape(1, -1),
        conv_w=conv_w, conv_b=conv_b.reshape(1, -1),
        dt_bias=pad_h(dt_bias), a_log=pad_h(a_log),
        d_skip_x=jnp.repeat(d_skip, HEAD_DIM).reshape(1, -1),
        norm_w=norm_w.reshape(1, -1), head_expand=head_expand,
        pool_scale=pool_scale.reshape(1, -1),
        ln1_g=ln1_g.reshape(1, -1), ln1_b=ln1_b.reshape(1, -1),
        fconv_w=_pad_ff(fconv_w), fconv_b=_pad_ff(fconv_b).reshape(1, -1),
        ln2_g=ln2_g.reshape(1, -1), ln2_b=ln2_b.reshape(1, -1),
    )


def _raw_xbc(proj):
    return proj[:, OFF_XS:OFF_XS + CONV_DIM]


def _layer_prompt(x, x_bf, lw, layer, bsz, seq):
    tm_in = 1024
    proj, xbc_tail = _in_proj(x_bf, lw, layer, tm_in, seq // tm_in, conv=True)
    dt_raw = _dt_proj(x_bf, lw["w_in"], layer, 1024)
    y, new_ssm = _ssd_prompt(proj, dt_raw, lw, bsz, seq)
    pooled = _pool_prompt(proj, bsz, seq)
    y_b = _pool_mm(pooled, lw["w_pool"], layer, lw["pool_scale"], 1024)
    mix = _branch_merge(y, lw["w_br"], layer, proj, y_b, 512, 512)
    x1, x1_bf = _proj_ln(mix, lw["w_out"], layer, x, lw["ln1_g"], lw["ln1_b"], 256)
    tm_up = 1024
    act, tail_g, tail_v = _ffn_up_prompt(x1_bf, lw, layer, seq, tm_up)
    x2, x2_bf = _down_ln(act, lw["w_down"], layer, x1, lw["ln2_g"], lw["ln2_b"], 1024)
    p3 = proj.reshape(bsz, seq, N_MAIN)
    new_pool = p3[:, seq - POOL_BUF:, OFF_POOL:OFF_POOL + D_POOL]
    tps_in = seq // tm_in
    new_conv = xbc_tail[tps_in - 1::tps_in, CONV_HALO - (CONV_K - 1):, OFF_XS:OFF_XS + CONV_DIM]
    tps = seq // tm_up
    last = slice(tps - 1, None, tps)
    tail = jnp.concatenate([tail_g[last, :, :D_FF], tail_v[last, :, :D_FF]], axis=-1)
    new_ffn = tail[:, FFN_HALO - (FFN_K - 1):, :]
    return x2, x2_bf, new_ssm, new_conv, new_pool, new_ffn


def _layer_sample(x, x_bf, ssm_all, layer, ssm_prev_out, s_conv, s_pool, s_ffn, lw):
    s = x.shape[0]
    proj, _ = _in_proj(x_bf, lw, layer, s, 1, conv=False)
    dt_raw = _dt_proj(x_bf, lw["w_in"], layer, s)
    xs, b_m, c_m, decht, xdtt = _ssd_sample_prep(proj, dt_raw, jnp.swapaxes(s_conv, 0, 1), lw)
    new_ssm, yt = _ssd_sample_step(ssm_all, layer, ssm_prev_out, decht, xdtt, b_m, c_m)
    y = _ssd_sample_finish(yt, xs, proj, lw)
    pooled = _pool_sample(proj, jnp.swapaxes(s_pool, 0, 1))
    y_b = _pool_mm(pooled, lw["w_pool"], layer, lw["pool_scale"], s)
    mix = _branch_merge(y, lw["w_br"], layer, proj, y_b, s, 512)
    x1, x1_bf = _proj_ln(mix, lw["w_out"], layer, x, lw["ln1_g"], lw["ln1_b"], s)
    act, h_g, h_v = _ffn_up_sample(x1_bf, _pad_ff(jnp.swapaxes(s_ffn, 0, 1)), lw, layer)
    x2, x2_bf = _down_ln(act, lw["w_down"], layer, x1, lw["ln2_g"], lw["ln2_b"], s)
    new_conv = jnp.concatenate([s_conv[:, 1:], _raw_xbc(proj)[:, None, :]], axis=1)
    new_pool = jnp.concatenate([s_pool[:, 1:], proj[:, None, OFF_POOL:OFF_POOL + D_POOL]], axis=1)
    h_new = jnp.concatenate([h_g[:, :D_FF], h_v[:, :D_FF]], axis=-1)
    new_ffn = jnp.concatenate([s_ffn[:, 1:], h_new[:, None, :]], axis=1)
    return x2, x2_bf, new_ssm, new_conv, new_pool, new_ffn


def kernel(x_prompt, x_sample, state_ssm, state_ssd_conv, state_pool, state_ffn_conv, w_in, b_gate, conv_w, conv_b, dt_bias, a_log, d_skip, ssd_norm_w, w_ssd_branch, w_pool, pool_scale, w_out, ln1_g, ln1_b, w_up, ffn_conv_w, ffn_conv_b, w_down, ln2_g, ln2_b):
    bsz, seq, _ = x_prompt.shape
    n_s = x_sample.shape[0]
    assert x_sample.shape[1] == 1 and seq % 1024 == 0
    xp = x_prompt.reshape(bsz * seq, D_MODEL)
    xs = x_sample.reshape(n_s, D_MODEL)
    xp_bf, xs_bf = xp.astype(BF16), xs.astype(BF16)
    outs_p, outs_s = [], []
    ssm_all = state_ssm.reshape(DEPTH, n_s, D_INNER, D_STATE)
    ssm_out = None
    big = dict(w_in=w_in, w_br=w_ssd_branch, w_pool=w_pool, w_out=w_out, w_up=w_up, w_down=w_down)
    for i in range(DEPTH):
        lw = _prep_layer(big, b_gate[i], conv_w[i], conv_b[i], dt_bias[i], a_log[i], d_skip[i],
                         ssd_norm_w[i], pool_scale[i], ln1_g[i], ln1_b[i], ffn_conv_w[i],
                         ffn_conv_b[i], ln2_g[i], ln2_b[i])
        xp, xp_bf, *op = _layer_prompt(xp, xp_bf, lw, i, bsz, seq)
        xs, xs_bf, ssm_out, *os_ = _layer_sample(xs, xs_bf, ssm_all, i, ssm_out, state_ssd_conv[i],
                                                 state_pool[i], state_ffn_conv[i], lw)
        outs_p.append(op)
        outs_s.append(os_)
    stack = lambda outs, k: jnp.stack([o[k] for o in outs])
    return (xp.reshape(bsz, seq, D_MODEL), xs.reshape(n_s, 1, D_MODEL),
            stack(outs_p, 0), stack(outs_p, 1), stack(outs_p, 2), stack(outs_p, 3),
            ssm_out.reshape(state_ssm.shape), stack(outs_s, 0), stack(outs_s, 1), stack(outs_s, 2))
```

```python
import functools

import jax
import jax.numpy as jnp
from jax import lax
from jax.experimental import pallas as pl
from jax.experimental.pallas import tpu as pltpu

F32 = jnp.float32
BF16 = jnp.bfloat16

D_MODEL = 2048
HEAD_DIM = 64
D_INNER = 2 * D_MODEL
N_HEADS = D_INNER // HEAD_DIM
N_GROUPS = 8
HEADS_PER_GROUP = N_HEADS // N_GROUPS
GROUP_W = D_INNER // N_GROUPS
D_STATE = 128
CONV_K = 4
BC_W = 2 * N_GROUPS * D_STATE
CONV_DIM = D_INNER + BC_W
CHUNK = 128
D_POOL = D_MODEL
POOL_WINDOWS = (2, 4, 8, 16)
POOL_GC = D_POOL // len(POOL_WINDOWS)
POOL_BUF = max(POOL_WINDOWS) - 1
D_FF = 5504
FFN_K = 3
DEPTH = 2
PAST_LEN = 16384
ALPHA = (2 * DEPTH) ** 0.25
LN_EPS = 1e-5
RMS_EPS = 1e-5

LANES = 128
SUBLANES = 8
D_FF_PAD = 5632
FFN_TN = 512
DT_PAD = LANES
OFF_Z = 0
OFF_XS = D_INNER
OFF_BC = 2 * D_INNER
OFF_POOL = 2 * D_INNER + BC_W
OFF_GATE = OFF_POOL + D_POOL
N_MAIN = OFF_GATE + 2 * D_MODEL
VMEM_LIMIT = 56 * 1024 * 1024
NEG_BIG = -1e30


def _cparams(n_axes):
    return pltpu.CompilerParams(dimension_semantics=("arbitrary",) * n_axes,
                                vmem_limit_bytes=VMEM_LIMIT)


def _sigmoid(x):
    return 1.0 / (1.0 + jnp.exp(-x))


def _silu(x):
    return x * _sigmoid(x)


def _softplus(x):
    return jnp.maximum(x, 0.0) + jnp.log(1.0 + jnp.exp(-jnp.abs(x)))


def _layer_norm(r, g, b):
    mu = jnp.mean(r, axis=-1, keepdims=True)
    d = r - mu
    var = jnp.mean(d * d, axis=-1, keepdims=True)
    return d * lax.rsqrt(var + LN_EPS) * g + b


def _dt_proj_kernel(x_ref, w_ref, o_ref):
    o_ref[...] = lax.dot_general(x_ref[...], w_ref[...].astype(BF16), (((1,), (1,)), ((), ())),
                                 preferred_element_type=F32)


def _dt_proj(x_bf, w_in_t, layer, tm):
    m = x_bf.shape[0]
    return pl.pallas_call(
        _dt_proj_kernel,
        grid=(m // tm,),
        in_specs=[pl.BlockSpec((tm, D_MODEL), lambda i: (i, 0)),
                  pl.BlockSpec((None, DT_PAD, D_MODEL), lambda i: (layer, DT_COL_BLOCK, 0))],
        out_specs=pl.BlockSpec((tm, DT_PAD), lambda i: (i, 0)),
        out_shape=jax.ShapeDtypeStruct((m, DT_PAD), F32),
        compiler_params=_cparams(1),
        name="dt_proj",
    )(x_bf, w_in_t)


IN_TN = 1024
J_XBC = OFF_XS // IN_TN
J_POOL = OFF_POOL // IN_TN
J_GATE = OFF_GATE // IN_TN
J_END = N_MAIN // IN_TN
CONV_HALO = SUBLANES
PG_SHIFT = N_HEADS
DT_COL_BLOCK = (2 * D_INNER + BC_W) // DT_PAD


def _in_proj_kernel(x_ref, wa_ref, wb_ref, cw_ref, cb_ref, bg_ref, o_ref, tail_ref, w_s, h_s,
                    *, tiles_per_seq, conv):
    j = pl.program_id(0)
    i = pl.program_id(1)
    tm = x_ref.shape[0]

    @pl.when((i == 0) & (j < J_POOL))
    def _():
        w_s[...] = wa_ref[...].astype(BF16)

    @pl.when((i == 0) & (j >= J_POOL))
    def _():
        w_s[...] = jnp.concatenate([wa_ref[PG_SHIFT:, :], wb_ref[:PG_SHIFT, :]], axis=0).astype(BF16)

    def mm():
        return lax.dot_general(x_ref[...], w_s[...], (((1,), (1,)), ((), ())),
                               preferred_element_type=F32)

    is_xbc = (j >= J_XBC) & (j < J_POOL)

    @pl.when(jnp.logical_not(is_xbc))
    def _():
        tail_ref[0] = jnp.zeros((CONV_HALO, IN_TN), F32)

    @pl.when(j < J_XBC)
    def _():
        o_ref[...] = _silu(mm())

    @pl.when(is_xbc)
    def _():
        if conv:
            @pl.when(i % tiles_per_seq == 0)
            def _():
                h_s[0:CONV_HALO, :] = jnp.zeros((CONV_HALO, IN_TN), F32)

            h_s[CONV_HALO:CONV_HALO + tm, :] = mm()
            acc = cb_ref[...]
            for k in range(CONV_K):
                lo = CONV_HALO - (CONV_K - 1) + k
                acc = acc + h_s[lo:lo + tm, :] * cw_ref[k:k + 1, :]
            o_ref[...] = _silu(acc)
            tail = h_s[tm:tm + CONV_HALO, :]
            h_s[0:CONV_HALO, :] = tail
            tail_ref[0] = tail
        else:
            o_ref[...] = mm()
            tail_ref[0] = jnp.zeros((CONV_HALO, IN_TN), F32)

    @pl.when((j >= J_POOL) & (j < J_GATE))
    def _():
        o_ref[...] = mm()

    @pl.when(j >= J_GATE)
    def _():
        o_ref[...] = _sigmoid(mm() + bg_ref[...])


def _in_proj(x_bf, lw, layer, tm, tiles_per_seq, conv):
    m = x_bf.shape[0]
    ni = m // tm
    n_conv_tiles = CONV_DIM // IN_TN
    clamp = lambda v, lo, hi: jnp.minimum(jnp.maximum(v, lo), hi)
    conv_tile = lambda j: clamp(j - J_XBC, 0, n_conv_tiles - 1)
    kern = functools.partial(_in_proj_kernel, tiles_per_seq=tiles_per_seq, conv=conv)
    return pl.pallas_call(
        kern,
        grid=(J_END, ni),
        in_specs=[pl.BlockSpec((tm, D_MODEL), lambda j, i: (i, 0)),
                  pl.BlockSpec((None, IN_TN, D_MODEL), lambda j, i: (layer, j, 0)),
                  pl.BlockSpec((None, IN_TN, D_MODEL), lambda j, i: (layer, jnp.maximum(j + 1, J_POOL), 0),
                               pipeline_mode=pl.Buffered(1)),
                  pl.BlockSpec((CONV_K, IN_TN), lambda j, i: (0, conv_tile(j))),
                  pl.BlockSpec((1, IN_TN), lambda j, i: (0, conv_tile(j))),
                  pl.BlockSpec((1, IN_TN), lambda j, i: (0, clamp(j - J_GATE, 0, J_END - J_GATE - 1)))],
        out_specs=[pl.BlockSpec((tm, IN_TN), lambda j, i: (i, j)),
                   pl.BlockSpec((1, CONV_HALO, IN_TN), lambda j, i: (i, 0, j))],
        out_shape=[jax.ShapeDtypeStruct((m, N_MAIN), F32),
                   jax.ShapeDtypeStruct((ni, CONV_HALO, N_MAIN), F32)],
        scratch_shapes=[pltpu.VMEM((IN_TN, D_MODEL), BF16),
                        pltpu.VMEM((CONV_HALO + tm, IN_TN), F32)],
        compiler_params=_cparams(2),
        name="in_proj",
    )(x_bf, lw["w_in_t"], lw["w_in_t"], lw["conv_w"], lw["conv_b"], lw["b_gate"])


def _split3(v):
    hi = v.astype(BF16)
    r1 = v - hi.astype(F32)
    mid = r1.astype(BF16)
    lo = (r1 - mid.astype(F32)).astype(BF16)
    return hi, mid, lo


def _ssd_prompt_kernel(xs_ref, bc_ref, zs_ref, dt_ref, dtb_ref, alog_ref, dsk_ref, nw_ref,
                       y_ref, st_ref,
                       xs_s, bt_s, c_s, y_s, state_s, acol_s, arow_s, dtrow_s):
    c = pl.program_id(1)
    n_chunks = pl.num_programs(1)
    q = CHUNK

    @pl.when(c == 0)
    def _():
        state_s[...] = jnp.zeros(state_s.shape, F32)

    for g in range(N_GROUPS):
        xs_s[g] = xs_ref[:, g * GROUP_W:(g + 1) * GROUP_W]
        bt_s[g] = bc_ref[:, g * D_STATE:(g + 1) * D_STATE].T
        c_s[g] = bc_ref[:, (N_GROUPS + g) * D_STATE:(N_GROUPS + g + 1) * D_STATE].astype(BF16)

    dt = _softplus(dt_ref[...] + dtb_ref[...])
    a_neg = -jnp.exp(alog_ref[...])
    d_a = dt * a_neg
    row = lax.broadcasted_iota(jnp.int32, (q, q), 0)
    col = lax.broadcasted_iota(jnp.int32, (q, q), 1)
    causal = row >= col
    tril = jnp.where(causal, 1.0, 0.0).astype(BF16)
    hi, mid, lo3 = _split3(d_a)
    a_cum = (jnp.dot(tril, hi, preferred_element_type=F32)
             + jnp.dot(tril, mid, preferred_element_type=F32)
             + jnp.dot(tril, lo3, preferred_element_type=F32))
    arow_s[...] = a_cum.T
    dtrow_s[...] = dt.T
    for g in range(N_GROUPS):
        sh = (LANES - HEADS_PER_GROUP * g) % LANES
        acol_s[g] = a_cum if sh == 0 else pltpu.roll(a_cum, sh, 1)

    lane = lax.broadcasted_iota(jnp.int32, (q, LANES), 1)
    lo_half = lane < HEAD_DIM

    def group_body(g, carry):
        acol = acol_s[g]
        c_g = c_s[g]
        bt_g = bt_s[g]
        cb = jnp.dot(c_g, bt_g.astype(BF16), preferred_element_type=F32)
        y_off_g = jnp.dot(c_g, state_s[g].astype(BF16), preferred_element_type=F32)
        for k in range(HEADS_PER_GROUP // 2):
            l_parts, b_parts, a_b = [], [], []
            for e in range(2):
                hh = 2 * k + e
                head = g * HEADS_PER_GROUP + hh
                a_col = jnp.broadcast_to(acol[:, hh:hh + 1], (q, q))
                a_row = arow_s[pl.ds(head, 1), :]
                dt_row = dtrow_s[pl.ds(head, 1), :]
                seg = jnp.where(causal, a_col - a_row, NEG_BIG)
                l_parts.append((cb * jnp.exp(seg) * dt_row).astype(BF16))
                w_row = dt_row * jnp.exp(a_col[q - 1:q, :] - a_row)
                b_parts.append((bt_g * w_row).astype(BF16))
                a_b.append(a_col)
            lhs = jnp.concatenate([jnp.concatenate(l_parts, axis=1),
                                   jnp.concatenate(b_parts, axis=1)], axis=0)
            cols = slice(k * LANES, (k + 1) * LANES)
            xs_bf = xs_s[g, :, cols].astype(BF16)
            zero = jnp.zeros_like(xs_bf)
            rhs = jnp.concatenate([jnp.where(lo_half, xs_bf, zero),
                                   jnp.where(lo_half, zero, xs_bf)], axis=0)
            res = jnp.dot(lhs, rhs, preferred_element_type=F32)
            a_pair = jnp.where(lo_half, a_b[0], a_b[1])
            y_s[g, :, cols] = res[0:q] + y_off_g[:, cols] * jnp.exp(a_pair)
            cdec = jnp.exp(a_pair[q - 1:q, :])
            state_s[g, :, cols] = state_s[g, :, cols] * cdec + res[q:2 * q]
        return carry

    lax.fori_loop(0, N_GROUPS, group_body, 0)

    for g in range(N_GROUPS):
        cols = slice(g * GROUP_W, (g + 1) * GROUP_W)
        yv = y_s[g] + dsk_ref[:, cols] * xs_s[g]
        v = yv * zs_ref[:, cols]
        ms = jnp.mean(v * v, axis=-1, keepdims=True)
        y_ref[:, cols] = (v * lax.rsqrt(ms + RMS_EPS) * nw_ref[:, cols]).astype(y_ref.dtype)

    @pl.when(c == n_chunks - 1)
    def _():
        for g in range(N_GROUPS):
            st_ref[0, g] = state_s[g].T


def _ssd_prompt(proj, dt_raw, lw, bsz, seq):
    n_chunks = seq // CHUNK
    rows = lambda b, c: b * n_chunks + c
    small = lambda shape: pl.BlockSpec(shape, lambda b, c: (0, 0))
    y, st = pl.pallas_call(
        _ssd_prompt_kernel,
        grid=(bsz, n_chunks),
        in_specs=[
            pl.BlockSpec((CHUNK, D_INNER), lambda b, c: (rows(b, c), OFF_XS // D_INNER)),
            pl.BlockSpec((CHUNK, BC_W), lambda b, c: (rows(b, c), OFF_BC // BC_W)),
            pl.BlockSpec((CHUNK, D_INNER), lambda b, c: (rows(b, c), OFF_Z // D_INNER)),
            pl.BlockSpec((CHUNK, DT_PAD), lambda b, c: (rows(b, c), 0)),
            small((1, DT_PAD)), small((1, DT_PAD)), small((1, D_INNER)), small((1, D_INNER)),
        ],
        out_specs=[
            pl.BlockSpec((CHUNK, D_INNER), lambda b, c: (rows(b, c), 0)),
            pl.BlockSpec((1, N_GROUPS, GROUP_W, D_STATE), lambda b, c: (b, 0, 0, 0)),
        ],
        out_shape=[jax.ShapeDtypeStruct((bsz * seq, D_INNER), BF16),
                   jax.ShapeDtypeStruct((bsz, N_GROUPS, GROUP_W, D_STATE), F32)],
        scratch_shapes=[
            pltpu.VMEM((N_GROUPS, CHUNK, GROUP_W), F32),
            pltpu.VMEM((N_GROUPS, D_STATE, CHUNK), F32),
            pltpu.VMEM((N_GROUPS, CHUNK, D_STATE), BF16),
            pltpu.VMEM((N_GROUPS, CHUNK, GROUP_W), F32),
            pltpu.VMEM((N_GROUPS, D_STATE, GROUP_W), F32),
            pltpu.VMEM((N_GROUPS, CHUNK, LANES), F32),
            pltpu.VMEM((LANES, CHUNK), F32),
            pltpu.VMEM((LANES, CHUNK), F32),
        ],
        compiler_params=_cparams(2),
        name="ssd_prompt",
    )(proj, proj, proj, dt_raw, lw["dt_bias"], lw["a_log"], lw["d_skip_x"], lw["norm_w"])
    return y, st.reshape(bsz, N_HEADS, HEAD_DIM, D_STATE)


def _ssd_sample_prep_kernel(xs_ref, bc_ref, dt_ref, cst_ref, cw_ref, cb_ref, dtb_ref, alog_ref,
                            ex_ref, xs_o, b_o, c_o, decht_o, xdtt_o):
    def conv(u, lo, hi):
        acc = cb_ref[:, lo:hi]
        for j in range(CONV_K - 1):
            acc = acc + cst_ref[j][:, lo:hi] * cw_ref[j:j + 1, lo:hi]
        acc = acc + u * cw_ref[CONV_K - 1:CONV_K, lo:hi]
        return _silu(acc)

    xs = conv(xs_ref[...], 0, D_INNER)
    bc = conv(bc_ref[...], D_INNER, CONV_DIM)
    xs_o[...] = xs
    b_o[...] = bc[:, 0:N_GROUPS * D_STATE]
    c_o[...] = bc[:, N_GROUPS * D_STATE:BC_W]
    dt = _softplus(dt_ref[...] + dtb_ref[...])
    d_a = dt * (-jnp.exp(alog_ref[...]))
    ex = ex_ref[...]

    def expand(v):
        hi, mid, lo3 = _split3(v)
        return (jnp.dot(hi, ex, preferred_element_type=F32)
                + jnp.dot(mid, ex, preferred_element_type=F32)
                + jnp.dot(lo3, ex, preferred_element_type=F32))

    xdt = expand(dt) * xs
    decht_o[...] = jnp.exp(d_a).T
    xdtt_o[...] = xdt.T


def _ssd_sample_prep(proj, dt_raw, conv_state_t, lw):
    s = proj.shape[0]
    full = lambda shape: pl.BlockSpec(shape, lambda i: (0,) * len(shape))
    return pl.pallas_call(
        _ssd_sample_prep_kernel,
        grid=(1,),
        in_specs=[
            pl.BlockSpec((s, D_INNER), lambda i: (0, OFF_XS // D_INNER)),
            pl.BlockSpec((s, BC_W), lambda i: (0, OFF_BC // BC_W)),
            full((s, DT_PAD)), full((CONV_K - 1, s, CONV_DIM)),
            full((CONV_K, CONV_DIM)), full((1, CONV_DIM)), full((1, DT_PAD)), full((1, DT_PAD)),
            full((DT_PAD, D_INNER)),
        ],
        out_specs=[full((s, D_INNER)), full((s, N_GROUPS * D_STATE)), full((s, N_GROUPS * D_STATE)),
                   full((DT_PAD, s)), full((D_INNER, s))],
        out_shape=[jax.ShapeDtypeStruct((s, D_INNER), F32),
                   jax.ShapeDtypeStruct((s, N_GROUPS * D_STATE), F32),
                   jax.ShapeDtypeStruct((s, N_GROUPS * D_STATE), F32),
                   jax.ShapeDtypeStruct((DT_PAD, s), F32),
                   jax.ShapeDtypeStruct((D_INNER, s), F32)],
        compiler_params=_cparams(1),
        name="ssd_sample_prep",
    )(proj, proj, dt_raw, conv_state_t, lw["conv_w"], lw["conv_b"], lw["dt_bias"], lw["a_log"],
      lw["head_expand"])


def _ssd_sample_step_kernel(st_ref, decht_ref, xdtt_ref, b_ref, c_ref, *rest, fill_other_layers):
    st_o, yt_o = rest[-2:]
    s = pl.program_id(0)
    n_s = xdtt_ref.shape[1]
    onehot = lax.broadcasted_iota(jnp.int32, (GROUP_W, n_s), 1) == s
    onehot_h = lax.broadcasted_iota(jnp.int32, (DT_PAD, n_s), 1) == s

    @pl.when(s == 0)
    def _():
        yt_o[...] = jnp.zeros(yt_o.shape, F32)

    if fill_other_layers:
        st_o[1:] = jnp.zeros((st_o.shape[0] - 1,) + st_o.shape[1:], F32)

    dech = jnp.sum(jnp.where(onehot_h, decht_ref[...], 0.0), axis=-1, keepdims=True)
    b_all = b_ref[pl.ds(s, 1), :]
    c_all = c_ref[pl.ds(s, 1), :]
    for g in range(N_GROUPS):
        rows = slice(g * GROUP_W, (g + 1) * GROUP_W)
        cols = slice(g * D_STATE, (g + 1) * D_STATE)
        xdt = jnp.sum(jnp.where(onehot, xdtt_ref[rows, :], 0.0), axis=-1, keepdims=True)
        b_row = b_all[:, cols]
        c_row = c_all[:, cols]
        upd = xdt * b_row
        parts = []
        for hh in range(HEADS_PER_GROUP):
            h = g * HEADS_PER_GROUP + hh
            r_h = slice(h * HEAD_DIM, (h + 1) * HEAD_DIM)
            parts.append(st_ref[0, 0, r_h, :] * dech[h:h + 1, :]
                         + upd[hh * HEAD_DIM:(hh + 1) * HEAD_DIM, :])
        st_new = jnp.concatenate(parts, axis=0)
        st_o[0, 0, rows, :] = st_new
        y_col = jnp.sum(st_new * c_row, axis=-1, keepdims=True)
        yt_o[rows, :] = yt_o[rows, :] + jnp.where(onehot, y_col, 0.0)


def _ssd_sample_step(state_all, layer, prev_out, decht, xdtt, b_m, c_m):
    depth = state_all.shape[0]
    s = xdtt.shape[1]
    full = lambda shape: pl.BlockSpec(shape, lambda i: (0,) * len(shape))
    in_specs = [pl.BlockSpec((1, 1, D_INNER, D_STATE), lambda i: (layer, i, 0, 0)),
                full((DT_PAD, s)), full((D_INNER, s)),
                full((s, N_GROUPS * D_STATE)), full((s, N_GROUPS * D_STATE))]
    args = [state_all, decht, xdtt, b_m, c_m]
    if prev_out is None:
        assert layer == 0
        aliases = {}
        st_out = pl.BlockSpec((depth, 1, D_INNER, D_STATE), lambda i: (0, i, 0, 0))
    else:
        in_specs.append(pl.BlockSpec(memory_space=pl.ANY))
        args.append(prev_out)
        aliases = {len(args) - 1: 0}
        st_out = pl.BlockSpec((1, 1, D_INNER, D_STATE), lambda i: (layer, i, 0, 0))
    return pl.pallas_call(
        functools.partial(_ssd_sample_step_kernel, fill_other_layers=prev_out is None),
        grid=(s,),
        in_specs=in_specs,
        out_specs=[st_out, full((D_INNER, s))],
        out_shape=[jax.ShapeDtypeStruct(state_all.shape, F32),
                   jax.ShapeDtypeStruct((D_INNER, s), F32)],
        input_output_aliases=aliases,
        compiler_params=_cparams(1),
        name="ssd_sample_step",
    )(*args)


def _ssd_sample_finish_kernel(yt_ref, xs_ref, zs_ref, dsk_ref, nw_ref, y_ref):
    y = yt_ref[...].T
    for g in range(N_GROUPS):
        cols = slice(g * GROUP_W, (g + 1) * GROUP_W)
        yv = y[:, cols] + dsk_ref[:, cols] * xs_ref[:, cols]
        v = yv * zs_ref[:, cols]
        ms = jnp.mean(v * v, axis=-1, keepdims=True)
        y_ref[:, cols] = (v * lax.rsqrt(ms + RMS_EPS) * nw_ref[:, cols]).astype(y_ref.dtype)


def _ssd_sample_finish(yt, xs, proj, lw):
    s = xs.shape[0]
    full = lambda shape: pl.BlockSpec(shape, lambda i: (0,) * len(shape))
    return pl.pallas_call(
        _ssd_sample_finish_kernel,
        grid=(1,),
        in_specs=[full((D_INNER, s)), full((s, D_INNER)),
                  pl.BlockSpec((s, D_INNER), lambda i: (0, OFF_Z // D_INNER)),
                  full((1, D_INNER)), full((1, D_INNER))],
        out_specs=full((s, D_INNER)),
        out_shape=jax.ShapeDtypeStruct((s, D_INNER), BF16),
        compiler_params=_cparams(1),
        name="ssd_sample_finish",
    )(yt, xs, proj, lw["d_skip_x"], lw["norm_w"])


POOL_TP = 256
POOL_HALO = 16


def _pool_prompt_kernel(u_ref, o_ref, ext_s):
    c = pl.program_id(1)
    tp = POOL_TP

    @pl.when(c == 0)
    def _():
        ext_s[0:POOL_HALO, :] = jnp.zeros((POOL_HALO, D_POOL), F32)

    ext_s[POOL_HALO:POOL_HALO + tp, :] = u_ref[...]
    pos = c * tp + lax.broadcasted_iota(jnp.int32, (tp, 1), 0)
    for gi, win in enumerate(POOL_WINDOWS):
        cols = slice(gi * POOL_GC, (gi + 1) * POOL_GC)
        u = ext_s[POOL_HALO:POOL_HALO + tp, cols]
        tot = u
        for k in range(1, win):
            tot = tot + ext_s[POOL_HALO - k:POOL_HALO - k + tp, cols]
        cnt = jnp.minimum(win, pos + 1).astype(F32)
        o_ref[:, cols] = (tot / cnt - u).astype(o_ref.dtype)
    ext_s[0:POOL_HALO, :] = ext_s[tp:tp + POOL_HALO, :]


def _pool_prompt(proj, bsz, seq):
    nt = seq // POOL_TP
    return pl.pallas_call(
        _pool_prompt_kernel,
        grid=(bsz, nt),
        in_specs=[pl.BlockSpec((POOL_TP, D_POOL), lambda b, c: (b * nt + c, OFF_POOL // D_POOL))],
        out_specs=pl.BlockSpec((POOL_TP, D_POOL), lambda b, c: (b * nt + c, 0)),
        out_shape=jax.ShapeDtypeStruct((bsz * seq, D_POOL), BF16),
        scratch_shapes=[pltpu.VMEM((POOL_HALO + POOL_TP, D_POOL), F32)],
        compiler_params=_cparams(2),
        name="pool_prompt",
    )(proj)


def _pool_sample_kernel(u_ref, buf_ref, o_ref):
    for gi, win in enumerate(POOL_WINDOWS):
        cols = slice(gi * POOL_GC, (gi + 1) * POOL_GC)
        u = u_ref[:, cols]
        tot = u
        for k in range(1, win):
            tot = tot + buf_ref[POOL_BUF - k][:, cols]
        cnt = float(min(win, PAST_LEN + 1))
        o_ref[:, cols] = (tot / cnt - u).astype(o_ref.dtype)


def _pool_sample(proj, pool_state_t):
    s = proj.shape[0]
    return pl.pallas_call(
        _pool_sample_kernel,
        grid=(1,),
        in_specs=[pl.BlockSpec((s, D_POOL), lambda i: (0, OFF_POOL // D_POOL)),
                  pl.BlockSpec((POOL_BUF, s, D_POOL), lambda i: (0, 0, 0))],
        out_specs=pl.BlockSpec((s, D_POOL), lambda i: (0, 0)),
        out_shape=jax.ShapeDtypeStruct((s, D_POOL), BF16),
        compiler_params=_cparams(1),
        name="pool_sample",
    )(proj, pool_state_t)


def _pool_mm_kernel(p_ref, w_ref, sc_ref, o_ref, w_s):
    @pl.when(pl.program_id(1) == 0)
    def _():
        w_s[...] = w_ref[...].astype(BF16)

    acc = jnp.dot(p_ref[...], w_s[...], preferred_element_type=F32)
    o_ref[...] = acc * sc_ref[...]


def _pool_mm(pooled, w_pool, layer, scale, tm):
    m = pooled.shape[0]
    ng = len(POOL_WINDOWS)
    return pl.pallas_call(
        _pool_mm_kernel,
        grid=(ng, m // tm),
        in_specs=[pl.BlockSpec((tm, POOL_GC), lambda g, i: (i, g)),
                  pl.BlockSpec((None, None, POOL_GC, POOL_GC), lambda g, i: (layer, g, 0, 0)),
                  pl.BlockSpec((1, POOL_GC), lambda g, i: (0, g))],
        out_specs=pl.BlockSpec((tm, POOL_GC), lambda g, i: (i, g)),
        out_shape=jax.ShapeDtypeStruct((m, D_POOL), F32),
        scratch_shapes=[pltpu.VMEM((POOL_GC, POOL_GC), BF16)],
        compiler_params=_cparams(2),
        name="pool_mm",
    )(pooled, w_pool, scale)


def _branch_merge_kernel(y_ref, w_ref, ga_ref, gb_ref, yb_ref, o_ref, w_s):
    @pl.when(pl.program_id(1) == 0)
    def _():
        w_s[...] = w_ref[...].astype(BF16)

    y_a = jnp.dot(y_ref[...], w_s[...], preferred_element_type=F32)
    o_ref[...] = (ga_ref[...] * y_a + gb_ref[...] * yb_ref[...]).astype(o_ref.dtype)


def _branch_merge(y, w_br, layer, proj, y_b, tm, tn):
    m = y.shape[0]
    nj = D_MODEL // tn
    ga0 = OFF_GATE // tn
    return pl.pallas_call(
        _branch_merge_kernel,
        grid=(nj, m // tm),
        in_specs=[pl.BlockSpec((tm, D_INNER), lambda j, i: (i, 0)),
                  pl.BlockSpec((None, D_INNER, tn), lambda j, i: (layer, 0, j)),
                  pl.BlockSpec((tm, tn), lambda j, i: (i, ga0 + j)),
                  pl.BlockSpec((tm, tn), lambda j, i: (i, ga0 + nj + j)),
                  pl.BlockSpec((tm, tn), lambda j, i: (i, j))],
        out_specs=pl.BlockSpec((tm, tn), lambda j, i: (i, j)),
        out_shape=jax.ShapeDtypeStruct((m, D_MODEL), BF16),
        scratch_shapes=[pltpu.VMEM((D_INNER, tn), BF16)],
        compiler_params=_cparams(2),
        name="branch_merge",
    )(y, w_br, proj, proj, y_b)


def _proj_ln_kernel(m_ref, w_ref, x_ref, g_ref, b_ref, o_ref, obf_ref, w_s):
    @pl.when(pl.program_id(0) == 0)
    def _():
        w_s[...] = w_ref[...].astype(BF16)

    acc = jnp.dot(m_ref[...], w_s[...], preferred_element_type=F32)
    y = _layer_norm(ALPHA * x_ref[...] + acc, g_ref[...], b_ref[...])
    o_ref[...] = y
    obf_ref[...] = y.astype(BF16)


def _proj_ln(mix, w, layer, x, g, b, tm):
    m, k = mix.shape
    return pl.pallas_call(
        _proj_ln_kernel,
        grid=(m // tm,),
        in_specs=[pl.BlockSpec((tm, k), lambda i: (i, 0)),
                  pl.BlockSpec((None, k, D_MODEL), lambda i: (layer, 0, 0),
                               pipeline_mode=pl.Buffered(1)),
                  pl.BlockSpec((tm, D_MODEL), lambda i: (i, 0)),
                  pl.BlockSpec((1, D_MODEL), lambda i: (0, 0)),
                  pl.BlockSpec((1, D_MODEL), lambda i: (0, 0))],
        out_specs=[pl.BlockSpec((tm, D_MODEL), lambda i: (i, 0)),
                   pl.BlockSpec((tm, D_MODEL), lambda i: (i, 0))],
        out_shape=[jax.ShapeDtypeStruct((m, D_MODEL), F32),
                   jax.ShapeDtypeStruct((m, D_MODEL), BF16)],
        scratch_shapes=[pltpu.VMEM((k, D_MODEL), BF16)],
        compiler_params=_cparams(1),
        name="proj_ln",
    )(mix, w, x, g, b)


FFN_HALO = SUBLANES
FFN_VAL_BLK = D_FF // FFN_TN
FFN_VAL_SHIFT = D_FF % FFN_TN
assert FFN_VAL_SHIFT % LANES == 0


def _ffn_weight_tiles(j, wg_ref, wva_ref, wvb_ref, wg_s, wv_s):
    col = j * FFN_TN + lax.broadcasted_iota(jnp.int32, (1, FFN_TN), 1)
    valid = col < D_FF
    wg_s[...] = jnp.where(valid, wg_ref[...], 0.0).astype(BF16)
    wv = jnp.concatenate([wva_ref[:, FFN_VAL_SHIFT:], wvb_ref[:, :FFN_VAL_SHIFT]], axis=1)
    wv_s[...] = jnp.where(valid, wv, 0.0).astype(BF16)


def _ffn_w_specs(layer, index_of):
    blk = (None, D_MODEL, FFN_TN)
    return [pl.BlockSpec(blk, index_of(lambda j: (layer, 0, j))),
            pl.BlockSpec(blk, index_of(lambda j: (layer, 0, FFN_VAL_BLK + j))),
            pl.BlockSpec(blk, index_of(lambda j: (layer, 0, FFN_VAL_BLK + j + 1)))]


def _ffn_conv_gate(hg_s, hv_s, cw_g, cw_v, cb_g, cb_v, tm):
    def conv(h_s, cw, cb):
        acc = cb[...]
        for j in range(FFN_K):
            lo = FFN_HALO - (FFN_K - 1) + j
            acc = acc + h_s[lo:lo + tm, :] * cw[j:j + 1, :]
        return acc
    return _silu(conv(hg_s, cw_g, cb_g)) * conv(hv_s, cw_v, cb_v)


def _ffn_up_prompt_kernel(x_ref, wg_ref, wva_ref, wvb_ref, cwg_ref, cwv_ref, cbg_ref, cbv_ref,
                          a_ref, tg_ref, tv_ref, wg_s, wv_s, hg_s, hv_s, *, tiles_per_seq):
    j = pl.program_id(0)
    i = pl.program_id(1)
    tm = x_ref.shape[0]

    @pl.when(i == 0)
    def _():
        _ffn_weight_tiles(j, wg_ref, wva_ref, wvb_ref, wg_s, wv_s)

    @pl.when(i % tiles_per_seq == 0)
    def _():
        hg_s[0:FFN_HALO, :] = jnp.zeros((FFN_HALO, FFN_TN), F32)
        hv_s[0:FFN_HALO, :] = jnp.zeros((FFN_HALO, FFN_TN), F32)

    x = x_ref[...]
    hg_s[FFN_HALO:FFN_HALO + tm, :] = jnp.dot(x, wg_s[...], preferred_element_type=F32)
    hv_s[FFN_HALO:FFN_HALO + tm, :] = jnp.dot(x, wv_s[...], preferred_element_type=F32)
    a_ref[...] = _ffn_conv_gate(hg_s, hv_s, cwg_ref, cwv_ref, cbg_ref, cbv_ref, tm).astype(a_ref.dtype)
    tail_g = hg_s[tm:tm + FFN_HALO, :]
    tail_v = hv_s[tm:tm + FFN_HALO, :]
    hg_s[0:FFN_HALO, :] = tail_g
    hv_s[0:FFN_HALO, :] = tail_v
    tg_ref[0] = tail_g
    tv_ref[0] = tail_v


def _ffn_up_prompt(x_bf, lw, layer, seq, tm):
    m = x_bf.shape[0]
    nj = D_FF_PAD // FFN_TN
    ni = m // tm
    kern = functools.partial(_ffn_up_prompt_kernel, tiles_per_seq=seq // tm)
    cspec_g = lambda r: pl.BlockSpec((r, FFN_TN), lambda j, i: (0, j))
    cspec_v = lambda r: pl.BlockSpec((r, FFN_TN), lambda j, i: (0, nj + j))
    w_specs = _ffn_w_specs(layer, lambda f: (lambda j, i: f(j)))
    return pl.pallas_call(
        kern,
        grid=(nj, ni),
        in_specs=[pl.BlockSpec((tm, D_MODEL), lambda j, i: (i, 0)), *w_specs,
                  cspec_g(FFN_K), cspec_v(FFN_K), cspec_g(1), cspec_v(1)],
        out_specs=[pl.BlockSpec((tm, FFN_TN), lambda j, i: (i, j)),
                   pl.BlockSpec((1, FFN_HALO, FFN_TN), lambda j, i: (i, 0, j)),
                   pl.BlockSpec((1, FFN_HALO, FFN_TN), lambda j, i: (i, 0, j))],
        out_shape=[jax.ShapeDtypeStruct((m, D_FF_PAD), BF16),
                   jax.ShapeDtypeStruct((ni, FFN_HALO, D_FF_PAD), F32),
                   jax.ShapeDtypeStruct((ni, FFN_HALO, D_FF_PAD), F32)],
        scratch_shapes=[pltpu.VMEM((D_MODEL, FFN_TN), BF16),
                        pltpu.VMEM((D_MODEL, FFN_TN), BF16),
                        pltpu.VMEM((FFN_HALO + tm, FFN_TN), F32),
                        pltpu.VMEM((FFN_HALO + tm, FFN_TN), F32)],
        compiler_params=_cparams(2),
        name="ffn_up_prompt",
    )(x_bf, lw["w_up"], lw["w_up"], lw["w_up"],
      lw["fconv_w"], lw["fconv_w"], lw["fconv_b"], lw["fconv_b"])


def _ffn_up_sample_kernel(x_ref, wg_ref, wva_ref, wvb_ref, sg_ref, sv_ref, cwg_ref, cwv_ref,
                          cbg_ref, cbv_ref, a_ref, hg_ref, hv_ref, wg_s, wv_s):
    _ffn_weight_tiles(pl.program_id(0), wg_ref, wva_ref, wvb_ref, wg_s, wv_s)
    x = x_ref[...]
    hg = jnp.dot(x, wg_s[...], preferred_element_type=F32)
    hv = jnp.dot(x, wv_s[...], preferred_element_type=F32)
    hg_ref[...] = hg
    hv_ref[...] = hv

    def conv(h, st, cw, cb):
        acc = cb[...]
        for j in range(FFN_K - 1):
            acc = acc + st[j] * cw[j:j + 1, :]
        return acc + h * cw[FFN_K - 1:FFN_K, :]

    a_ref[...] = (_silu(conv(hg, sg_ref, cwg_ref, cbg_ref))
                  * conv(hv, sv_ref, cwv_ref, cbv_ref)).astype(a_ref.dtype)


def _ffn_up_sample(x_bf, ffn_state_t, lw, layer):
    s = x_bf.shape[0]
    nj = D_FF_PAD // FFN_TN
    g_blk = lambda r: pl.BlockSpec((r, FFN_TN), lambda j: (0, j))
    v_blk = lambda r: pl.BlockSpec((r, FFN_TN), lambda j: (0, nj + j))
    w_specs = _ffn_w_specs(layer, lambda f: f)
    return pl.pallas_call(
        _ffn_up_sample_kernel,
        grid=(nj,),
        in_specs=[pl.BlockSpec((s, D_MODEL), lambda j: (0, 0)), *w_specs,
                  pl.BlockSpec((FFN_K - 1, s, FFN_TN), lambda j: (0, 0, j)),
                  pl.BlockSpec((FFN_K - 1, s, FFN_TN), lambda j: (0, 0, nj + j)),
                  g_blk(FFN_K), v_blk(FFN_K), g_blk(1), v_blk(1)],
        out_specs=[pl.BlockSpec((s, FFN_TN), lambda j: (0, j)),
                   pl.BlockSpec((s, FFN_TN), lambda j: (0, j)),
                   pl.BlockSpec((s, FFN_TN), lambda j: (0, j))],
        out_shape=[jax.ShapeDtypeStruct((s, D_FF_PAD), BF16),
                   jax.ShapeDtypeStruct((s, D_FF_PAD), F32),
                   jax.ShapeDtypeStruct((s, D_FF_PAD), F32)],
        scratch_shapes=[pltpu.VMEM((D_MODEL, FFN_TN), BF16),
                        pltpu.VMEM((D_MODEL, FFN_TN), BF16)],
        compiler_params=_cparams(1),
        name="ffn_up_sample",
    )(x_bf, lw["w_up"], lw["w_up"], lw["w_up"], ffn_state_t, ffn_state_t,
      lw["fconv_w"], lw["fconv_w"], lw["fconv_b"], lw["fconv_b"])


DOWN_TK = 512


def _down_ln_kernel(a_ref, w_ref, x_ref, g_ref, b_ref, o_ref, obf_ref):
    k = pl.program_id(1)

    @pl.when(k == 0)
    def _():
        o_ref[...] = ALPHA * x_ref[...]

    row = k * DOWN_TK + lax.broadcasted_iota(jnp.int32, (DOWN_TK, 1), 0)
    w = jnp.where(row < D_FF, w_ref[...], 0.0).astype(BF16)
    o_ref[...] += jnp.dot(a_ref[...], w, preferred_element_type=F32)

    @pl.when(k == pl.num_programs(1) - 1)
    def _():
        y = _layer_norm(o_ref[...], g_ref[...], b_ref[...])
        o_ref[...] = y
        obf_ref[...] = y.astype(BF16)


def _down_ln(a, w, layer, x, g, b, tm):
    m = a.shape[0]
    nk = D_FF_PAD // DOWN_TK
    return pl.pallas_call(
        _down_ln_kernel,
        grid=(m // tm, nk),
        in_specs=[pl.BlockSpec((tm, DOWN_TK), lambda i, k: (i, k)),
                  pl.BlockSpec((None, DOWN_TK, D_MODEL), lambda i, k: (layer, k, 0)),
                  pl.BlockSpec((tm, D_MODEL), lambda i, k: (i, 0), pipeline_mode=pl.Buffered(1)),
                  pl.BlockSpec((1, D_MODEL), lambda i, k: (0, 0)),
                  pl.BlockSpec((1, D_MODEL), lambda i, k: (0, 0))],
        out_specs=[pl.BlockSpec((tm, D_MODEL), lambda i, k: (i, 0)),
                   pl.BlockSpec((tm, D_MODEL), lambda i, k: (i, 0))],
        out_shape=[jax.ShapeDtypeStruct((m, D_MODEL), F32),
                   jax.ShapeDtypeStruct((m, D_MODEL), BF16)],
        compiler_params=_cparams(2),
        name="down_ln",
    )(a, w, x, g, b)


def _pad_ff(v):
    pad = [(0, 0)] * (v.ndim - 1) + [(0, D_FF_PAD - D_FF)]
    return jnp.concatenate([jnp.pad(v[..., :D_FF], pad), jnp.pad(v[..., D_FF:], pad)], axis=-1)


def _unpad_ff(v):
    return jnp.concatenate([v[..., :D_FF], v[..., D_FF_PAD:D_FF_PAD + D_FF]], axis=-1)


def _prep_layer(big, b_gate, conv_w, conv_b, dt_bias, a_log, d_skip, norm_w, pool_scale,
                ln1_g, ln1_b, fconv_w, fconv_b, ln2_g, ln2_b):
    pad_h = lambda v: jnp.pad(v, (0, DT_PAD - N_HEADS)).reshape(1, DT_PAD)
    head_of_channel = jnp.arange(D_INNER) // HEAD_DIM
    head_expand = (jnp.arange(DT_PAD)[:, None] == head_of_channel[None, :]).astype(BF16)
    return dict(
        big, b_gate=b_gate.reshape(1, -1),
        conv_w=conv_w, conv_b=conv_b.reshape(1, -1),
        dt_bias=pad_h(dt_bias), a_log=pad_h(a_log),
        d_skip_x=jnp.repeat(d_skip, HEAD_DIM).reshape(1, -1),
        norm_w=norm_w.reshape(1, -1), head_expand=head_expand,
        pool_scale=pool_scale.reshape(1, -1),
        ln1_g=ln1_g.reshape(1, -1), ln1_b=ln1_b.reshape(1, -1),
        fconv_w=_pad_ff(fconv_w), fconv_b=_pad_ff(fconv_b).reshape(1, -1),
        ln2_g=ln2_g.reshape(1, -1), ln2_b=ln2_b.reshape(1, -1),
    )


def _raw_xbc(proj):
    return proj[:, OFF_XS:OFF_XS + CONV_DIM]


def _layer_prompt(x, x_bf, lw, layer, bsz, seq):
    tm_in = 1024
    proj, xbc_tail = _in_proj(x_bf, lw, layer, tm_in, seq // tm_in, conv=True)
    dt_raw = _dt_proj(x_bf, lw["w_in_t"], layer, 1024)
    y, new_ssm = _ssd_prompt(proj, dt_raw, lw, bsz, seq)
    pooled = _pool_prompt(proj, bsz, seq)
    y_b = _pool_mm(pooled, lw["w_pool"], layer, lw["pool_scale"], 1024)
    mix = _branch_merge(y, lw["w_br"], layer, proj, y_b, 512, 512)
    x1, x1_bf = _proj_ln(mix, lw["w_out"], layer, x, lw["ln1_g"], lw["ln1_b"], 256)
    tm_up = 1024
    act, tail_g, tail_v = _ffn_up_prompt(x1_bf, lw, layer, seq, tm_up)
    x2, x2_bf = _down_ln(act, lw["w_down"], layer, x1, lw["ln2_g"], lw["ln2_b"], 1024)
    p3 = proj.reshape(bsz, seq, N_MAIN)
    new_pool = p3[:, seq - POOL_BUF:, OFF_POOL:OFF_POOL + D_POOL]
    tps_in = seq // tm_in
    new_conv = xbc_tail[tps_in - 1::tps_in, CONV_HALO - (CONV_K - 1):, OFF_XS:OFF_XS + CONV_DIM]
    tps = seq // tm_up
    last = slice(tps - 1, None, tps)
    tail = jnp.concatenate([tail_g[last, :, :D_FF], tail_v[last, :, :D_FF]], axis=-1)
    new_ffn = tail[:, FFN_HALO - (FFN_K - 1):, :]
    return x2, x2_bf, new_ssm, new_conv, new_pool, new_ffn


def _layer_sample(x, x_bf, ssm_all, layer, ssm_prev_out, s_conv, s_pool, s_ffn, lw):
    s = x.shape[0]
    proj, _ = _in_proj(x_bf, lw, layer, s, 1, conv=False)
    dt_raw = _dt_proj(x_bf, lw["w_in_t"], layer, s)
    xs, b_m, c_m, decht, xdtt = _ssd_sample_prep(proj, dt_raw, jnp.swapaxes(s_conv, 0, 1), lw)
    new_ssm, yt = _ssd_sample_step(ssm_all, layer, ssm_prev_out, decht, xdtt, b_m, c_m)
    y = _ssd_sample_finish(yt, xs, proj, lw)
    pooled = _pool_sample(proj, jnp.swapaxes(s_pool, 0, 1))
    y_b = _pool_mm(pooled, lw["w_pool"], layer, lw["pool_scale"], s)
    mix = _branch_merge(y, lw["w_br"], layer, proj, y_b, s, 512)
    x1, x1_bf = _proj_ln(mix, lw["w_out"], layer, x, lw["ln1_g"], lw["ln1_b"], s)
    act, h_g, h_v = _ffn_up_sample(x1_bf, _pad_ff(jnp.swapaxes(s_ffn, 0, 1)), lw, layer)
    x2, x2_bf = _down_ln(act, lw["w_down"], layer, x1, lw["ln2_g"], lw["ln2_b"], s)
    new_conv = jnp.concatenate([s_conv[:, 1:], _raw_xbc(proj)[:, None, :]], axis=1)
    new_pool = jnp.concatenate([s_pool[:, 1:], proj[:, None, OFF_POOL:OFF_POOL + D_POOL]], axis=1)
    h_new = jnp.concatenate([h_g[:, :D_FF], h_v[:, :D_FF]], axis=-1)
    new_ffn = jnp.concatenate([s_ffn[:, 1:], h_new[:, None, :]], axis=1)
    return x2, x2_bf, new_ssm, new_conv, new_pool, new_ffn


def kernel(x_prompt, x_sample, state_ssm, state_ssd_conv, state_pool, state_ffn_conv, w_in, b_gate, conv_w, conv_b, dt_bias, a_log, d_skip, ssd_norm_w, w_ssd_branch, w_pool, pool_scale, w_out, ln1_g, ln1_b, w_up, ffn_conv_w, ffn_conv_b, w_down, ln2_g, ln2_b):
    bsz, seq, _ = x_prompt.shape
    n_s = x_sample.shape[0]
    assert x_sample.shape[1] == 1 and seq % 1024 == 0
    xp = x_prompt.reshape(bsz * seq, D_MODEL)
    xs = x_sample.reshape(n_s, D_MODEL)
    xp_bf, xs_bf = xp.astype(BF16), xs.astype(BF16)
    outs_p, outs_s = [], []
    ssm_all = state_ssm.reshape(DEPTH, n_s, D_INNER, D_STATE)
    ssm_out = None
    big = dict(w_in_t=jnp.swapaxes(w_in, 1, 2), w_br=w_ssd_branch, w_pool=w_pool, w_out=w_out,
               w_up=w_up, w_down=w_down)
    for i in range(DEPTH):
        lw = _prep_layer(big, b_gate[i], conv_w[i], conv_b[i], dt_bias[i], a_log[i], d_skip[i],
                         ssd_norm_w[i], pool_scale[i], ln1_g[i], ln1_b[i], ffn_conv_w[i],
                         ffn_conv_b[i], ln2_g[i], ln2_b[i])
        xp, xp_bf, *op = _layer_prompt(xp, xp_bf, lw, i, bsz, seq)
        xs, xs_bf, ssm_out, *os_ = _layer_sample(xs, xs_bf, ssm_all, i, ssm_out, state_ssd_conv[i],
                                                 state_pool[i], state_ffn_conv[i], lw)
        outs_p.append(op)
        outs_s.append(os_)
    stack = lambda outs, k: jnp.stack([o[k] for o in outs])
    return (xp.reshape(bsz, seq, D_MODEL), xs.reshape(n_s, 1, D_MODEL),
            stack(outs_p, 0), stack(outs_p, 1), stack(outs_p, 2), stack(outs_p, 3),
            ssm_out.reshape(state_ssm.shape), stack(outs_s, 0), stack(outs_s, 1), stack(outs_s, 2))
```

```python
import functools

import jax
import jax.numpy as jnp
from jax import lax
from jax.experimental import pallas as pl
from jax.experimental.pallas import tpu as pltpu

F32 = jnp.float32
BF16 = jnp.bfloat16

D_MODEL = 2048
HEAD_DIM = 64
D_INNER = 2 * D_MODEL
N_HEADS = D_INNER // HEAD_DIM
N_GROUPS = 8
HEADS_PER_GROUP = N_HEADS // N_GROUPS
GROUP_W = D_INNER // N_GROUPS
D_STATE = 128
CONV_K = 4
BC_W = 2 * N_GROUPS * D_STATE
CONV_DIM = D_INNER + BC_W
CHUNK = 128
D_POOL = D_MODEL
POOL_WINDOWS = (2, 4, 8, 16)
POOL_GC = D_POOL // len(POOL_WINDOWS)
POOL_BUF = max(POOL_WINDOWS) - 1
D_FF = 5504
FFN_K = 3
DEPTH = 2
PAST_LEN = 16384
ALPHA = (2 * DEPTH) ** 0.25
LN_EPS = 1e-5
RMS_EPS = 1e-5

LANES = 128
SUBLANES = 8
D_FF_PAD = 5632
FFN_TN = 512
DT_PAD = LANES
OFF_Z = 0
OFF_XS = D_INNER
OFF_BC = 2 * D_INNER
OFF_POOL = 2 * D_INNER + BC_W
OFF_GATE = OFF_POOL + D_POOL
N_MAIN = OFF_GATE + 2 * D_MODEL
VMEM_LIMIT = 56 * 1024 * 1024
NEG_BIG = -1e30


def _cparams(n_axes):
    return pltpu.CompilerParams(dimension_semantics=("arbitrary",) * n_axes,
                                vmem_limit_bytes=VMEM_LIMIT)


def _sigmoid(x):
    return 1.0 / (1.0 + jnp.exp(-x))


def _silu(x):
    return x * _sigmoid(x)


def _softplus(x):
    return jnp.maximum(x, 0.0) + jnp.log(1.0 + jnp.exp(-jnp.abs(x)))


def _layer_norm(r, g, b):
    mu = jnp.mean(r, axis=-1, keepdims=True)
    d = r - mu
    var = jnp.mean(d * d, axis=-1, keepdims=True)
    return d * lax.rsqrt(var + LN_EPS) * g + b


def _dt_proj_kernel(x_ref, w_ref, o_ref):
    o_ref[...] = lax.dot_general(x_ref[...], w_ref[...].astype(BF16), (((1,), (1,)), ((), ())),
                                 preferred_element_type=F32)


def _dt_proj(x_bf, w_in_t, layer, tm):
    m = x_bf.shape[0]
    return pl.pallas_call(
        _dt_proj_kernel,
        grid=(m // tm,),
        in_specs=[pl.BlockSpec((tm, D_MODEL), lambda i: (i, 0)),
                  pl.BlockSpec((None, DT_PAD, D_MODEL), lambda i: (layer, DT_COL_BLOCK, 0))],
        out_specs=pl.BlockSpec((tm, DT_PAD), lambda i: (i, 0)),
        out_shape=jax.ShapeDtypeStruct((m, DT_PAD), F32),
        compiler_params=_cparams(1),
        name="dt_proj",
    )(x_bf, w_in_t)


IN_TN = 1024
J_XBC = OFF_XS // IN_TN
J_POOL = OFF_POOL // IN_TN
J_GATE = OFF_GATE // IN_TN
J_END = N_MAIN // IN_TN
CONV_HALO = SUBLANES
PG_SHIFT = N_HEADS
DT_COL_BLOCK = (2 * D_INNER + BC_W) // DT_PAD


def _in_proj_kernel(x_ref, wa_ref, wb_ref, cw_ref, cb_ref, bg_ref, o_ref, tail_ref, w_s, h_s,
                    *, tiles_per_seq, conv):
    j = pl.program_id(0)
    i = pl.program_id(1)
    tm = x_ref.shape[0]

    @pl.when((i == 0) & (j < J_POOL))
    def _():
        w_s[...] = wa_ref[...].astype(BF16)

    @pl.when((i == 0) & (j >= J_POOL))
    def _():
        w_s[...] = jnp.concatenate([wa_ref[PG_SHIFT:, :], wb_ref[:PG_SHIFT, :]], axis=0).astype(BF16)

    def mm():
        return lax.dot_general(x_ref[...], w_s[...], (((1,), (1,)), ((), ())),
                               preferred_element_type=F32)

    is_xbc = (j >= J_XBC) & (j < J_POOL)

    @pl.when(jnp.logical_not(is_xbc))
    def _():
        tail_ref[0] = jnp.zeros((CONV_HALO, IN_TN), F32)

    @pl.when(j < J_XBC)
    def _():
        o_ref[...] = _silu(mm())

    @pl.when(is_xbc)
    def _():
        if conv:
            @pl.when(i % tiles_per_seq == 0)
            def _():
                h_s[0:CONV_HALO, :] = jnp.zeros((CONV_HALO, IN_TN), F32)

            h_s[CONV_HALO:CONV_HALO + tm, :] = mm()
            acc = cb_ref[...]
            for k in range(CONV_K):
                lo = CONV_HALO - (CONV_K - 1) + k
                acc = acc + h_s[lo:lo + tm, :] * cw_ref[k:k + 1, :]
            o_ref[...] = _silu(acc)
            tail = h_s[tm:tm + CONV_HALO, :]
            h_s[0:CONV_HALO, :] = tail
            tail_ref[0] = tail
        else:
            o_ref[...] = mm()
            tail_ref[0] = jnp.zeros((CONV_HALO, IN_TN), F32)

    @pl.when((j >= J_POOL) & (j < J_GATE))
    def _():
        o_ref[...] = mm()

    @pl.when(j >= J_GATE)
    def _():
        o_ref[...] = _sigmoid(mm() + bg_ref[...])


def _in_proj(x_bf, lw, layer, tm, tiles_per_seq, conv):
    m = x_bf.shape[0]
    ni = m // tm
    n_conv_tiles = CONV_DIM // IN_TN
    clamp = lambda v, lo, hi: jnp.minimum(jnp.maximum(v, lo), hi)
    conv_tile = lambda j: clamp(j - J_XBC, 0, n_conv_tiles - 1)
    kern = functools.partial(_in_proj_kernel, tiles_per_seq=tiles_per_seq, conv=conv)
    return pl.pallas_call(
        kern,
        grid=(J_END, ni),
        in_specs=[pl.BlockSpec((tm, D_MODEL), lambda j, i: (i, 0)),
                  pl.BlockSpec((None, IN_TN, D_MODEL), lambda j, i: (layer, j, 0)),
                  pl.BlockSpec((None, IN_TN, D_MODEL), lambda j, i: (layer, jnp.maximum(j + 1, J_POOL), 0),
                               pipeline_mode=pl.Buffered(1)),
                  pl.BlockSpec((CONV_K, IN_TN), lambda j, i: (0, conv_tile(j))),
                  pl.BlockSpec((1, IN_TN), lambda j, i: (0, conv_tile(j))),
                  pl.BlockSpec((1, IN_TN), lambda j, i: (0, clamp(j - J_GATE, 0, J_END - J_GATE - 1)))],
        out_specs=[pl.BlockSpec((tm, IN_TN), lambda j, i: (i, j)),
                   pl.BlockSpec((1, CONV_HALO, IN_TN), lambda j, i: (i, 0, j))],
        out_shape=[jax.ShapeDtypeStruct((m, N_MAIN), F32),
                   jax.ShapeDtypeStruct((ni, CONV_HALO, N_MAIN), F32)],
        scratch_shapes=[pltpu.VMEM((IN_TN, D_MODEL), BF16),
                        pltpu.VMEM((CONV_HALO + tm, IN_TN), F32)],
        compiler_params=_cparams(2),
        name="in_proj",
    )(x_bf, lw["w_in_t"], lw["w_in_t"], lw["conv_w"], lw["conv_b"], lw["b_gate"])


def _split3(v):
    hi = v.astype(BF16)
    r1 = v - hi.astype(F32)
    mid = r1.astype(BF16)
    lo = (r1 - mid.astype(F32)).astype(BF16)
    return hi, mid, lo


def _ssd_prompt_kernel(xs_ref, bc_ref, zs_ref, dt_ref, dtb_ref, alog_ref, dsk_ref, nw_ref,
                       y_ref, st_ref,
                       xs_s, bt_s, c_s, y_s, state_s, acol_s, arow_s, dtrow_s):
    c = pl.program_id(1)
    n_chunks = pl.num_programs(1)
    q = CHUNK

    @pl.when(c == 0)
    def _():
        state_s[...] = jnp.zeros(state_s.shape, F32)

    for g in range(N_GROUPS):
        xs_s[g] = xs_ref[:, g * GROUP_W:(g + 1) * GROUP_W]
        bt_s[g] = bc_ref[:, g * D_STATE:(g + 1) * D_STATE].T
        c_s[g] = bc_ref[:, (N_GROUPS + g) * D_STATE:(N_GROUPS + g + 1) * D_STATE].astype(BF16)

    dt = _softplus(dt_ref[...] + dtb_ref[...])
    a_neg = -jnp.exp(alog_ref[...])
    d_a = dt * a_neg
    row = lax.broadcasted_iota(jnp.int32, (q, q), 0)
    col = lax.broadcasted_iota(jnp.int32, (q, q), 1)
    causal = row >= col
    tril = jnp.where(causal, 1.0, 0.0).astype(BF16)
    hi, mid, lo3 = _split3(d_a)
    a_cum = (jnp.dot(tril, hi, preferred_element_type=F32)
             + jnp.dot(tril, mid, preferred_element_type=F32)
             + jnp.dot(tril, lo3, preferred_element_type=F32))
    arow_s[...] = a_cum.T
    dtrow_s[...] = dt.T
    for g in range(N_GROUPS):
        sh = (LANES - HEADS_PER_GROUP * g) % LANES
        acol_s[g] = a_cum if sh == 0 else pltpu.roll(a_cum, sh, 1)

    lane = lax.broadcasted_iota(jnp.int32, (q, LANES), 1)
    lo_half = lane < HEAD_DIM

    def group_body(g, carry):
        acol = acol_s[g]
        c_g = c_s[g]
        bt_g = bt_s[g]
        cb = jnp.dot(c_g, bt_g.astype(BF16), preferred_element_type=F32)
        y_off_g = jnp.dot(c_g, state_s[g].astype(BF16), preferred_element_type=F32)
        for k in range(HEADS_PER_GROUP // 2):
            l_parts, b_parts, a_b = [], [], []
            for e in range(2):
                hh = 2 * k + e
                head = g * HEADS_PER_GROUP + hh
                a_col = jnp.broadcast_to(acol[:, hh:hh + 1], (q, q))
                a_row = arow_s[pl.ds(head, 1), :]
                dt_row = dtrow_s[pl.ds(head, 1), :]
                seg = jnp.where(causal, a_col - a_row, NEG_BIG)
                l_parts.append((cb * jnp.exp(seg) * dt_row).astype(BF16))
                w_row = dt_row * jnp.exp(a_col[q - 1:q, :] - a_row)
                b_parts.append((bt_g * w_row).astype(BF16))
                a_b.append(a_col)
            lhs = jnp.concatenate([jnp.concatenate(l_parts, axis=1),
                                   jnp.concatenate(b_parts, axis=1)], axis=0)
            cols = slice(k * LANES, (k + 1) * LANES)
            xs_bf = xs_s[g, :, cols].astype(BF16)
            zero = jnp.zeros_like(xs_bf)
            rhs = jnp.concatenate([jnp.where(lo_half, xs_bf, zero),
                                   jnp.where(lo_half, zero, xs_bf)], axis=0)
            res = jnp.dot(lhs, rhs, preferred_element_type=F32)
            a_pair = jnp.where(lo_half, a_b[0], a_b[1])
            y_s[g, :, cols] = res[0:q] + y_off_g[:, cols] * jnp.exp(a_pair)
            cdec = jnp.exp(a_pair[q - 1:q, :])
            state_s[g, :, cols] = state_s[g, :, cols] * cdec + res[q:2 * q]
        return carry

    lax.fori_loop(0, N_GROUPS, group_body, 0)

    for g in range(N_GROUPS):
        cols = slice(g * GROUP_W, (g + 1) * GROUP_W)
        yv = y_s[g] + dsk_ref[:, cols] * xs_s[g]
        v = yv * zs_ref[:, cols]
        ms = jnp.mean(v * v, axis=-1, keepdims=True)
        y_ref[:, cols] = (v * lax.rsqrt(ms + RMS_EPS) * nw_ref[:, cols]).astype(y_ref.dtype)

    @pl.when(c == n_chunks - 1)
    def _():
        for g in range(N_GROUPS):
            st_ref[0, g] = state_s[g].T


def _ssd_prompt(proj, dt_raw, lw, bsz, seq):
    n_chunks = seq // CHUNK
    rows = lambda b, c: b * n_chunks + c
    small = lambda shape: pl.BlockSpec(shape, lambda b, c: (0, 0))
    y, st = pl.pallas_call(
        _ssd_prompt_kernel,
        grid=(bsz, n_chunks),
        in_specs=[
            pl.BlockSpec((CHUNK, D_INNER), lambda b, c: (rows(b, c), OFF_XS // D_INNER)),
            pl.BlockSpec((CHUNK, BC_W), lambda b, c: (rows(b, c), OFF_BC // BC_W)),
            pl.BlockSpec((CHUNK, D_INNER), lambda b, c: (rows(b, c), OFF_Z // D_INNER)),
            pl.BlockSpec((CHUNK, DT_PAD), lambda b, c: (rows(b, c), 0)),
            small((1, DT_PAD)), small((1, DT_PAD)), small((1, D_INNER)), small((1, D_INNER)),
        ],
        out_specs=[
            pl.BlockSpec((CHUNK, D_INNER), lambda b, c: (rows(b, c), 0)),
            pl.BlockSpec((1, N_GROUPS, GROUP_W, D_STATE), lambda b, c: (b, 0, 0, 0)),
        ],
        out_shape=[jax.ShapeDtypeStruct((bsz * seq, D_INNER), BF16),
                   jax.ShapeDtypeStruct((bsz, N_GROUPS, GROUP_W, D_STATE), F32)],
        scratch_shapes=[
            pltpu.VMEM((N_GROUPS, CHUNK, GROUP_W), F32),
            pltpu.VMEM((N_GROUPS, D_STATE, CHUNK), F32),
            pltpu.VMEM((N_GROUPS, CHUNK, D_STATE), BF16),
            pltpu.VMEM((N_GROUPS, CHUNK, GROUP_W), F32),
            pltpu.VMEM((N_GROUPS, D_STATE, GROUP_W), F32),
            pltpu.VMEM((N_GROUPS, CHUNK, LANES), F32),
            pltpu.VMEM((LANES, CHUNK), F32),
            pltpu.VMEM((LANES, CHUNK), F32),
        ],
        compiler_params=_cparams(2),
        name="ssd_prompt",
    )(proj, proj, proj, dt_raw, lw["dt_bias"], lw["a_log"], lw["d_skip_x"], lw["norm_w"])
    return y, st.reshape(bsz, N_HEADS, HEAD_DIM, D_STATE)


def _ssd_sample_prep_kernel(xs_ref, bc_ref, dt_ref, cst_ref, cw_ref, cb_ref, dtb_ref, alog_ref,
                            ex_ref, xs_o, b_o, c_o, decht_o, xdtt_o):
    def conv(u, lo, hi):
        acc = cb_ref[:, lo:hi]
        for j in range(CONV_K - 1):
            acc = acc + cst_ref[j][:, lo:hi] * cw_ref[j:j + 1, lo:hi]
        acc = acc + u * cw_ref[CONV_K - 1:CONV_K, lo:hi]
        return _silu(acc)

    xs = conv(xs_ref[...], 0, D_INNER)
    bc = conv(bc_ref[...], D_INNER, CONV_DIM)
    xs_o[...] = xs
    b_o[...] = bc[:, 0:N_GROUPS * D_STATE]
    c_o[...] = bc[:, N_GROUPS * D_STATE:BC_W]
    dt = _softplus(dt_ref[...] + dtb_ref[...])
    d_a = dt * (-jnp.exp(alog_ref[...]))
    ex = ex_ref[...]

    def expand(v):
        hi, mid, lo3 = _split3(v)
        return (jnp.dot(hi, ex, preferred_element_type=F32)
                + jnp.dot(mid, ex, preferred_element_type=F32)
                + jnp.dot(lo3, ex, preferred_element_type=F32))

    xdt = expand(dt) * xs
    decht_o[...] = jnp.exp(d_a).T
    xdtt_o[...] = xdt.T


def _ssd_sample_prep(proj, dt_raw, conv_state_t, lw):
    s = proj.shape[0]
    full = lambda shape: pl.BlockSpec(shape, lambda i: (0,) * len(shape))
    return pl.pallas_call(
        _ssd_sample_prep_kernel,
        grid=(1,),
        in_specs=[
            pl.BlockSpec((s, D_INNER), lambda i: (0, OFF_XS // D_INNER)),
            pl.BlockSpec((s, BC_W), lambda i: (0, OFF_BC // BC_W)),
            full((s, DT_PAD)), full((CONV_K - 1, s, CONV_DIM)),
            full((CONV_K, CONV_DIM)), full((1, CONV_DIM)), full((1, DT_PAD)), full((1, DT_PAD)),
            full((DT_PAD, D_INNER)),
        ],
        out_specs=[full((s, D_INNER)), full((s, N_GROUPS * D_STATE)), full((s, N_GROUPS * D_STATE)),
                   full((DT_PAD, s)), full((D_INNER, s))],
        out_shape=[jax.ShapeDtypeStruct((s, D_INNER), F32),
                   jax.ShapeDtypeStruct((s, N_GROUPS * D_STATE), F32),
                   jax.ShapeDtypeStruct((s, N_GROUPS * D_STATE), F32),
                   jax.ShapeDtypeStruct((DT_PAD, s), F32),
                   jax.ShapeDtypeStruct((D_INNER, s), F32)],
        compiler_params=_cparams(1),
        name="ssd_sample_prep",
    )(proj, proj, dt_raw, conv_state_t, lw["conv_w"], lw["conv_b"], lw["dt_bias"], lw["a_log"],
      lw["head_expand"])


def _ssd_sample_step_kernel(st_ref, decht_ref, xdtt_ref, b_ref, c_ref, *rest, fill_other_layers):
    st_o, yt_o = rest[-2:]
    s = pl.program_id(0)
    n_s = xdtt_ref.shape[1]
    onehot = lax.broadcasted_iota(jnp.int32, (GROUP_W, n_s), 1) == s
    onehot_h = lax.broadcasted_iota(jnp.int32, (DT_PAD, n_s), 1) == s

    @pl.when(s == 0)
    def _():
        yt_o[...] = jnp.zeros(yt_o.shape, F32)

    if fill_other_layers:
        st_o[1:] = jnp.zeros((st_o.shape[0] - 1,) + st_o.shape[1:], F32)

    dech = jnp.sum(jnp.where(onehot_h, decht_ref[...], 0.0), axis=-1, keepdims=True)
    b_all = b_ref[pl.ds(s, 1), :]
    c_all = c_ref[pl.ds(s, 1), :]
    for g in range(N_GROUPS):
        rows = slice(g * GROUP_W, (g + 1) * GROUP_W)
        cols = slice(g * D_STATE, (g + 1) * D_STATE)
        xdt = jnp.sum(jnp.where(onehot, xdtt_ref[rows, :], 0.0), axis=-1, keepdims=True)
        b_row = b_all[:, cols]
        c_row = c_all[:, cols]
        upd = xdt * b_row
        parts = []
        for hh in range(HEADS_PER_GROUP):
            h = g * HEADS_PER_GROUP + hh
            r_h = slice(h * HEAD_DIM, (h + 1) * HEAD_DIM)
            parts.append(st_ref[0, 0, r_h, :] * dech[h:h + 1, :]
                         + upd[hh * HEAD_DIM:(hh + 1) * HEAD_DIM, :])
        st_new = jnp.concatenate(parts, axis=0)
        st_o[0, 0, rows, :] = st_new
        y_col = jnp.sum(st_new * c_row, axis=-1, keepdims=True)
        yt_o[rows, :] = yt_o[rows, :] + jnp.where(onehot, y_col, 0.0)


def _ssd_sample_step(state_all, layer, prev_out, decht, xdtt, b_m, c_m):
    depth = state_all.shape[0]
    s = xdtt.shape[1]
    full = lambda shape: pl.BlockSpec(shape, lambda i: (0,) * len(shape))
    in_specs = [pl.BlockSpec((1, 1, D_INNER, D_STATE), lambda i: (layer, i, 0, 0)),
                full((DT_PAD, s)), full((D_INNER, s)),
                full((s, N_GROUPS * D_STATE)), full((s, N_GROUPS * D_STATE))]
    args = [state_all, decht, xdtt, b_m, c_m]
    if prev_out is None:
        assert layer == 0
        aliases = {}
        st_out = pl.BlockSpec((depth, 1, D_INNER, D_STATE), lambda i: (0, i, 0, 0))
    else:
        in_specs.append(pl.BlockSpec(memory_space=pl.ANY))
        args.append(prev_out)
        aliases = {len(args) - 1: 0}
        st_out = pl.BlockSpec((1, 1, D_INNER, D_STATE), lambda i: (layer, i, 0, 0))
    return pl.pallas_call(
        functools.partial(_ssd_sample_step_kernel, fill_other_layers=prev_out is None),
        grid=(s,),
        in_specs=in_specs,
        out_specs=[st_out, full((D_INNER, s))],
        out_shape=[jax.ShapeDtypeStruct(state_all.shape, F32),
                   jax.ShapeDtypeStruct((D_INNER, s), F32)],
        input_output_aliases=aliases,
        compiler_params=_cparams(1),
        name="ssd_sample_step",
    )(*args)


def _ssd_sample_finish_kernel(yt_ref, xs_ref, zs_ref, dsk_ref, nw_ref, y_ref):
    y = yt_ref[...].T
    for g in range(N_GROUPS):
        cols = slice(g * GROUP_W, (g + 1) * GROUP_W)
        yv = y[:, cols] + dsk_ref[:, cols] * xs_ref[:, cols]
        v = yv * zs_ref[:, cols]
        ms = jnp.mean(v * v, axis=-1, keepdims=True)
        y_ref[:, cols] = (v * lax.rsqrt(ms + RMS_EPS) * nw_ref[:, cols]).astype(y_ref.dtype)


def _ssd_sample_finish(yt, xs, proj, lw):
    s = xs.shape[0]
    full = lambda shape: pl.BlockSpec(shape, lambda i: (0,) * len(shape))
    return pl.pallas_call(
        _ssd_sample_finish_kernel,
        grid=(1,),
        in_specs=[full((D_INNER, s)), full((s, D_INNER)),
                  pl.BlockSpec((s, D_INNER), lambda i: (0, OFF_Z // D_INNER)),
                  full((1, D_INNER)), full((1, D_INNER))],
        out_specs=full((s, D_INNER)),
        out_shape=jax.ShapeDtypeStruct((s, D_INNER), BF16),
        compiler_params=_cparams(1),
        name="ssd_sample_finish",
    )(yt, xs, proj, lw["d_skip_x"], lw["norm_w"])


POOL_TP = 512
POOL_HALO = 16


def _pool_prompt_kernel(u_ref, w_ref, sc_ref, o_ref, ext_s, w_s, *, tiles_per_seq):
    g = pl.program_id(0)
    i = pl.program_id(1)
    tp = POOL_TP

    @pl.when(i == 0)
    def _():
        w_s[...] = w_ref[...].astype(BF16)

    @pl.when(i % tiles_per_seq == 0)
    def _():
        ext_s[0:POOL_HALO, :] = jnp.zeros((POOL_HALO, POOL_GC), F32)

    ext_s[POOL_HALO:POOL_HALO + tp, :] = u_ref[...]
    pos = (i % tiles_per_seq) * tp + lax.broadcasted_iota(jnp.int32, (tp, 1), 0)
    for gi, win in enumerate(POOL_WINDOWS):
        @pl.when(g == gi)
        def _():
            u = ext_s[POOL_HALO:POOL_HALO + tp, :]
            tot = u
            for k in range(1, win):
                tot = tot + ext_s[POOL_HALO - k:POOL_HALO - k + tp, :]
            cnt = jnp.minimum(win, pos + 1).astype(F32)
            pooled = (tot / cnt - u).astype(BF16)
            o_ref[...] = jnp.dot(pooled, w_s[...], preferred_element_type=F32) * sc_ref[...]
    ext_s[0:POOL_HALO, :] = ext_s[tp:tp + POOL_HALO, :]


def _pool_prompt(proj, w_pool, layer, scale, seq):
    m = proj.shape[0]
    ng = len(POOL_WINDOWS)
    kern = functools.partial(_pool_prompt_kernel, tiles_per_seq=seq // POOL_TP)
    return pl.pallas_call(
        kern,
        grid=(ng, m // POOL_TP),
        in_specs=[pl.BlockSpec((POOL_TP, POOL_GC), lambda g, i: (i, OFF_POOL // POOL_GC + g)),
                  pl.BlockSpec((None, None, POOL_GC, POOL_GC), lambda g, i: (layer, g, 0, 0)),
                  pl.BlockSpec((1, POOL_GC), lambda g, i: (0, g))],
        out_specs=pl.BlockSpec((POOL_TP, POOL_GC), lambda g, i: (i, g)),
        out_shape=jax.ShapeDtypeStruct((m, D_POOL), F32),
        scratch_shapes=[pltpu.VMEM((POOL_HALO + POOL_TP, POOL_GC), F32),
                        pltpu.VMEM((POOL_GC, POOL_GC), BF16)],
        compiler_params=_cparams(2),
        name="pool_prompt",
    )(proj, w_pool, scale)


def _pool_sample_kernel(u_ref, buf_ref, o_ref):
    g = pl.program_id(0)
    for gi, win in enumerate(POOL_WINDOWS):
        @pl.when(g == gi)
        def _():
            u = u_ref[...]
            tot = u
            for k in range(1, win):
                tot = tot + buf_ref[:, POOL_BUF - k, :]
            cnt = float(min(win, PAST_LEN + 1))
            o_ref[...] = (tot / cnt - u).astype(o_ref.dtype)


def _pool_sample(proj, pool_state, layer):
    s = proj.shape[0]
    return pl.pallas_call(
        _pool_sample_kernel,
        grid=(len(POOL_WINDOWS),),
        in_specs=[pl.BlockSpec((s, POOL_GC), lambda g: (0, OFF_POOL // POOL_GC + g)),
                  pl.BlockSpec((None, s, POOL_BUF, POOL_GC), lambda g: (layer, 0, 0, g))],
        out_specs=pl.BlockSpec((s, POOL_GC), lambda g: (0, g)),
        out_shape=jax.ShapeDtypeStruct((s, D_POOL), BF16),
        compiler_params=_cparams(1),
        name="pool_sample",
    )(proj, pool_state)


def _pool_mm_kernel(p_ref, w_ref, sc_ref, o_ref, w_s):
    @pl.when(pl.program_id(1) == 0)
    def _():
        w_s[...] = w_ref[...].astype(BF16)

    acc = jnp.dot(p_ref[...], w_s[...], preferred_element_type=F32)
    o_ref[...] = acc * sc_ref[...]


def _pool_mm(pooled, w_pool, layer, scale, tm):
    m = pooled.shape[0]
    ng = len(POOL_WINDOWS)
    return pl.pallas_call(
        _pool_mm_kernel,
        grid=(ng, m // tm),
        in_specs=[pl.BlockSpec((tm, POOL_GC), lambda g, i: (i, g)),
                  pl.BlockSpec((None, None, POOL_GC, POOL_GC), lambda g, i: (layer, g, 0, 0)),
                  pl.BlockSpec((1, POOL_GC), lambda g, i: (0, g))],
        out_specs=pl.BlockSpec((tm, POOL_GC), lambda g, i: (i, g)),
        out_shape=jax.ShapeDtypeStruct((m, D_POOL), F32),
        scratch_shapes=[pltpu.VMEM((POOL_GC, POOL_GC), BF16)],
        compiler_params=_cparams(2),
        name="pool_mm",
    )(pooled, w_pool, scale)


def _branch_merge_kernel(y_ref, w_ref, ga_ref, gb_ref, yb_ref, o_ref, w_s):
    @pl.when(pl.program_id(1) == 0)
    def _():
        w_s[...] = w_ref[...].astype(BF16)

    y_a = jnp.dot(y_ref[...], w_s[...], preferred_element_type=F32)
    o_ref[...] = (ga_ref[...] * y_a + gb_ref[...] * yb_ref[...]).astype(o_ref.dtype)


def _branch_merge(y, w_br, layer, proj, y_b, tm, tn):
    m = y.shape[0]
    nj = D_MODEL // tn
    ga0 = OFF_GATE // tn
    return pl.pallas_call(
        _branch_merge_kernel,
        grid=(nj, m // tm),
        in_specs=[pl.BlockSpec((tm, D_INNER), lambda j, i: (i, 0)),
                  pl.BlockSpec((None, D_INNER, tn), lambda j, i: (layer, 0, j)),
                  pl.BlockSpec((tm, tn), lambda j, i: (i, ga0 + j)),
                  pl.BlockSpec((tm, tn), lambda j, i: (i, ga0 + nj + j)),
                  pl.BlockSpec((tm, tn), lambda j, i: (i, j))],
        out_specs=pl.BlockSpec((tm, tn), lambda j, i: (i, j)),
        out_shape=jax.ShapeDtypeStruct((m, D_MODEL), BF16),
        scratch_shapes=[pltpu.VMEM((D_INNER, tn), BF16)],
        compiler_params=_cparams(2),
        name="branch_merge",
    )(y, w_br, proj, proj, y_b)


def _proj_ln_kernel(m_ref, w_ref, x_ref, g_ref, b_ref, o_ref, obf_ref, w_s):
    @pl.when(pl.program_id(0) == 0)
    def _():
        w_s[...] = w_ref[...].astype(BF16)

    acc = jnp.dot(m_ref[...], w_s[...], preferred_element_type=F32)
    y = _layer_norm(ALPHA * x_ref[...] + acc, g_ref[...], b_ref[...])
    o_ref[...] = y
    obf_ref[...] = y.astype(BF16)


def _proj_ln(mix, w, layer, x, g, b, tm):
    m, k = mix.shape
    return pl.pallas_call(
        _proj_ln_kernel,
        grid=(m // tm,),
        in_specs=[pl.BlockSpec((tm, k), lambda i: (i, 0)),
                  pl.BlockSpec((None, k, D_MODEL), lambda i: (layer, 0, 0),
                               pipeline_mode=pl.Buffered(1)),
                  pl.BlockSpec((tm, D_MODEL), lambda i: (i, 0)),
                  pl.BlockSpec((1, D_MODEL), lambda i: (0, 0)),
                  pl.BlockSpec((1, D_MODEL), lambda i: (0, 0))],
        out_specs=[pl.BlockSpec((tm, D_MODEL), lambda i: (i, 0)),
                   pl.BlockSpec((tm, D_MODEL), lambda i: (i, 0))],
        out_shape=[jax.ShapeDtypeStruct((m, D_MODEL), F32),
                   jax.ShapeDtypeStruct((m, D_MODEL), BF16)],
        scratch_shapes=[pltpu.VMEM((k, D_MODEL), BF16)],
        compiler_params=_cparams(1),
        name="proj_ln",
    )(mix, w, x, g, b)


FFN_HALO = SUBLANES
FFN_VAL_BLK = D_FF // FFN_TN
FFN_VAL_SHIFT = D_FF % FFN_TN
assert FFN_VAL_SHIFT % LANES == 0


def _ffn_weight_tiles(j, wg_ref, wva_ref, wvb_ref, wg_s, wv_s):
    col = j * FFN_TN + lax.broadcasted_iota(jnp.int32, (1, FFN_TN), 1)
    valid = col < D_FF
    wg_s[...] = jnp.where(valid, wg_ref[...], 0.0).astype(BF16)
    wv = jnp.concatenate([wva_ref[:, FFN_VAL_SHIFT:], wvb_ref[:, :FFN_VAL_SHIFT]], axis=1)
    wv_s[...] = jnp.where(valid, wv, 0.0).astype(BF16)


def _ffn_w_specs(layer, index_of):
    blk = (None, D_MODEL, FFN_TN)
    return [pl.BlockSpec(blk, index_of(lambda j: (layer, 0, j))),
            pl.BlockSpec(blk, index_of(lambda j: (layer, 0, FFN_VAL_BLK + j))),
            pl.BlockSpec(blk, index_of(lambda j: (layer, 0, FFN_VAL_BLK + j + 1)))]


def _ffn_conv_gate(hg_s, hv_s, cw_g, cw_v, cb_g, cb_v, tm):
    def conv(h_s, cw, cb):
        acc = cb[...]
        for j in range(FFN_K):
            lo = FFN_HALO - (FFN_K - 1) + j
            acc = acc + h_s[lo:lo + tm, :] * cw[j:j + 1, :]
        return acc
    return _silu(conv(hg_s, cw_g, cb_g)) * conv(hv_s, cw_v, cb_v)


def _ffn_up_prompt_kernel(x_ref, wg_ref, wva_ref, wvb_ref, cwg_ref, cwv_ref, cbg_ref, cbv_ref,
                          a_ref, tg_ref, tv_ref, wg_s, wv_s, hg_s, hv_s, *, tiles_per_seq):
    j = pl.program_id(0)
    i = pl.program_id(1)
    tm = x_ref.shape[0]

    @pl.when(i == 0)
    def _():
        _ffn_weight_tiles(j, wg_ref, wva_ref, wvb_ref, wg_s, wv_s)

    @pl.when(i % tiles_per_seq == 0)
    def _():
        hg_s[0:FFN_HALO, :] = jnp.zeros((FFN_HALO, FFN_TN), F32)
        hv_s[0:FFN_HALO, :] = jnp.zeros((FFN_HALO, FFN_TN), F32)

    x = x_ref[...]
    hg_s[FFN_HALO:FFN_HALO + tm, :] = jnp.dot(x, wg_s[...], preferred_element_type=F32)
    hv_s[FFN_HALO:FFN_HALO + tm, :] = jnp.dot(x, wv_s[...], preferred_element_type=F32)
    a_ref[...] = _ffn_conv_gate(hg_s, hv_s, cwg_ref, cwv_ref, cbg_ref, cbv_ref, tm).astype(a_ref.dtype)
    tail_g = hg_s[tm:tm + FFN_HALO, :]
    tail_v = hv_s[tm:tm + FFN_HALO, :]
    hg_s[0:FFN_HALO, :] = tail_g
    hv_s[0:FFN_HALO, :] = tail_v
    tg_ref[0] = tail_g
    tv_ref[0] = tail_v


def _ffn_up_prompt(x_bf, lw, layer, seq, tm):
    m = x_bf.shape[0]
    nj = D_FF_PAD // FFN_TN
    ni = m // tm
    kern = functools.partial(_ffn_up_prompt_kernel, tiles_per_seq=seq // tm)
    cspec_g = lambda r: pl.BlockSpec((r, FFN_TN), lambda j, i: (0, j))
    cspec_v = lambda r: pl.BlockSpec((r, FFN_TN), lambda j, i: (0, nj + j))
    w_specs = _ffn_w_specs(layer, lambda f: (lambda j, i: f(j)))
    return pl.pallas_call(
        kern,
        grid=(nj, ni),
        in_specs=[pl.BlockSpec((tm, D_MODEL), lambda j, i: (i, 0)), *w_specs,
                  cspec_g(FFN_K), cspec_v(FFN_K), cspec_g(1), cspec_v(1)],
        out_specs=[pl.BlockSpec((tm, FFN_TN), lambda j, i: (i, j)),
                   pl.BlockSpec((1, FFN_HALO, FFN_TN), lambda j, i: (i, 0, j)),
                   pl.BlockSpec((1, FFN_HALO, FFN_TN), lambda j, i: (i, 0, j))],
        out_shape=[jax.ShapeDtypeStruct((m, D_FF_PAD), BF16),
                   jax.ShapeDtypeStruct((ni, FFN_HALO, D_FF_PAD), F32),
                   jax.ShapeDtypeStruct((ni, FFN_HALO, D_FF_PAD), F32)],
        scratch_shapes=[pltpu.VMEM((D_MODEL, FFN_TN), BF16),
                        pltpu.VMEM((D_MODEL, FFN_TN), BF16),
                        pltpu.VMEM((FFN_HALO + tm, FFN_TN), F32),
                        pltpu.VMEM((FFN_HALO + tm, FFN_TN), F32)],
        compiler_params=_cparams(2),
        name="ffn_up_prompt",
    )(x_bf, lw["w_up"], lw["w_up"], lw["w_up"],
      lw["fconv_w"], lw["fconv_w"], lw["fconv_b"], lw["fconv_b"])


def _ffn_up_sample_kernel(x_ref, wg_ref, wva_ref, wvb_ref, sg_ref, sva_ref, svb_ref, cwg_ref, cwv_ref,
                          cbg_ref, cbv_ref, a_ref, hg_ref, hv_ref, wg_s, wv_s):
    j = pl.program_id(0)
    _ffn_weight_tiles(j, wg_ref, wva_ref, wvb_ref, wg_s, wv_s)
    x = x_ref[...]
    hg = jnp.dot(x, wg_s[...], preferred_element_type=F32)
    hv = jnp.dot(x, wv_s[...], preferred_element_type=F32)
    hg_ref[...] = hg
    hv_ref[...] = hv
    valid = j * FFN_TN + lax.broadcasted_iota(jnp.int32, (1, FFN_TN), 1) < D_FF

    def conv(h, rows, cw, cb):
        acc = cb[...]
        for k in range(FFN_K - 1):
            acc = acc + jnp.where(valid, rows(k), 0.0) * cw[k:k + 1, :]
        return acc + h * cw[FFN_K - 1:FFN_K, :]

    gate_rows = lambda k: sg_ref[:, k, :]
    value_rows = lambda k: jnp.concatenate([sva_ref[:, k, FFN_VAL_SHIFT:],
                                            svb_ref[:, k, :FFN_VAL_SHIFT]], axis=1)
    a_ref[...] = (_silu(conv(hg, gate_rows, cwg_ref, cbg_ref))
                  * conv(hv, value_rows, cwv_ref, cbv_ref)).astype(a_ref.dtype)


def _ffn_up_sample(x_bf, ffn_state, lw, layer):
    s = x_bf.shape[0]
    nj = D_FF_PAD // FFN_TN
    g_blk = lambda r: pl.BlockSpec((r, FFN_TN), lambda j: (0, j))
    v_blk = lambda r: pl.BlockSpec((r, FFN_TN), lambda j: (0, nj + j))
    w_specs = _ffn_w_specs(layer, lambda f: f)
    st_blk = (None, s, FFN_K - 1, FFN_TN)
    return pl.pallas_call(
        _ffn_up_sample_kernel,
        grid=(nj,),
        in_specs=[pl.BlockSpec((s, D_MODEL), lambda j: (0, 0)), *w_specs,
                  pl.BlockSpec(st_blk, lambda j: (layer, 0, 0, j)),
                  pl.BlockSpec(st_blk, lambda j: (layer, 0, 0, FFN_VAL_BLK + j)),
                  pl.BlockSpec(st_blk, lambda j: (layer, 0, 0, FFN_VAL_BLK + j + 1)),
                  g_blk(FFN_K), v_blk(FFN_K), g_blk(1), v_blk(1)],
        out_specs=[pl.BlockSpec((s, FFN_TN), lambda j: (0, j)),
                   pl.BlockSpec((s, FFN_TN), lambda j: (0, j)),
                   pl.BlockSpec((s, FFN_TN), lambda j: (0, j))],
        out_shape=[jax.ShapeDtypeStruct((s, D_FF_PAD), BF16),
                   jax.ShapeDtypeStruct((s, D_FF_PAD), F32),
                   jax.ShapeDtypeStruct((s, D_FF_PAD), F32)],
        scratch_shapes=[pltpu.VMEM((D_MODEL, FFN_TN), BF16),
                        pltpu.VMEM((D_MODEL, FFN_TN), BF16)],
        compiler_params=_cparams(1),
        name="ffn_up_sample",
    )(x_bf, lw["w_up"], lw["w_up"], lw["w_up"], ffn_state, ffn_state, ffn_state,
      lw["fconv_w"], lw["fconv_w"], lw["fconv_b"], lw["fconv_b"])


DOWN_TK = 512


def _down_ln_kernel(a_ref, w_ref, x_ref, g_ref, b_ref, o_ref, obf_ref):
    k = pl.program_id(1)

    @pl.when(k == 0)
    def _():
        o_ref[...] = ALPHA * x_ref[...]

    row = k * DOWN_TK + lax.broadcasted_iota(jnp.int32, (DOWN_TK, 1), 0)
    w = jnp.where(row < D_FF, w_ref[...], 0.0).astype(BF16)
    o_ref[...] += jnp.dot(a_ref[...], w, preferred_element_type=F32)

    @pl.when(k == pl.num_programs(1) - 1)
    def _():
        y = _layer_norm(o_ref[...], g_ref[...], b_ref[...])
        o_ref[...] = y
        obf_ref[...] = y.astype(BF16)


def _down_ln(a, w, layer, x, g, b, tm):
    m = a.shape[0]
    nk = D_FF_PAD // DOWN_TK
    return pl.pallas_call(
        _down_ln_kernel,
        grid=(m // tm, nk),
        in_specs=[pl.BlockSpec((tm, DOWN_TK), lambda i, k: (i, k)),
                  pl.BlockSpec((None, DOWN_TK, D_MODEL), lambda i, k: (layer, k, 0)),
                  pl.BlockSpec((tm, D_MODEL), lambda i, k: (i, 0), pipeline_mode=pl.Buffered(1)),
                  pl.BlockSpec((1, D_MODEL), lambda i, k: (0, 0)),
                  pl.BlockSpec((1, D_MODEL), lambda i, k: (0, 0))],
        out_specs=[pl.BlockSpec((tm, D_MODEL), lambda i, k: (i, 0)),
                   pl.BlockSpec((tm, D_MODEL), lambda i, k: (i, 0))],
        out_shape=[jax.ShapeDtypeStruct((m, D_MODEL), F32),
                   jax.ShapeDtypeStruct((m, D_MODEL), BF16)],
        compiler_params=_cparams(2),
        name="down_ln",
    )(a, w, x, g, b)


STATE_NS = 8


def _state_out_kernel(sc_ref, sp_ref, sf_ref, *rest):
    oc_ref, op_ref, of_ref = rest[-3:]
    new_rows = rest[:-3]
    layer = pl.program_id(0)
    oc_ref[0, :, 0:CONV_K - 2, :] = sc_ref[0, :, 1:CONV_K - 1, :]
    op_ref[0, :, 0:POOL_BUF - 1, :] = sp_ref[0, :, 1:POOL_BUF, :]
    of_ref[0, :, 0:FFN_K - 2, :] = sf_ref[0, :, 1:FFN_K - 1, :]
    for li in range(len(new_rows) // 5):
        xs, bc, po, hg, hv = new_rows[5 * li:5 * li + 5]

        @pl.when(layer == li)
        def _():
            oc_ref[0, :, CONV_K - 2, 0:D_INNER] = xs[...]
            oc_ref[0, :, CONV_K - 2, D_INNER:CONV_DIM] = bc[...]
            op_ref[0, :, POOL_BUF - 1, :] = po[...]
            of_ref[0, :, FFN_K - 2, 0:D_FF] = hg[:, 0:D_FF]
            of_ref[0, :, FFN_K - 2, D_FF:2 * D_FF] = hv[:, 0:D_FF]


def _state_out(s_conv, s_pool, s_ffn, per_layer):
    depth, s = s_conv.shape[:2]
    ns = STATE_NS
    blk = lambda a: pl.BlockSpec((1, ns) + a.shape[2:], lambda l, i: (l, i, 0, 0))
    rows = lambda w, cb: pl.BlockSpec((ns, w), lambda l, i: (i, cb))
    in_specs = [blk(s_conv), blk(s_pool), blk(s_ffn)]
    args = [s_conv, s_pool, s_ffn]
    for proj, h_g, h_v in per_layer:
        in_specs += [rows(D_INNER, OFF_XS // D_INNER), rows(BC_W, OFF_BC // BC_W),
                     rows(D_POOL, OFF_POOL // D_POOL), rows(D_FF_PAD, 0), rows(D_FF_PAD, 0)]
        args += [proj, proj, proj, h_g, h_v]
    return pl.pallas_call(
        _state_out_kernel,
        grid=(depth, s // ns),
        in_specs=in_specs,
        out_specs=[blk(s_conv), blk(s_pool), blk(s_ffn)],
        out_shape=[jax.ShapeDtypeStruct(a.shape, F32) for a in (s_conv, s_pool, s_ffn)],
        compiler_params=_cparams(2),
        name="state_out",
    )(*args)


def _pad_ff(v):
    pad = [(0, 0)] * (v.ndim - 1) + [(0, D_FF_PAD - D_FF)]
    return jnp.concatenate([jnp.pad(v[..., :D_FF], pad), jnp.pad(v[..., D_FF:], pad)], axis=-1)


def _unpad_ff(v):
    return jnp.concatenate([v[..., :D_FF], v[..., D_FF_PAD:D_FF_PAD + D_FF]], axis=-1)


def _prep_layer(big, b_gate, conv_w, conv_b, dt_bias, a_log, d_skip, norm_w, pool_scale,
                ln1_g, ln1_b, fconv_w, fconv_b, ln2_g, ln2_b):
    pad_h = lambda v: jnp.pad(v, (0, DT_PAD - N_HEADS)).reshape(1, DT_PAD)
    head_of_channel = jnp.arange(D_INNER) // HEAD_DIM
    head_expand = (jnp.arange(DT_PAD)[:, None] == head_of_channel[None, :]).astype(BF16)
    return dict(
        big, b_gate=b_gate.reshape(1, -1),
        conv_w=conv_w, conv_b=conv_b.reshape(1, -1),
        dt_bias=pad_h(dt_bias), a_log=pad_h(a_log),
        d_skip_x=jnp.repeat(d_skip, HEAD_DIM).reshape(1, -1),
        norm_w=norm_w.reshape(1, -1), head_expand=head_expand,
        pool_scale=pool_scale.reshape(1, -1),
        ln1_g=ln1_g.reshape(1, -1), ln1_b=ln1_b.reshape(1, -1),
        fconv_w=_pad_ff(fconv_w), fconv_b=_pad_ff(fconv_b).reshape(1, -1),
        ln2_g=ln2_g.reshape(1, -1), ln2_b=ln2_b.reshape(1, -1),
    )


def _raw_xbc(proj):
    return proj[:, OFF_XS:OFF_XS + CONV_DIM]


def _layer_prompt(x, x_bf, lw, layer, bsz, seq):
    tm_in = 1024
    proj, xbc_tail = _in_proj(x_bf, lw, layer, tm_in, seq // tm_in, conv=True)
    dt_raw = _dt_proj(x_bf, lw["w_in_t"], layer, 1024)
    y, new_ssm = _ssd_prompt(proj, dt_raw, lw, bsz, seq)
    y_b = _pool_prompt(proj, lw["w_pool"], layer, lw["pool_scale"], seq)
    mix = _branch_merge(y, lw["w_br"], layer, proj, y_b, 512, 512)
    x1, x1_bf = _proj_ln(mix, lw["w_out"], layer, x, lw["ln1_g"], lw["ln1_b"], 512)
    tm_up = 1024
    act, tail_g, tail_v = _ffn_up_prompt(x1_bf, lw, layer, seq, tm_up)
    x2, x2_bf = _down_ln(act, lw["w_down"], layer, x1, lw["ln2_g"], lw["ln2_b"], 1024)
    p3 = proj.reshape(bsz, seq, N_MAIN)
    new_pool = p3[:, seq - POOL_BUF:, OFF_POOL:OFF_POOL + D_POOL]
    tps_in = seq // tm_in
    new_conv = xbc_tail[tps_in - 1::tps_in, CONV_HALO - (CONV_K - 1):, OFF_XS:OFF_XS + CONV_DIM]
    tps = seq // tm_up
    last = slice(tps - 1, None, tps)
    tail = jnp.concatenate([tail_g[last, :, :D_FF], tail_v[last, :, :D_FF]], axis=-1)
    new_ffn = tail[:, FFN_HALO - (FFN_K - 1):, :]
    return x2, x2_bf, new_ssm, new_conv, new_pool, new_ffn


def _layer_sample(x, x_bf, ssm_all, layer, ssm_prev_out, s_conv, s_pool, s_ffn, lw):
    s = x.shape[0]
    proj, _ = _in_proj(x_bf, lw, layer, s, 1, conv=False)
    dt_raw = _dt_proj(x_bf, lw["w_in_t"], layer, s)
    xs, b_m, c_m, decht, xdtt = _ssd_sample_prep(proj, dt_raw, jnp.swapaxes(s_conv, 0, 1), lw)
    new_ssm, yt = _ssd_sample_step(ssm_all, layer, ssm_prev_out, decht, xdtt, b_m, c_m)
    y = _ssd_sample_finish(yt, xs, proj, lw)
    pooled = _pool_sample(proj, s_pool, layer)
    y_b = _pool_mm(pooled, lw["w_pool"], layer, lw["pool_scale"], s)
    mix = _branch_merge(y, lw["w_br"], layer, proj, y_b, s, 512)
    x1, x1_bf = _proj_ln(mix, lw["w_out"], layer, x, lw["ln1_g"], lw["ln1_b"], s)
    act, h_g, h_v = _ffn_up_sample(x1_bf, s_ffn, lw, layer)
    x2, x2_bf = _down_ln(act, lw["w_down"], layer, x1, lw["ln2_g"], lw["ln2_b"], s)
    return x2, x2_bf, new_ssm, (proj, h_g, h_v)


def kernel(x_prompt, x_sample, state_ssm, state_ssd_conv, state_pool, state_ffn_conv, w_in, b_gate, conv_w, conv_b, dt_bias, a_log, d_skip, ssd_norm_w, w_ssd_branch, w_pool, pool_scale, w_out, ln1_g, ln1_b, w_up, ffn_conv_w, ffn_conv_b, w_down, ln2_g, ln2_b):
    bsz, seq, _ = x_prompt.shape
    n_s = x_sample.shape[0]
    assert x_sample.shape[1] == 1 and seq % 1024 == 0
    xp = x_prompt.reshape(bsz * seq, D_MODEL)
    xs = x_sample.reshape(n_s, D_MODEL)
    xp_bf, xs_bf = xp.astype(BF16), xs.astype(BF16)
    outs_p, outs_s = [], []
    ssm_all = state_ssm.reshape(DEPTH, n_s, D_INNER, D_STATE)
    ssm_out = None
    big = dict(w_in_t=jnp.swapaxes(w_in, 1, 2), w_br=w_ssd_branch, w_pool=w_pool, w_out=w_out,
               w_up=w_up, w_down=w_down)
    for i in range(DEPTH):
        lw = _prep_layer(big, b_gate[i], conv_w[i], conv_b[i], dt_bias[i], a_log[i], d_skip[i],
                         ssd_norm_w[i], pool_scale[i], ln1_g[i], ln1_b[i], ffn_conv_w[i],
                         ffn_conv_b[i], ln2_g[i], ln2_b[i])
        xp, xp_bf, *op = _layer_prompt(xp, xp_bf, lw, i, bsz, seq)
        xs, xs_bf, ssm_out, new_rows = _layer_sample(xs, xs_bf, ssm_all, i, ssm_out, state_ssd_conv[i],
                                                     state_pool, state_ffn_conv, lw)
        outs_p.append(op)
        outs_s.append(new_rows)
    new_conv_s, new_pool_s, new_ffn_s = _state_out(state_ssd_conv, state_pool, state_ffn_conv, outs_s)
    stack = lambda outs, k: jnp.stack([o[k] for o in outs])
    return (xp.reshape(bsz, seq, D_MODEL), xs.reshape(n_s, 1, D_MODEL),
            stack(outs_p, 0), stack(outs_p, 1), stack(outs_p, 2), stack(outs_p, 3),
            ssm_out.reshape(state_ssm.shape), new_conv_s, new_pool_s, new_ffn_s)
```

```python
import functools

import jax
import jax.numpy as jnp
from jax import lax
from jax.experimental import pallas as pl
from jax.experimental.pallas import tpu as pltpu

F32 = jnp.float32
BF16 = jnp.bfloat16

D_MODEL = 2048
HEAD_DIM = 64
D_INNER = 2 * D_MODEL
N_HEADS = D_INNER // HEAD_DIM
N_GROUPS = 8
HEADS_PER_GROUP = N_HEADS // N_GROUPS
GROUP_W = D_INNER // N_GROUPS
D_STATE = 128
CONV_K = 4
BC_W = 2 * N_GROUPS * D_STATE
CONV_DIM = D_INNER + BC_W
CHUNK = 128
D_POOL = D_MODEL
POOL_WINDOWS = (2, 4, 8, 16)
POOL_GC = D_POOL // len(POOL_WINDOWS)
POOL_BUF = max(POOL_WINDOWS) - 1
D_FF = 5504
FFN_K = 3
DEPTH = 2
PAST_LEN = 16384
ALPHA = (2 * DEPTH) ** 0.25
LN_EPS = 1e-5
RMS_EPS = 1e-5

LANES = 128
SUBLANES = 8
D_FF_PAD = 5632
FFN_TN = 512
DT_PAD = LANES
OFF_Z = 0
OFF_XS = D_INNER
OFF_BC = 2 * D_INNER
OFF_POOL = 2 * D_INNER + BC_W
OFF_GATE = OFF_POOL + D_POOL
N_MAIN = OFF_GATE + 2 * D_MODEL
VMEM_LIMIT = 56 * 1024 * 1024
NEG_BIG = -1e30


def _cparams(n_axes):
    return pltpu.CompilerParams(dimension_semantics=("arbitrary",) * n_axes,
                                vmem_limit_bytes=VMEM_LIMIT)


def _sigmoid(x):
    return 1.0 / (1.0 + jnp.exp(-x))


def _silu(x):
    return x * _sigmoid(x)


def _softplus(x):
    return jnp.maximum(x, 0.0) + jnp.log(1.0 + jnp.exp(-jnp.abs(x)))


def _rows_above(e, shift):
    n, c = e.shape
    rot = pltpu.roll(e.reshape(n // SUBLANES, SUBLANES, c), shift, 1).reshape(n, c)
    sub = lax.broadcasted_iota(jnp.int32, (n - SUBLANES, c), 0) % SUBLANES
    return jnp.where(sub < shift, rot[0:n - SUBLANES, :], rot[SUBLANES:n, :])


def _shift_rows(ext, shift, rows):
    return ext[SUBLANES:SUBLANES + rows, :] if shift == 0 else _rows_above(ext, shift)


def _layer_norm(r, g, b):
    mu = jnp.mean(r, axis=-1, keepdims=True)
    d = r - mu
    var = jnp.mean(d * d, axis=-1, keepdims=True)
    return d * lax.rsqrt(var + LN_EPS) * g + b


def _dt_proj_kernel(x_ref, w_ref, o_ref):
    o_ref[...] = lax.dot_general(x_ref[...], w_ref[...].astype(BF16), (((1,), (1,)), ((), ())),
                                 preferred_element_type=F32)


def _dt_proj(x_bf, w_in_t, layer, tm):
    m = x_bf.shape[0]
    return pl.pallas_call(
        _dt_proj_kernel,
        grid=(m // tm,),
        in_specs=[pl.BlockSpec((tm, D_MODEL), lambda i: (i, 0)),
                  pl.BlockSpec((None, DT_PAD, D_MODEL), lambda i: (layer, DT_COL_BLOCK, 0))],
        out_specs=pl.BlockSpec((tm, DT_PAD), lambda i: (i, 0)),
        out_shape=jax.ShapeDtypeStruct((m, DT_PAD), F32),
        compiler_params=_cparams(1),
        name="dt_proj",
    )(x_bf, w_in_t)


IN_TN = 1024
J_XBC = OFF_XS // IN_TN
J_POOL = OFF_POOL // IN_TN
J_GATE = OFF_GATE // IN_TN
J_END = N_MAIN // IN_TN
CONV_HALO = SUBLANES
PG_SHIFT = N_HEADS
DT_COL_BLOCK = (2 * D_INNER + BC_W) // DT_PAD


def _in_proj_kernel(x_ref, wa_ref, wb_ref, cw_ref, cb_ref, bg_ref, o_ref, tail_ref, w_s, h_s,
                    *, tiles_per_seq, conv):
    j = pl.program_id(0)
    i = pl.program_id(1)
    tm = x_ref.shape[0]

    @pl.when((i == 0) & (j < J_POOL))
    def _():
        w_s[...] = wa_ref[...].astype(BF16)

    @pl.when((i == 0) & (j >= J_POOL))
    def _():
        w_s[...] = jnp.concatenate([wa_ref[PG_SHIFT:, :], wb_ref[:PG_SHIFT, :]], axis=0).astype(BF16)

    def mm():
        return lax.dot_general(x_ref[...], w_s[...], (((1,), (1,)), ((), ())),
                               preferred_element_type=F32)

    is_xbc = (j >= J_XBC) & (j < J_POOL)

    @pl.when(jnp.logical_not(is_xbc))
    def _():
        tail_ref[0] = jnp.zeros((CONV_HALO, IN_TN), F32)

    @pl.when(j < J_XBC)
    def _():
        o_ref[...] = _silu(mm())

    @pl.when(is_xbc)
    def _():
        if conv:
            @pl.when(i % tiles_per_seq == 0)
            def _():
                h_s[...] = jnp.zeros((CONV_HALO, IN_TN), F32)

            ext = jnp.concatenate([h_s[...], mm()], axis=0)
            acc = cb_ref[...]
            for k in range(CONV_K):
                acc = acc + _shift_rows(ext, CONV_K - 1 - k, tm) * cw_ref[k:k + 1, :]
            o_ref[...] = _silu(acc)
            tail = ext[tm:tm + CONV_HALO, :]
            h_s[...] = tail
            tail_ref[0] = tail
        else:
            o_ref[...] = mm()
            tail_ref[0] = jnp.zeros((CONV_HALO, IN_TN), F32)

    @pl.when((j >= J_POOL) & (j < J_GATE))
    def _():
        o_ref[...] = mm()

    @pl.when(j >= J_GATE)
    def _():
        o_ref[...] = _sigmoid(mm() + bg_ref[...])


def _in_proj(x_bf, lw, layer, tm, tiles_per_seq, conv):
    m = x_bf.shape[0]
    ni = m // tm
    n_conv_tiles = CONV_DIM // IN_TN
    clamp = lambda v, lo, hi: jnp.minimum(jnp.maximum(v, lo), hi)
    conv_tile = lambda j: clamp(j - J_XBC, 0, n_conv_tiles - 1)
    kern = functools.partial(_in_proj_kernel, tiles_per_seq=tiles_per_seq, conv=conv)
    return pl.pallas_call(
        kern,
        grid=(J_END, ni),
        in_specs=[pl.BlockSpec((tm, D_MODEL), lambda j, i: (i, 0)),
                  pl.BlockSpec((None, IN_TN, D_MODEL), lambda j, i: (layer, j, 0)),
                  pl.BlockSpec((None, IN_TN, D_MODEL), lambda j, i: (layer, jnp.maximum(j + 1, J_POOL), 0),
                               pipeline_mode=pl.Buffered(1)),
                  pl.BlockSpec((CONV_K, IN_TN), lambda j, i: (0, conv_tile(j))),
                  pl.BlockSpec((1, IN_TN), lambda j, i: (0, conv_tile(j))),
                  pl.BlockSpec((1, IN_TN), lambda j, i: (0, clamp(j - J_GATE, 0, J_END - J_GATE - 1)))],
        out_specs=[pl.BlockSpec((tm, IN_TN), lambda j, i: (i, j)),
                   pl.BlockSpec((1, CONV_HALO, IN_TN), lambda j, i: (i, 0, j))],
        out_shape=[jax.ShapeDtypeStruct((m, N_MAIN), F32),
                   jax.ShapeDtypeStruct((ni, CONV_HALO, N_MAIN), F32)],
        scratch_shapes=[pltpu.VMEM((IN_TN, D_MODEL), BF16),
                        pltpu.VMEM((CONV_HALO, IN_TN), F32)],
        compiler_params=_cparams(2),
        name="in_proj",
    )(x_bf, lw["w_in_t"], lw["w_in_t"], lw["conv_w"], lw["conv_b"], lw["b_gate"])


def _split3(v):
    hi = v.astype(BF16)
    r1 = v - hi.astype(F32)
    mid = r1.astype(BF16)
    lo = (r1 - mid.astype(F32)).astype(BF16)
    return hi, mid, lo


def _ssd_prompt_kernel(xs_ref, bc_ref, zs_ref, dt_ref, dtb_ref, alog_ref, dsk_ref, nw_ref,
                       y_ref, st_ref,
                       xs_s, bt_s, c_s, y_s, state_s, acol_s, arow_s, dtrow_s):
    c = pl.program_id(1)
    n_chunks = pl.num_programs(1)
    q = CHUNK

    @pl.when(c == 0)
    def _():
        state_s[...] = jnp.zeros(state_s.shape, F32)

    for g in range(N_GROUPS):
        xs_s[g] = xs_ref[:, g * GROUP_W:(g + 1) * GROUP_W]
        bt_s[g] = bc_ref[:, g * D_STATE:(g + 1) * D_STATE].T
        c_s[g] = bc_ref[:, (N_GROUPS + g) * D_STATE:(N_GROUPS + g + 1) * D_STATE].astype(BF16)

    dt = _softplus(dt_ref[...] + dtb_ref[...])
    a_neg = -jnp.exp(alog_ref[...])
    d_a = dt * a_neg
    row = lax.broadcasted_iota(jnp.int32, (q, q), 0)
    col = lax.broadcasted_iota(jnp.int32, (q, q), 1)
    causal = row >= col
    tril = jnp.where(causal, 1.0, 0.0).astype(BF16)
    hi, mid, lo3 = _split3(d_a)
    a_cum = (jnp.dot(tril, hi, preferred_element_type=F32)
             + jnp.dot(tril, mid, preferred_element_type=F32)
             + jnp.dot(tril, lo3, preferred_element_type=F32))
    arow_s[...] = a_cum.T
    dtrow_s[...] = dt.T
    for g in range(N_GROUPS):
        sh = (LANES - HEADS_PER_GROUP * g) % LANES
        acol_s[g] = a_cum if sh == 0 else pltpu.roll(a_cum, sh, 1)

    lane = lax.broadcasted_iota(jnp.int32, (q, LANES), 1)
    lo_half = lane < HEAD_DIM

    def group_body(g, carry):
        acol = acol_s[g]
        c_g = c_s[g]
        bt_g = bt_s[g]
        cb = jnp.dot(c_g, bt_g.astype(BF16), preferred_element_type=F32)
        y_off_g = jnp.dot(c_g, state_s[g].astype(BF16), preferred_element_type=F32)
        for k in range(HEADS_PER_GROUP // 2):
            l_parts, b_parts, a_b = [], [], []
            for e in range(2):
                hh = 2 * k + e
                head = g * HEADS_PER_GROUP + hh
                a_col = jnp.broadcast_to(acol[:, hh:hh + 1], (q, q))
                a_row = arow_s[pl.ds(head, 1), :]
                dt_row = dtrow_s[pl.ds(head, 1), :]
                seg = jnp.where(causal, a_col - a_row, NEG_BIG)
                l_parts.append((cb * jnp.exp(seg) * dt_row).astype(BF16))
                w_row = dt_row * jnp.exp(a_col[q - 1:q, :] - a_row)
                b_parts.append((bt_g * w_row).astype(BF16))
                a_b.append(a_col)
            lhs = jnp.concatenate([jnp.concatenate(l_parts, axis=1),
                                   jnp.concatenate(b_parts, axis=1)], axis=0)
            cols = slice(k * LANES, (k + 1) * LANES)
            xs_bf = xs_s[g, :, cols].astype(BF16)
            zero = jnp.zeros_like(xs_bf)
            rhs = jnp.concatenate([jnp.where(lo_half, xs_bf, zero),
                                   jnp.where(lo_half, zero, xs_bf)], axis=0)
            res = jnp.dot(lhs, rhs, preferred_element_type=F32)
            a_pair = jnp.where(lo_half, a_b[0], a_b[1])
            y_s[g, :, cols] = res[0:q] + y_off_g[:, cols] * jnp.exp(a_pair)
            cdec = jnp.exp(a_pair[q - 1:q, :])
            state_s[g, :, cols] = state_s[g, :, cols] * cdec + res[q:2 * q]
        return carry

    lax.fori_loop(0, N_GROUPS, group_body, 0)

    for g in range(N_GROUPS):
        cols = slice(g * GROUP_W, (g + 1) * GROUP_W)
        yv = y_s[g] + dsk_ref[:, cols] * xs_s[g]
        v = yv * zs_ref[:, cols]
        ms = jnp.mean(v * v, axis=-1, keepdims=True)
        y_ref[:, cols] = (v * lax.rsqrt(ms + RMS_EPS) * nw_ref[:, cols]).astype(y_ref.dtype)

    @pl.when(c == n_chunks - 1)
    def _():
        for g in range(N_GROUPS):
            st_ref[0, g] = state_s[g].T


def _ssd_prompt(proj, dt_raw, lw, bsz, seq):
    n_chunks = seq // CHUNK
    rows = lambda b, c: b * n_chunks + c
    small = lambda shape: pl.BlockSpec(shape, lambda b, c: (0, 0))
    y, st = pl.pallas_call(
        _ssd_prompt_kernel,
        grid=(bsz, n_chunks),
        in_specs=[
            pl.BlockSpec((CHUNK, D_INNER), lambda b, c: (rows(b, c), OFF_XS // D_INNER)),
            pl.BlockSpec((CHUNK, BC_W), lambda b, c: (rows(b, c), OFF_BC // BC_W)),
            pl.BlockSpec((CHUNK, D_INNER), lambda b, c: (rows(b, c), OFF_Z // D_INNER)),
            pl.BlockSpec((CHUNK, DT_PAD), lambda b, c: (rows(b, c), 0)),
            small((1, DT_PAD)), small((1, DT_PAD)), small((1, D_INNER)), small((1, D_INNER)),
        ],
        out_specs=[
            pl.BlockSpec((CHUNK, D_INNER), lambda b, c: (rows(b, c), 0)),
            pl.BlockSpec((1, N_GROUPS, GROUP_W, D_STATE), lambda b, c: (b, 0, 0, 0)),
        ],
        out_shape=[jax.ShapeDtypeStruct((bsz * seq, D_INNER), BF16),
                   jax.ShapeDtypeStruct((bsz, N_GROUPS, GROUP_W, D_STATE), F32)],
        scratch_shapes=[
            pltpu.VMEM((N_GROUPS, CHUNK, GROUP_W), F32),
            pltpu.VMEM((N_GROUPS, D_STATE, CHUNK), F32),
            pltpu.VMEM((N_GROUPS, CHUNK, D_STATE), BF16),
            pltpu.VMEM((N_GROUPS, CHUNK, GROUP_W), F32),
            pltpu.VMEM((N_GROUPS, D_STATE, GROUP_W), F32),
            pltpu.VMEM((N_GROUPS, CHUNK, LANES), F32),
            pltpu.VMEM((LANES, CHUNK), F32),
            pltpu.VMEM((LANES, CHUNK), F32),
        ],
        compiler_params=_cparams(2),
        name="ssd_prompt",
    )(proj, proj, proj, dt_raw, lw["dt_bias"], lw["a_log"], lw["d_skip_x"], lw["norm_w"])
    return y, st.reshape(bsz, N_HEADS, HEAD_DIM, D_STATE)


def _ssd_sample_prep_kernel(xs_ref, bc_ref, dt_ref, cst_ref, cw_ref, cb_ref, dtb_ref, alog_ref,
                            ex_ref, xs_o, b_o, c_o, decht_o, xdtt_o):
    def conv(u, lo, hi):
        acc = cb_ref[:, lo:hi]
        for j in range(CONV_K - 1):
            acc = acc + cst_ref[j][:, lo:hi] * cw_ref[j:j + 1, lo:hi]
        acc = acc + u * cw_ref[CONV_K - 1:CONV_K, lo:hi]
        return _silu(acc)

    xs = conv(xs_ref[...], 0, D_INNER)
    bc = conv(bc_ref[...], D_INNER, CONV_DIM)
    xs_o[...] = xs
    b_o[...] = bc[:, 0:N_GROUPS * D_STATE]
    c_o[...] = bc[:, N_GROUPS * D_STATE:BC_W]
    dt = _softplus(dt_ref[...] + dtb_ref[...])
    d_a = dt * (-jnp.exp(alog_ref[...]))
    ex = ex_ref[...]

    def expand(v):
        hi, mid, lo3 = _split3(v)
        return (jnp.dot(hi, ex, preferred_element_type=F32)
                + jnp.dot(mid, ex, preferred_element_type=F32)
                + jnp.dot(lo3, ex, preferred_element_type=F32))

    xdt = expand(dt) * xs
    decht_o[...] = jnp.exp(d_a).T
    xdtt_o[...] = xdt.T


def _ssd_sample_prep(proj, dt_raw, conv_state_t, lw):
    s = proj.shape[0]
    full = lambda shape: pl.BlockSpec(shape, lambda i: (0,) * len(shape))
    return pl.pallas_call(
        _ssd_sample_prep_kernel,
        grid=(1,),
        in_specs=[
            pl.BlockSpec((s, D_INNER), lambda i: (0, OFF_XS // D_INNER)),
            pl.BlockSpec((s, BC_W), lambda i: (0, OFF_BC // BC_W)),
            full((s, DT_PAD)), full((CONV_K - 1, s, CONV_DIM)),
            full((CONV_K, CONV_DIM)), full((1, CONV_DIM)), full((1, DT_PAD)), full((1, DT_PAD)),
            full((DT_PAD, D_INNER)),
        ],
        out_specs=[full((s, D_INNER)), full((s, N_GROUPS * D_STATE)), full((s, N_GROUPS * D_STATE)),
                   full((DT_PAD, s)), full((D_INNER, s))],
        out_shape=[jax.ShapeDtypeStruct((s, D_INNER), F32),
                   jax.ShapeDtypeStruct((s, N_GROUPS * D_STATE), F32),
                   jax.ShapeDtypeStruct((s, N_GROUPS * D_STATE), F32),
                   jax.ShapeDtypeStruct((DT_PAD, s), F32),
                   jax.ShapeDtypeStruct((D_INNER, s), F32)],
        compiler_params=_cparams(1),
        name="ssd_sample_prep",
    )(proj, proj, dt_raw, conv_state_t, lw["conv_w"], lw["conv_b"], lw["dt_bias"], lw["a_log"],
      lw["head_expand"])


def _ssd_sample_step_kernel(st_ref, decht_ref, xdtt_ref, b_ref, c_ref, *rest, fill_other_layers):
    st_o, yt_o = rest[-2:]
    s = pl.program_id(0)
    n_s = xdtt_ref.shape[1]
    onehot = lax.broadcasted_iota(jnp.int32, (GROUP_W, n_s), 1) == s
    onehot_h = lax.broadcasted_iota(jnp.int32, (DT_PAD, n_s), 1) == s

    @pl.when(s == 0)
    def _():
        yt_o[...] = jnp.zeros(yt_o.shape, F32)

    if fill_other_layers:
        st_o[1:] = jnp.zeros((st_o.shape[0] - 1,) + st_o.shape[1:], F32)

    dech = jnp.sum(jnp.where(onehot_h, decht_ref[...], 0.0), axis=-1, keepdims=True)
    b_all = b_ref[pl.ds(s, 1), :]
    c_all = c_ref[pl.ds(s, 1), :]
    for g in range(N_GROUPS):
        rows = slice(g * GROUP_W, (g + 1) * GROUP_W)
        cols = slice(g * D_STATE, (g + 1) * D_STATE)
        xdt = jnp.sum(jnp.where(onehot, xdtt_ref[rows, :], 0.0), axis=-1, keepdims=True)
        b_row = b_all[:, cols]
        c_row = c_all[:, cols]
        upd = xdt * b_row
        parts = []
        for hh in range(HEADS_PER_GROUP):
            h = g * HEADS_PER_GROUP + hh
            r_h = slice(h * HEAD_DIM, (h + 1) * HEAD_DIM)
            parts.append(st_ref[0, 0, r_h, :] * dech[h:h + 1, :]
                         + upd[hh * HEAD_DIM:(hh + 1) * HEAD_DIM, :])
        st_new = jnp.concatenate(parts, axis=0)
        st_o[0, 0, rows, :] = st_new
        y_col = jnp.sum(st_new * c_row, axis=-1, keepdims=True)
        yt_o[rows, :] = yt_o[rows, :] + jnp.where(onehot, y_col, 0.0)


def _ssd_sample_step(state_all, layer, prev_out, decht, xdtt, b_m, c_m):
    depth = state_all.shape[0]
    s = xdtt.shape[1]
    full = lambda shape: pl.BlockSpec(shape, lambda i: (0,) * len(shape))
    in_specs = [pl.BlockSpec((1, 1, D_INNER, D_STATE), lambda i: (layer, i, 0, 0)),
                full((DT_PAD, s)), full((D_INNER, s)),
                full((s, N_GROUPS * D_STATE)), full((s, N_GROUPS * D_STATE))]
    args = [state_all, decht, xdtt, b_m, c_m]
    if prev_out is None:
        assert layer == 0
        aliases = {}
        st_out = pl.BlockSpec((depth, 1, D_INNER, D_STATE), lambda i: (0, i, 0, 0))
    else:
        in_specs.append(pl.BlockSpec(memory_space=pl.ANY))
        args.append(prev_out)
        aliases = {len(args) - 1: 0}
        st_out = pl.BlockSpec((1, 1, D_INNER, D_STATE), lambda i: (layer, i, 0, 0))
    return pl.pallas_call(
        functools.partial(_ssd_sample_step_kernel, fill_other_layers=prev_out is None),
        grid=(s,),
        in_specs=in_specs,
        out_specs=[st_out, full((D_INNER, s))],
        out_shape=[jax.ShapeDtypeStruct(state_all.shape, F32),
                   jax.ShapeDtypeStruct((D_INNER, s), F32)],
        input_output_aliases=aliases,
        compiler_params=_cparams(1),
        name="ssd_sample_step",
    )(*args)


def _ssd_sample_finish_kernel(yt_ref, xs_ref, zs_ref, dsk_ref, nw_ref, y_ref):
    y = yt_ref[...].T
    for g in range(N_GROUPS):
        cols = slice(g * GROUP_W, (g + 1) * GROUP_W)
        yv = y[:, cols] + dsk_ref[:, cols] * xs_ref[:, cols]
        v = yv * zs_ref[:, cols]
        ms = jnp.mean(v * v, axis=-1, keepdims=True)
        y_ref[:, cols] = (v * lax.rsqrt(ms + RMS_EPS) * nw_ref[:, cols]).astype(y_ref.dtype)


def _ssd_sample_finish(yt, xs, proj, lw):
    s = xs.shape[0]
    full = lambda shape: pl.BlockSpec(shape, lambda i: (0,) * len(shape))
    return pl.pallas_call(
        _ssd_sample_finish_kernel,
        grid=(1,),
        in_specs=[full((D_INNER, s)), full((s, D_INNER)),
                  pl.BlockSpec((s, D_INNER), lambda i: (0, OFF_Z // D_INNER)),
                  full((1, D_INNER)), full((1, D_INNER))],
        out_specs=full((s, D_INNER)),
        out_shape=jax.ShapeDtypeStruct((s, D_INNER), BF16),
        compiler_params=_cparams(1),
        name="ssd_sample_finish",
    )(yt, xs, proj, lw["d_skip_x"], lw["norm_w"])


POOL_TP = 512
POOL_HALO = 32


def _pool_prompt_kernel(u_ref, w_ref, sc_ref, o_ref, ext_s, w_s, *, tiles_per_seq):
    g = pl.program_id(0)
    i = pl.program_id(1)
    tp = POOL_TP

    @pl.when(i == 0)
    def _():
        w_s[...] = w_ref[...].astype(BF16)

    @pl.when(i % tiles_per_seq == 0)
    def _():
        ext_s[...] = jnp.zeros((POOL_HALO, POOL_GC), F32)

    u = u_ref[...]
    ext = jnp.concatenate([ext_s[...], u], axis=0)
    pos = (i % tiles_per_seq) * tp + lax.broadcasted_iota(jnp.int32, (tp, 1), 0)
    for gi, win in enumerate(POOL_WINDOWS):
        @pl.when(g == gi)
        def _():
            tot, width = ext, 1
            while width < win:
                if width < SUBLANES:
                    tot = tot[SUBLANES:, :] + _rows_above(tot, width)
                else:
                    tot = tot[width:, :] + tot[:-width, :]
                width *= 2
            tot = tot[tot.shape[0] - tp:, :]
            cnt = jnp.minimum(win, pos + 1).astype(F32)
            pooled = (tot / cnt - u).astype(BF16)
            o_ref[...] = jnp.dot(pooled, w_s[...], preferred_element_type=F32) * sc_ref[...]
    ext_s[...] = ext[tp:tp + POOL_HALO, :]


def _pool_prompt(proj, w_pool, layer, scale, seq):
    m = proj.shape[0]
    ng = len(POOL_WINDOWS)
    kern = functools.partial(_pool_prompt_kernel, tiles_per_seq=seq // POOL_TP)
    return pl.pallas_call(
        kern,
        grid=(ng, m // POOL_TP),
        in_specs=[pl.BlockSpec((POOL_TP, POOL_GC), lambda g, i: (i, OFF_POOL // POOL_GC + g)),
                  pl.BlockSpec((None, None, POOL_GC, POOL_GC), lambda g, i: (layer, g, 0, 0)),
                  pl.BlockSpec((1, POOL_GC), lambda g, i: (0, g))],
        out_specs=pl.BlockSpec((POOL_TP, POOL_GC), lambda g, i: (i, g)),
        out_shape=jax.ShapeDtypeStruct((m, D_POOL), F32),
        scratch_shapes=[pltpu.VMEM((POOL_HALO, POOL_GC), F32),
                        pltpu.VMEM((POOL_GC, POOL_GC), BF16)],
        compiler_params=_cparams(2),
        name="pool_prompt",
    )(proj, w_pool, scale)


def _pool_sample_kernel(u_ref, buf_ref, o_ref):
    g = pl.program_id(0)
    for gi, win in enumerate(POOL_WINDOWS):
        @pl.when(g == gi)
        def _():
            u = u_ref[...]
            tot = u
            for k in range(1, win):
                tot = tot + buf_ref[:, POOL_BUF - k, :]
            cnt = float(min(win, PAST_LEN + 1))
            o_ref[...] = (tot / cnt - u).astype(o_ref.dtype)


def _pool_sample(proj, pool_state, layer):
    s = proj.shape[0]
    return pl.pallas_call(
        _pool_sample_kernel,
        grid=(len(POOL_WINDOWS),),
        in_specs=[pl.BlockSpec((s, POOL_GC), lambda g: (0, OFF_POOL // POOL_GC + g)),
                  pl.BlockSpec((None, s, POOL_BUF, POOL_GC), lambda g: (layer, 0, 0, g))],
        out_specs=pl.BlockSpec((s, POOL_GC), lambda g: (0, g)),
        out_shape=jax.ShapeDtypeStruct((s, D_POOL), BF16),
        compiler_params=_cparams(1),
        name="pool_sample",
    )(proj, pool_state)


def _pool_mm_kernel(p_ref, w_ref, sc_ref, o_ref, w_s):
    @pl.when(pl.program_id(1) == 0)
    def _():
        w_s[...] = w_ref[...].astype(BF16)

    acc = jnp.dot(p_ref[...], w_s[...], preferred_element_type=F32)
    o_ref[...] = acc * sc_ref[...]


def _pool_mm(pooled, w_pool, layer, scale, tm):
    m = pooled.shape[0]
    ng = len(POOL_WINDOWS)
    return pl.pallas_call(
        _pool_mm_kernel,
        grid=(ng, m // tm),
        in_specs=[pl.BlockSpec((tm, POOL_GC), lambda g, i: (i, g)),
                  pl.BlockSpec((None, None, POOL_GC, POOL_GC), lambda g, i: (layer, g, 0, 0)),
                  pl.BlockSpec((1, POOL_GC), lambda g, i: (0, g))],
        out_specs=pl.BlockSpec((tm, POOL_GC), lambda g, i: (i, g)),
        out_shape=jax.ShapeDtypeStruct((m, D_POOL), F32),
        scratch_shapes=[pltpu.VMEM((POOL_GC, POOL_GC), BF16)],
        compiler_params=_cparams(2),
        name="pool_mm",
    )(pooled, w_pool, scale)


def _branch_merge_kernel(y_ref, w_ref, ga_ref, gb_ref, yb_ref, o_ref, w_s):
    @pl.when(pl.program_id(1) == 0)
    def _():
        w_s[...] = w_ref[...].astype(BF16)

    y_a = jnp.dot(y_ref[...], w_s[...], preferred_element_type=F32)
    o_ref[...] = (ga_ref[...] * y_a + gb_ref[...] * yb_ref[...]).astype(o_ref.dtype)


def _branch_merge(y, w_br, layer, proj, y_b, tm, tn):
    m = y.shape[0]
    nj = D_MODEL // tn
    ga0 = OFF_GATE // tn
    return pl.pallas_call(
        _branch_merge_kernel,
        grid=(nj, m // tm),
        in_specs=[pl.BlockSpec((tm, D_INNER), lambda j, i: (i, 0)),
                  pl.BlockSpec((None, D_INNER, tn), lambda j, i: (layer, 0, j)),
                  pl.BlockSpec((tm, tn), lambda j, i: (i, ga0 + j)),
                  pl.BlockSpec((tm, tn), lambda j, i: (i, ga0 + nj + j)),
                  pl.BlockSpec((tm, tn), lambda j, i: (i, j))],
        out_specs=pl.BlockSpec((tm, tn), lambda j, i: (i, j)),
        out_shape=jax.ShapeDtypeStruct((m, D_MODEL), BF16),
        scratch_shapes=[pltpu.VMEM((D_INNER, tn), BF16)],
        compiler_params=_cparams(2),
        name="branch_merge",
    )(y, w_br, proj, proj, y_b)


def _proj_ln_kernel(m_ref, w_ref, x_ref, g_ref, b_ref, o_ref, obf_ref, w_s):
    @pl.when(pl.program_id(0) == 0)
    def _():
        w_s[...] = w_ref[...].astype(BF16)

    acc = jnp.dot(m_ref[...], w_s[...], preferred_element_type=F32)
    y = _layer_norm(ALPHA * x_ref[...] + acc, g_ref[...], b_ref[...])
    o_ref[...] = y
    obf_ref[...] = y.astype(BF16)


def _proj_ln(mix, w, layer, x, g, b, tm):
    m, k = mix.shape
    return pl.pallas_call(
        _proj_ln_kernel,
        grid=(m // tm,),
        in_specs=[pl.BlockSpec((tm, k), lambda i: (i, 0)),
                  pl.BlockSpec((None, k, D_MODEL), lambda i: (layer, 0, 0),
                               pipeline_mode=pl.Buffered(1)),
                  pl.BlockSpec((tm, D_MODEL), lambda i: (i, 0)),
                  pl.BlockSpec((1, D_MODEL), lambda i: (0, 0)),
                  pl.BlockSpec((1, D_MODEL), lambda i: (0, 0))],
        out_specs=[pl.BlockSpec((tm, D_MODEL), lambda i: (i, 0)),
                   pl.BlockSpec((tm, D_MODEL), lambda i: (i, 0))],
        out_shape=[jax.ShapeDtypeStruct((m, D_MODEL), F32),
                   jax.ShapeDtypeStruct((m, D_MODEL), BF16)],
        scratch_shapes=[pltpu.VMEM((k, D_MODEL), BF16)],
        compiler_params=_cparams(1),
        name="proj_ln",
    )(mix, w, x, g, b)


FFN_HALO = SUBLANES
FFN_VAL_BLK = D_FF // FFN_TN
FFN_VAL_SHIFT = D_FF % FFN_TN
assert FFN_VAL_SHIFT % LANES == 0


def _ffn_weight_tiles(j, wg_ref, wva_ref, wvb_ref, wg_s, wv_s):
    col = j * FFN_TN + lax.broadcasted_iota(jnp.int32, (1, FFN_TN), 1)
    valid = col < D_FF
    wg_s[...] = jnp.where(valid, wg_ref[...], 0.0).astype(BF16)
    wv = jnp.concatenate([wva_ref[:, FFN_VAL_SHIFT:], wvb_ref[:, :FFN_VAL_SHIFT]], axis=1)
    wv_s[...] = jnp.where(valid, wv, 0.0).astype(BF16)


def _ffn_w_specs(layer, index_of):
    blk = (None, D_MODEL, FFN_TN)
    return [pl.BlockSpec(blk, index_of(lambda j: (layer, 0, j))),
            pl.BlockSpec(blk, index_of(lambda j: (layer, 0, FFN_VAL_BLK + j))),
            pl.BlockSpec(blk, index_of(lambda j: (layer, 0, FFN_VAL_BLK + j + 1)))]


def _ffn_conv_gate(ext_g, ext_v, cw_g, cw_v, cb_g, cb_v, tm):
    def conv(ext, cw, cb):
        acc = cb[...]
        for j in range(FFN_K):
            acc = acc + _shift_rows(ext, FFN_K - 1 - j, tm) * cw[j:j + 1, :]
        return acc
    return _silu(conv(ext_g, cw_g, cb_g)) * conv(ext_v, cw_v, cb_v)


def _ffn_up_prompt_kernel(x_ref, wg_ref, wva_ref, wvb_ref, cwg_ref, cwv_ref, cbg_ref, cbv_ref,
                          a_ref, tg_ref, tv_ref, wg_s, wv_s, hg_s, hv_s, *, tiles_per_seq):
    j = pl.program_id(0)
    i = pl.program_id(1)
    tm = x_ref.shape[0]

    @pl.when(i == 0)
    def _():
        _ffn_weight_tiles(j, wg_ref, wva_ref, wvb_ref, wg_s, wv_s)

    @pl.when(i % tiles_per_seq == 0)
    def _():
        hg_s[...] = jnp.zeros((FFN_HALO, FFN_TN), F32)
        hv_s[...] = jnp.zeros((FFN_HALO, FFN_TN), F32)

    x = x_ref[...]
    ext_g = jnp.concatenate([hg_s[...], jnp.dot(x, wg_s[...], preferred_element_type=F32)], axis=0)
    ext_v = jnp.concatenate([hv_s[...], jnp.dot(x, wv_s[...], preferred_element_type=F32)], axis=0)
    a_ref[...] = _ffn_conv_gate(ext_g, ext_v, cwg_ref, cwv_ref, cbg_ref, cbv_ref, tm).astype(a_ref.dtype)
    tail_g = ext_g[tm:tm + FFN_HALO, :]
    tail_v = ext_v[tm:tm + FFN_HALO, :]
    hg_s[...] = tail_g
    hv_s[...] = tail_v
    tg_ref[0] = tail_g
    tv_ref[0] = tail_v


def _ffn_up_prompt(x_bf, lw, layer, seq, tm):
    m = x_bf.shape[0]
    nj = D_FF_PAD // FFN_TN
    ni = m // tm
    kern = functools.partial(_ffn_up_prompt_kernel, tiles_per_seq=seq // tm)
    cspec_g = lambda r: pl.BlockSpec((r, FFN_TN), lambda j, i: (0, j))
    cspec_v = lambda r: pl.BlockSpec((r, FFN_TN), lambda j, i: (0, nj + j))
    w_specs = _ffn_w_specs(layer, lambda f: (lambda j, i: f(j)))
    return pl.pallas_call(
        kern,
        grid=(nj, ni),
        in_specs=[pl.BlockSpec((tm, D_MODEL), lambda j, i: (i, 0)), *w_specs,
                  cspec_g(FFN_K), cspec_v(FFN_K), cspec_g(1), cspec_v(1)],
        out_specs=[pl.BlockSpec((tm, FFN_TN), lambda j, i: (i, j)),
                   pl.BlockSpec((1, FFN_HALO, FFN_TN), lambda j, i: (i, 0, j)),
                   pl.BlockSpec((1, FFN_HALO, FFN_TN), lambda j, i: (i, 0, j))],
        out_shape=[jax.ShapeDtypeStruct((m, D_FF_PAD), BF16),
                   jax.ShapeDtypeStruct((ni, FFN_HALO, D_FF_PAD), F32),
                   jax.ShapeDtypeStruct((ni, FFN_HALO, D_FF_PAD), F32)],
        scratch_shapes=[pltpu.VMEM((D_MODEL, FFN_TN), BF16),
                        pltpu.VMEM((D_MODEL, FFN_TN), BF16),
                        pltpu.VMEM((FFN_HALO, FFN_TN), F32),
                        pltpu.VMEM((FFN_HALO, FFN_TN), F32)],
        compiler_params=_cparams(2),
        name="ffn_up_prompt",
    )(x_bf, lw["w_up"], lw["w_up"], lw["w_up"],
      lw["fconv_w"], lw["fconv_w"], lw["fconv_b"], lw["fconv_b"])


def _ffn_up_sample_kernel(x_ref, wg_ref, wva_ref, wvb_ref, sg_ref, sva_ref, svb_ref, cwg_ref, cwv_ref,
                          cbg_ref, cbv_ref, a_ref, hg_ref, hv_ref, wg_s, wv_s):
    j = pl.program_id(0)
    _ffn_weight_tiles(j, wg_ref, wva_ref, wvb_ref, wg_s, wv_s)
    x = x_ref[...]
    hg = jnp.dot(x, wg_s[...], preferred_element_type=F32)
    hv = jnp.dot(x, wv_s[...], preferred_element_type=F32)
    hg_ref[...] = hg
    hv_ref[...] = hv
    valid = j * FFN_TN + lax.broadcasted_iota(jnp.int32, (1, FFN_TN), 1) < D_FF

    def conv(h, rows, cw, cb):
        acc = cb[...]
        for k in range(FFN_K - 1):
            acc = acc + jnp.where(valid, rows(k), 0.0) * cw[k:k + 1, :]
        return acc + h * cw[FFN_K - 1:FFN_K, :]

    gate_rows = lambda k: sg_ref[:, k, :]
    value_rows = lambda k: jnp.concatenate([sva_ref[:, k, FFN_VAL_SHIFT:],
                                            svb_ref[:, k, :FFN_VAL_SHIFT]], axis=1)
    a_ref[...] = (_silu(conv(hg, gate_rows, cwg_ref, cbg_ref))
                  * conv(hv, value_rows, cwv_ref, cbv_ref)).astype(a_ref.dtype)


def _ffn_up_sample(x_bf, ffn_state, lw, layer):
    s = x_bf.shape[0]
    nj = D_FF_PAD // FFN_TN
    g_blk = lambda r: pl.BlockSpec((r, FFN_TN), lambda j: (0, j))
    v_blk = lambda r: pl.BlockSpec((r, FFN_TN), lambda j: (0, nj + j))
    w_specs = _ffn_w_specs(layer, lambda f: f)
    st_blk = (None, s, FFN_K - 1, FFN_TN)
    return pl.pallas_call(
        _ffn_up_sample_kernel,
        grid=(nj,),
        in_specs=[pl.BlockSpec((s, D_MODEL), lambda j: (0, 0)), *w_specs,
                  pl.BlockSpec(st_blk, lambda j: (layer, 0, 0, j)),
                  pl.BlockSpec(st_blk, lambda j: (layer, 0, 0, FFN_VAL_BLK + j)),
                  pl.BlockSpec(st_blk, lambda j: (layer, 0, 0, FFN_VAL_BLK + j + 1)),
                  g_blk(FFN_K), v_blk(FFN_K), g_blk(1), v_blk(1)],
        out_specs=[pl.BlockSpec((s, FFN_TN), lambda j: (0, j)),
                   pl.BlockSpec((s, FFN_TN), lambda j: (0, j)),
                   pl.BlockSpec((s, FFN_TN), lambda j: (0, j))],
        out_shape=[jax.ShapeDtypeStruct((s, D_FF_PAD), BF16),
                   jax.ShapeDtypeStruct((s, D_FF_PAD), F32),
                   jax.ShapeDtypeStruct((s, D_FF_PAD), F32)],
        scratch_shapes=[pltpu.VMEM((D_MODEL, FFN_TN), BF16),
                        pltpu.VMEM((D_MODEL, FFN_TN), BF16)],
        compiler_params=_cparams(1),
        name="ffn_up_sample",
    )(x_bf, lw["w_up"], lw["w_up"], lw["w_up"], ffn_state, ffn_state, ffn_state,
      lw["fconv_w"], lw["fconv_w"], lw["fconv_b"], lw["fconv_b"])


DOWN_TK = 512


def _down_ln_kernel(a_ref, w_ref, x_ref, g_ref, b_ref, o_ref, obf_ref):
    k = pl.program_id(1)

    @pl.when(k == 0)
    def _():
        o_ref[...] = ALPHA * x_ref[...]

    row = k * DOWN_TK + lax.broadcasted_iota(jnp.int32, (DOWN_TK, 1), 0)
    w = jnp.where(row < D_FF, w_ref[...], 0.0).astype(BF16)
    o_ref[...] += jnp.dot(a_ref[...], w, preferred_element_type=F32)

    @pl.when(k == pl.num_programs(1) - 1)
    def _():
        y = _layer_norm(o_ref[...], g_ref[...], b_ref[...])
        o_ref[...] = y
        obf_ref[...] = y.astype(BF16)


def _down_ln(a, w, layer, x, g, b, tm):
    m = a.shape[0]
    nk = D_FF_PAD // DOWN_TK
    return pl.pallas_call(
        _down_ln_kernel,
        grid=(m // tm, nk),
        in_specs=[pl.BlockSpec((tm, DOWN_TK), lambda i, k: (i, k)),
                  pl.BlockSpec((None, DOWN_TK, D_MODEL), lambda i, k: (layer, k, 0)),
                  pl.BlockSpec((tm, D_MODEL), lambda i, k: (i, 0), pipeline_mode=pl.Buffered(1)),
                  pl.BlockSpec((1, D_MODEL), lambda i, k: (0, 0)),
                  pl.BlockSpec((1, D_MODEL), lambda i, k: (0, 0))],
        out_specs=[pl.BlockSpec((tm, D_MODEL), lambda i, k: (i, 0)),
                   pl.BlockSpec((tm, D_MODEL), lambda i, k: (i, 0))],
        out_shape=[jax.ShapeDtypeStruct((m, D_MODEL), F32),
                   jax.ShapeDtypeStruct((m, D_MODEL), BF16)],
        compiler_params=_cparams(2),
        name="down_ln",
    )(a, w, x, g, b)


STATE_NS = 8


def _state_out_kernel(sc_ref, sp_ref, sf_ref, *rest):
    oc_ref, op_ref, of_ref = rest[-3:]
    new_rows = rest[:-3]
    layer = pl.program_id(0)
    oc_ref[0, :, 0:CONV_K - 2, :] = sc_ref[0, :, 1:CONV_K - 1, :]
    op_ref[0, :, 0:POOL_BUF - 1, :] = sp_ref[0, :, 1:POOL_BUF, :]
    of_ref[0, :, 0:FFN_K - 2, :] = sf_ref[0, :, 1:FFN_K - 1, :]
    for li in range(len(new_rows) // 5):
        xs, bc, po, hg, hv = new_rows[5 * li:5 * li + 5]

        @pl.when(layer == li)
        def _():
            oc_ref[0, :, CONV_K - 2, 0:D_INNER] = xs[...]
            oc_ref[0, :, CONV_K - 2, D_INNER:CONV_DIM] = bc[...]
            op_ref[0, :, POOL_BUF - 1, :] = po[...]
            of_ref[0, :, FFN_K - 2, 0:D_FF] = hg[:, 0:D_FF]
            of_ref[0, :, FFN_K - 2, D_FF:2 * D_FF] = hv[:, 0:D_FF]


def _state_out(s_conv, s_pool, s_ffn, per_layer):
    depth, s = s_conv.shape[:2]
    ns = STATE_NS
    blk = lambda a: pl.BlockSpec((1, ns) + a.shape[2:], lambda l, i: (l, i, 0, 0))
    rows = lambda w, cb: pl.BlockSpec((ns, w), lambda l, i: (i, cb))
    in_specs = [blk(s_conv), blk(s_pool), blk(s_ffn)]
    args = [s_conv, s_pool, s_ffn]
    for proj, h_g, h_v in per_layer:
        in_specs += [rows(D_INNER, OFF_XS // D_INNER), rows(BC_W, OFF_BC // BC_W),
                     rows(D_POOL, OFF_POOL // D_POOL), rows(D_FF_PAD, 0), rows(D_FF_PAD, 0)]
        args += [proj, proj, proj, h_g, h_v]
    return pl.pallas_call(
        _state_out_kernel,
        grid=(depth, s // ns),
        in_specs=in_specs,
        out_specs=[blk(s_conv), blk(s_pool), blk(s_ffn)],
        out_shape=[jax.ShapeDtypeStruct(a.shape, F32) for a in (s_conv, s_pool, s_ffn)],
        compiler_params=_cparams(2),
        name="state_out",
    )(*args)


def _pad_ff(v):
    pad = [(0, 0)] * (v.ndim - 1) + [(0, D_FF_PAD - D_FF)]
    return jnp.concatenate([jnp.pad(v[..., :D_FF], pad), jnp.pad(v[..., D_FF:], pad)], axis=-1)


def _unpad_ff(v):
    return jnp.concatenate([v[..., :D_FF], v[..., D_FF_PAD:D_FF_PAD + D_FF]], axis=-1)


def _prep_layer(big, b_gate, conv_w, conv_b, dt_bias, a_log, d_skip, norm_w, pool_scale,
                ln1_g, ln1_b, fconv_w, fconv_b, ln2_g, ln2_b):
    pad_h = lambda v: jnp.pad(v, (0, DT_PAD - N_HEADS)).reshape(1, DT_PAD)
    head_of_channel = jnp.arange(D_INNER) // HEAD_DIM
    head_expand = (jnp.arange(DT_PAD)[:, None] == head_of_channel[None, :]).astype(BF16)
    return dict(
        big, b_gate=b_gate.reshape(1, -1),
        conv_w=conv_w, conv_b=conv_b.reshape(1, -1),
        dt_bias=pad_h(dt_bias), a_log=pad_h(a_log),
        d_skip_x=jnp.repeat(d_skip, HEAD_DIM).reshape(1, -1),
        norm_w=norm_w.reshape(1, -1), head_expand=head_expand,
        pool_scale=pool_scale.reshape(1, -1),
        ln1_g=ln1_g.reshape(1, -1), ln1_b=ln1_b.reshape(1, -1),
        fconv_w=_pad_ff(fconv_w), fconv_b=_pad_ff(fconv_b).reshape(1, -1),
        ln2_g=ln2_g.reshape(1, -1), ln2_b=ln2_b.reshape(1, -1),
    )


def _raw_xbc(proj):
    return proj[:, OFF_XS:OFF_XS + CONV_DIM]


def _layer_prompt(x, x_bf, lw, layer, bsz, seq):
    tm_in = 1024
    proj, xbc_tail = _in_proj(x_bf, lw, layer, tm_in, seq // tm_in, conv=True)
    dt_raw = _dt_proj(x_bf, lw["w_in_t"], layer, 1024)
    y, new_ssm = _ssd_prompt(proj, dt_raw, lw, bsz, seq)
    y_b = _pool_prompt(proj, lw["w_pool"], layer, lw["pool_scale"], seq)
    mix = _branch_merge(y, lw["w_br"], layer, proj, y_b, 512, 512)
    x1, x1_bf = _proj_ln(mix, lw["w_out"], layer, x, lw["ln1_g"], lw["ln1_b"], 512)
    tm_up = 1024
    act, tail_g, tail_v = _ffn_up_prompt(x1_bf, lw, layer, seq, tm_up)
    x2, x2_bf = _down_ln(act, lw["w_down"], layer, x1, lw["ln2_g"], lw["ln2_b"], 1024)
    p3 = proj.reshape(bsz, seq, N_MAIN)
    new_pool = p3[:, seq - POOL_BUF:, OFF_POOL:OFF_POOL + D_POOL]
    tps_in = seq // tm_in
    new_conv = xbc_tail[tps_in - 1::tps_in, CONV_HALO - (CONV_K - 1):, OFF_XS:OFF_XS + CONV_DIM]
    tps = seq // tm_up
    last = slice(tps - 1, None, tps)
    tail = jnp.concatenate([tail_g[last, :, :D_FF], tail_v[last, :, :D_FF]], axis=-1)
    new_ffn = tail[:, FFN_HALO - (FFN_K - 1):, :]
    return x2, x2_bf, new_ssm, new_conv, new_pool, new_ffn


def _layer_sample(x, x_bf, ssm_all, layer, ssm_prev_out, s_conv, s_pool, s_ffn, lw):
    s = x.shape[0]
    proj, _ = _in_proj(x_bf, lw, layer, s, 1, conv=False)
    dt_raw = _dt_proj(x_bf, lw["w_in_t"], layer, s)
    xs, b_m, c_m, decht, xdtt = _ssd_sample_prep(proj, dt_raw, jnp.swapaxes(s_conv, 0, 1), lw)
    new_ssm, yt = _ssd_sample_step(ssm_all, layer, ssm_prev_out, decht, xdtt, b_m, c_m)
    y = _ssd_sample_finish(yt, xs, proj, lw)
    pooled = _pool_sample(proj, s_pool, layer)
    y_b = _pool_mm(pooled, lw["w_pool"], layer, lw["pool_scale"], s)
    mix = _branch_merge(y, lw["w_br"], layer, proj, y_b, s, 512)
    x1, x1_bf = _proj_ln(mix, lw["w_out"], layer, x, lw["ln1_g"], lw["ln1_b"], s)
    act, h_g, h_v = _ffn_up_sample(x1_bf, s_ffn, lw, layer)
    x2, x2_bf = _down_ln(act, lw["w_down"], layer, x1, lw["ln2_g"], lw["ln2_b"], s)
    return x2, x2_bf, new_ssm, (proj, h_g, h_v)


def kernel(x_prompt, x_sample, state_ssm, state_ssd_conv, state_pool, state_ffn_conv, w_in, b_gate, conv_w, conv_b, dt_bias, a_log, d_skip, ssd_norm_w, w_ssd_branch, w_pool, pool_scale, w_out, ln1_g, ln1_b, w_up, ffn_conv_w, ffn_conv_b, w_down, ln2_g, ln2_b):
    bsz, seq, _ = x_prompt.shape
    n_s = x_sample.shape[0]
    assert x_sample.shape[1] == 1 and seq % 1024 == 0
    xp = x_prompt.reshape(bsz * seq, D_MODEL)
    xs = x_sample.reshape(n_s, D_MODEL)
    xp_bf, xs_bf = xp.astype(BF16), xs.astype(BF16)
    outs_p, outs_s = [], []
    ssm_all = state_ssm.reshape(DEPTH, n_s, D_INNER, D_STATE)
    ssm_out = None
    big = dict(w_in_t=jnp.swapaxes(w_in, 1, 2), w_br=w_ssd_branch, w_pool=w_pool, w_out=w_out,
               w_up=w_up, w_down=w_down)
    for i in range(DEPTH):
        lw = _prep_layer(big, b_gate[i], conv_w[i], conv_b[i], dt_bias[i], a_log[i], d_skip[i],
                         ssd_norm_w[i], pool_scale[i], ln1_g[i], ln1_b[i], ffn_conv_w[i],
                         ffn_conv_b[i], ln2_g[i], ln2_b[i])
        xp, xp_bf, *op = _layer_prompt(xp, xp_bf, lw, i, bsz, seq)
        xs, xs_bf, ssm_out, new_rows = _layer_sample(xs, xs_bf, ssm_all, i, ssm_out, state_ssd_conv[i],
                                                     state_pool, state_ffn_conv, lw)
        outs_p.append(op)
        outs_s.append(new_rows)
    new_conv_s, new_pool_s, new_ffn_s = _state_out(state_ssd_conv, state_pool, state_ffn_conv, outs_s)
    stack = lambda outs, k: jnp.stack([o[k] for o in outs])
    return (xp.reshape(bsz, seq, D_MODEL), xs.reshape(n_s, 1, D_MODEL),
            stack(outs_p, 0), stack(outs_p, 1), stack(outs_p, 2), stack(outs_p, 3),
            ssm_out.reshape(state_ssm.shape), new_conv_s, new_pool_s, new_ffn_s)
```

```python
import functools

import jax
import jax.numpy as jnp
from jax import lax
from jax.experimental import pallas as pl
from jax.experimental.pallas import tpu as pltpu

F32 = jnp.float32
BF16 = jnp.bfloat16

D_MODEL = 2048
HEAD_DIM = 64
D_INNER = 2 * D_MODEL
N_HEADS = D_INNER // HEAD_DIM
N_GROUPS = 8
HEADS_PER_GROUP = N_HEADS // N_GROUPS
GROUP_W = D_INNER // N_GROUPS
D_STATE = 128
CONV_K = 4
BC_W = 2 * N_GROUPS * D_STATE
CONV_DIM = D_INNER + BC_W
CHUNK = 128
D_POOL = D_MODEL
POOL_WINDOWS = (2, 4, 8, 16)
POOL_GC = D_POOL // len(POOL_WINDOWS)
POOL_BUF = max(POOL_WINDOWS) - 1
D_FF = 5504
FFN_K = 3
DEPTH = 2
PAST_LEN = 16384
ALPHA = (2 * DEPTH) ** 0.25
LN_EPS = 1e-5
RMS_EPS = 1e-5

LANES = 128
SUBLANES = 8
D_FF_PAD = 5632
FFN_TN = 512
DT_PAD = LANES
OFF_Z = 0
OFF_XS = D_INNER
OFF_BC = 2 * D_INNER
OFF_POOL = 2 * D_INNER + BC_W
OFF_GATE = OFF_POOL + D_POOL
N_MAIN = OFF_GATE + 2 * D_MODEL
VMEM_LIMIT = 56 * 1024 * 1024
NEG_BIG = -1e30


def _cparams(n_axes):
    return pltpu.CompilerParams(dimension_semantics=("arbitrary",) * n_axes,
                                vmem_limit_bytes=VMEM_LIMIT)


def _sigmoid(x):
    return 1.0 / (1.0 + jnp.exp(-x))


def _silu(x):
    return x * _sigmoid(x)


def _softplus(x):
    return jnp.maximum(x, 0.0) + jnp.log(1.0 + jnp.exp(-jnp.abs(x)))


def _rows_above(e, shift):
    n, c = e.shape
    rot = pltpu.roll(e.reshape(n // SUBLANES, SUBLANES, c), shift, 1).reshape(n, c)
    sub = lax.broadcasted_iota(jnp.int32, (n - SUBLANES, c), 0) % SUBLANES
    return jnp.where(sub < shift, rot[0:n - SUBLANES, :], rot[SUBLANES:n, :])


def _shift_rows(ext, shift, rows):
    return ext[SUBLANES:SUBLANES + rows, :] if shift == 0 else _rows_above(ext, shift)


def _layer_norm(r, g, b):
    mu = jnp.mean(r, axis=-1, keepdims=True)
    d = r - mu
    var = jnp.mean(d * d, axis=-1, keepdims=True)
    return d * lax.rsqrt(var + LN_EPS) * g + b


def _dt_proj_kernel(x_ref, w_ref, o_ref):
    o_ref[...] = lax.dot_general(x_ref[...], w_ref[...].astype(BF16), (((1,), (1,)), ((), ())),
                                 preferred_element_type=F32)


def _dt_proj(x_bf, w_in_t, layer, tm):
    m = x_bf.shape[0]
    return pl.pallas_call(
        _dt_proj_kernel,
        grid=(m // tm,),
        in_specs=[pl.BlockSpec((tm, D_MODEL), lambda i: (i, 0)),
                  pl.BlockSpec((None, DT_PAD, D_MODEL), lambda i: (layer, DT_COL_BLOCK, 0))],
        out_specs=pl.BlockSpec((tm, DT_PAD), lambda i: (i, 0)),
        out_shape=jax.ShapeDtypeStruct((m, DT_PAD), F32),
        compiler_params=_cparams(1),
        name="dt_proj",
    )(x_bf, w_in_t)


IN_TN = 1024
J_XBC = OFF_XS // IN_TN
J_POOL = OFF_POOL // IN_TN
J_GATE = OFF_GATE // IN_TN
J_END = N_MAIN // IN_TN
CONV_HALO = SUBLANES
PG_SHIFT = N_HEADS
DT_COL_BLOCK = (2 * D_INNER + BC_W) // DT_PAD


def _in_proj_kernel(x_ref, wa_ref, wb_ref, cw_ref, cb_ref, bg_ref, o_ref, tail_ref, w_s, h_s,
                    *, tiles_per_seq, conv):
    j = pl.program_id(0)
    i = pl.program_id(1)
    tm = x_ref.shape[0]

    @pl.when((i == 0) & (j < J_POOL))
    def _():
        w_s[...] = wa_ref[...].astype(BF16)

    @pl.when((i == 0) & (j >= J_POOL))
    def _():
        w_s[...] = jnp.concatenate([wa_ref[PG_SHIFT:, :], wb_ref[:PG_SHIFT, :]], axis=0).astype(BF16)

    def mm():
        return lax.dot_general(x_ref[...], w_s[...], (((1,), (1,)), ((), ())),
                               preferred_element_type=F32)

    is_xbc = (j >= J_XBC) & (j < J_POOL)

    @pl.when(jnp.logical_not(is_xbc))
    def _():
        tail_ref[0] = jnp.zeros((CONV_HALO, IN_TN), F32)

    @pl.when(j < J_XBC)
    def _():
        o_ref[...] = _silu(mm())

    @pl.when(is_xbc)
    def _():
        if conv:
            @pl.when(i % tiles_per_seq == 0)
            def _():
                h_s[...] = jnp.zeros((CONV_HALO, IN_TN), F32)

            ext = jnp.concatenate([h_s[...], mm()], axis=0)
            acc = cb_ref[...]
            for k in range(CONV_K):
                acc = acc + _shift_rows(ext, CONV_K - 1 - k, tm) * cw_ref[k:k + 1, :]
            o_ref[...] = _silu(acc)
            tail = ext[tm:tm + CONV_HALO, :]
            h_s[...] = tail
            tail_ref[0] = tail
        else:
            o_ref[...] = mm()
            tail_ref[0] = jnp.zeros((CONV_HALO, IN_TN), F32)

    @pl.when((j >= J_POOL) & (j < J_GATE))
    def _():
        o_ref[...] = mm()

    @pl.when(j >= J_GATE)
    def _():
        o_ref[...] = _sigmoid(mm() + bg_ref[...])


def _in_proj(x_bf, lw, layer, tm, tiles_per_seq, conv):
    m = x_bf.shape[0]
    ni = m // tm
    n_conv_tiles = CONV_DIM // IN_TN
    clamp = lambda v, lo, hi: jnp.minimum(jnp.maximum(v, lo), hi)
    conv_tile = lambda j: clamp(j - J_XBC, 0, n_conv_tiles - 1)
    kern = functools.partial(_in_proj_kernel, tiles_per_seq=tiles_per_seq, conv=conv)
    return pl.pallas_call(
        kern,
        grid=(J_END, ni),
        in_specs=[pl.BlockSpec((tm, D_MODEL), lambda j, i: (i, 0)),
                  pl.BlockSpec((None, IN_TN, D_MODEL), lambda j, i: (layer, j, 0)),
                  pl.BlockSpec((None, IN_TN, D_MODEL), lambda j, i: (layer, jnp.maximum(j + 1, J_POOL), 0),
                               pipeline_mode=pl.Buffered(1)),
                  pl.BlockSpec((CONV_K, IN_TN), lambda j, i: (0, conv_tile(j))),
                  pl.BlockSpec((1, IN_TN), lambda j, i: (0, conv_tile(j))),
                  pl.BlockSpec((1, IN_TN), lambda j, i: (0, clamp(j - J_GATE, 0, J_END - J_GATE - 1)))],
        out_specs=[pl.BlockSpec((tm, IN_TN), lambda j, i: (i, j)),
                   pl.BlockSpec((1, CONV_HALO, IN_TN), lambda j, i: (i, 0, j))],
        out_shape=[jax.ShapeDtypeStruct((m, N_MAIN), F32),
                   jax.ShapeDtypeStruct((ni, CONV_HALO, N_MAIN), F32)],
        scratch_shapes=[pltpu.VMEM((IN_TN, D_MODEL), BF16),
                        pltpu.VMEM((CONV_HALO, IN_TN), F32)],
        compiler_params=_cparams(2),
        name="in_proj",
    )(x_bf, lw["w_in_t"], lw["w_in_t"], lw["conv_w"], lw["conv_b"], lw["b_gate"])


def _split3(v):
    hi = v.astype(BF16)
    r1 = v - hi.astype(F32)
    mid = r1.astype(BF16)
    lo = (r1 - mid.astype(F32)).astype(BF16)
    return hi, mid, lo


def _ssd_prompt_kernel(xs_ref, bc_ref, zs_ref, dt_ref, dtb_ref, alog_ref, dsk_ref, nw_ref,
                       y_ref, st_ref,
                       xs_s, bt_s, c_s, y_s, state_s, acol_s, arow_s, dtrow_s):
    c = pl.program_id(1)
    n_chunks = pl.num_programs(1)
    q = CHUNK

    @pl.when(c == 0)
    def _():
        state_s[...] = jnp.zeros(state_s.shape, F32)

    for g in range(N_GROUPS):
        xs_s[g] = xs_ref[:, g * GROUP_W:(g + 1) * GROUP_W]
        bt_s[g] = bc_ref[:, g * D_STATE:(g + 1) * D_STATE].T
        c_s[g] = bc_ref[:, (N_GROUPS + g) * D_STATE:(N_GROUPS + g + 1) * D_STATE].astype(BF16)

    dt = _softplus(dt_ref[...] + dtb_ref[...])
    a_neg = -jnp.exp(alog_ref[...])
    d_a = dt * a_neg
    row = lax.broadcasted_iota(jnp.int32, (q, q), 0)
    col = lax.broadcasted_iota(jnp.int32, (q, q), 1)
    causal = row >= col
    tril = jnp.where(causal, 1.0, 0.0).astype(BF16)
    hi, mid, lo3 = _split3(d_a)
    a_cum = (jnp.dot(tril, hi, preferred_element_type=F32)
             + jnp.dot(tril, mid, preferred_element_type=F32)
             + jnp.dot(tril, lo3, preferred_element_type=F32))
    arow_s[...] = a_cum.T
    dtrow_s[...] = dt.T
    for g in range(N_GROUPS):
        sh = (LANES - HEADS_PER_GROUP * g) % LANES
        acol_s[g] = a_cum if sh == 0 else pltpu.roll(a_cum, sh, 1)

    lane = lax.broadcasted_iota(jnp.int32, (q, LANES), 1)
    lo_half = lane < HEAD_DIM

    def group_body(g, carry):
        acol = acol_s[g]
        c_g = c_s[g]
        bt_g = bt_s[g]
        cb = jnp.dot(c_g, bt_g.astype(BF16), preferred_element_type=F32)
        y_off_g = jnp.dot(c_g, state_s[g].astype(BF16), preferred_element_type=F32)
        for k in range(HEADS_PER_GROUP // 2):
            l_parts, b_parts, a_b = [], [], []
            for e in range(2):
                hh = 2 * k + e
                head = g * HEADS_PER_GROUP + hh
                a_col = jnp.broadcast_to(acol[:, hh:hh + 1], (q, q))
                a_row = arow_s[pl.ds(head, 1), :]
                dt_row = dtrow_s[pl.ds(head, 1), :]
                seg = jnp.where(causal, a_col - a_row, NEG_BIG)
                l_parts.append((cb * jnp.exp(seg) * dt_row).astype(BF16))
                w_row = dt_row * jnp.exp(a_col[q - 1:q, :] - a_row)
                b_parts.append((bt_g * w_row).astype(BF16))
                a_b.append(a_col)
            lhs = jnp.concatenate([jnp.concatenate(l_parts, axis=1),
                                   jnp.concatenate(b_parts, axis=1)], axis=0)
            cols = slice(k * LANES, (k + 1) * LANES)
            xs_bf = xs_s[g, :, cols].astype(BF16)
            zero = jnp.zeros_like(xs_bf)
            rhs = jnp.concatenate([jnp.where(lo_half, xs_bf, zero),
                                   jnp.where(lo_half, zero, xs_bf)], axis=0)
            res = jnp.dot(lhs, rhs, preferred_element_type=F32)
            a_pair = jnp.where(lo_half, a_b[0], a_b[1])
            y_s[g, :, cols] = res[0:q] + y_off_g[:, cols] * jnp.exp(a_pair)
            cdec = jnp.exp(a_pair[q - 1:q, :])
            state_s[g, :, cols] = state_s[g, :, cols] * cdec + res[q:2 * q]
        return carry

    lax.fori_loop(0, N_GROUPS, group_body, 0)

    for g in range(N_GROUPS):
        cols = slice(g * GROUP_W, (g + 1) * GROUP_W)
        yv = y_s[g] + dsk_ref[:, cols] * xs_s[g]
        v = yv * zs_ref[:, cols]
        ms = jnp.mean(v * v, axis=-1, keepdims=True)
        y_ref[:, cols] = (v * lax.rsqrt(ms + RMS_EPS) * nw_ref[:, cols]).astype(y_ref.dtype)

    @pl.when(c == n_chunks - 1)
    def _():
        for g in range(N_GROUPS):
            st_ref[0, g] = state_s[g].T


def _ssd_prompt(proj, dt_raw, lw, bsz, seq):
    n_chunks = seq // CHUNK
    rows = lambda b, c: b * n_chunks + c
    small = lambda shape: pl.BlockSpec(shape, lambda b, c: (0, 0))
    y, st = pl.pallas_call(
        _ssd_prompt_kernel,
        grid=(bsz, n_chunks),
        in_specs=[
            pl.BlockSpec((CHUNK, D_INNER), lambda b, c: (rows(b, c), OFF_XS // D_INNER)),
            pl.BlockSpec((CHUNK, BC_W), lambda b, c: (rows(b, c), OFF_BC // BC_W)),
            pl.BlockSpec((CHUNK, D_INNER), lambda b, c: (rows(b, c), OFF_Z // D_INNER)),
            pl.BlockSpec((CHUNK, DT_PAD), lambda b, c: (rows(b, c), 0)),
            small((1, DT_PAD)), small((1, DT_PAD)), small((1, D_INNER)), small((1, D_INNER)),
        ],
        out_specs=[
            pl.BlockSpec((CHUNK, D_INNER), lambda b, c: (rows(b, c), 0)),
            pl.BlockSpec((1, N_GROUPS, GROUP_W, D_STATE), lambda b, c: (b, 0, 0, 0)),
        ],
        out_shape=[jax.ShapeDtypeStruct((bsz * seq, D_INNER), BF16),
                   jax.ShapeDtypeStruct((bsz, N_GROUPS, GROUP_W, D_STATE), F32)],
        scratch_shapes=[
            pltpu.VMEM((N_GROUPS, CHUNK, GROUP_W), F32),
            pltpu.VMEM((N_GROUPS, D_STATE, CHUNK), F32),
            pltpu.VMEM((N_GROUPS, CHUNK, D_STATE), BF16),
            pltpu.VMEM((N_GROUPS, CHUNK, GROUP_W), F32),
            pltpu.VMEM((N_GROUPS, D_STATE, GROUP_W), F32),
            pltpu.VMEM((N_GROUPS, CHUNK, LANES), F32),
            pltpu.VMEM((LANES, CHUNK), F32),
            pltpu.VMEM((LANES, CHUNK), F32),
        ],
        compiler_params=_cparams(2),
        name="ssd_prompt",
    )(proj, proj, proj, dt_raw, lw["dt_bias"], lw["a_log"], lw["d_skip_x"], lw["norm_w"])
    return y, st.reshape(bsz, N_HEADS, HEAD_DIM, D_STATE)


def _ssd_sample_prep_kernel(xs_ref, bc_ref, dt_ref, cst_ref, cw_ref, cb_ref, dtb_ref, alog_ref,
                            ex_ref, xs_o, b_o, c_o, decht_o, xdtt_o):
    def conv(u, lo, hi):
        acc = cb_ref[:, lo:hi]
        for j in range(CONV_K - 1):
            acc = acc + cst_ref[j][:, lo:hi] * cw_ref[j:j + 1, lo:hi]
        acc = acc + u * cw_ref[CONV_K - 1:CONV_K, lo:hi]
        return _silu(acc)

    xs = conv(xs_ref[...], 0, D_INNER)
    bc = conv(bc_ref[...], D_INNER, CONV_DIM)
    xs_o[...] = xs
    b_o[...] = bc[:, 0:N_GROUPS * D_STATE]
    c_o[...] = bc[:, N_GROUPS * D_STATE:BC_W]
    dt = _softplus(dt_ref[...] + dtb_ref[...])
    d_a = dt * (-jnp.exp(alog_ref[...]))
    ex = ex_ref[...]

    def expand(v):
        hi, mid, lo3 = _split3(v)
        return (jnp.dot(hi, ex, preferred_element_type=F32)
                + jnp.dot(mid, ex, preferred_element_type=F32)
                + jnp.dot(lo3, ex, preferred_element_type=F32))

    xdt = expand(dt) * xs
    decht_o[...] = jnp.exp(d_a).T
    xdtt_o[...] = xdt.T.astype(BF16)


def _ssd_sample_prep(proj, dt_raw, conv_state_t, lw):
    s = proj.shape[0]
    full = lambda shape: pl.BlockSpec(shape, lambda i: (0,) * len(shape))
    return pl.pallas_call(
        _ssd_sample_prep_kernel,
        grid=(1,),
        in_specs=[
            pl.BlockSpec((s, D_INNER), lambda i: (0, OFF_XS // D_INNER)),
            pl.BlockSpec((s, BC_W), lambda i: (0, OFF_BC // BC_W)),
            full((s, DT_PAD)), full((CONV_K - 1, s, CONV_DIM)),
            full((CONV_K, CONV_DIM)), full((1, CONV_DIM)), full((1, DT_PAD)), full((1, DT_PAD)),
            full((DT_PAD, D_INNER)),
        ],
        out_specs=[full((s, D_INNER)), full((s, N_GROUPS * D_STATE)), full((s, N_GROUPS * D_STATE)),
                   full((DT_PAD, s)), full((D_INNER, s))],
        out_shape=[jax.ShapeDtypeStruct((s, D_INNER), F32),
                   jax.ShapeDtypeStruct((s, N_GROUPS * D_STATE), F32),
                   jax.ShapeDtypeStruct((s, N_GROUPS * D_STATE), F32),
                   jax.ShapeDtypeStruct((DT_PAD, s), F32),
                   jax.ShapeDtypeStruct((D_INNER, s), BF16)],
        compiler_params=_cparams(1),
        name="ssd_sample_prep",
    )(proj, proj, dt_raw, conv_state_t, lw["conv_w"], lw["conv_b"], lw["dt_bias"], lw["a_log"],
      lw["head_expand"])


def _ssd_sample_step_kernel(st_ref, decht_ref, xdtt_ref, b_ref, c_ref, *rest, fill_other_layers):
    st_o, yt_o = rest[-2:]
    s = pl.program_id(0)
    n_s = xdtt_ref.shape[1]
    onehot = lax.broadcasted_iota(jnp.int32, (GROUP_W, n_s), 1) == s
    onehot_h = lax.broadcasted_iota(jnp.int32, (DT_PAD, n_s), 1) == s
    pick = jnp.where(lax.broadcasted_iota(jnp.int32, (n_s, D_STATE), 0) == s, 1.0, 0.0).astype(BF16)

    @pl.when(s == 0)
    def _():
        yt_o[...] = jnp.zeros(yt_o.shape, F32)

    if fill_other_layers:
        st_o[1:] = jnp.zeros((st_o.shape[0] - 1,) + st_o.shape[1:], F32)

    dech = jnp.sum(jnp.where(onehot_h, decht_ref[...], 0.0), axis=-1, keepdims=True)
    b_all = b_ref[pl.ds(s, 1), :]
    c_all = c_ref[pl.ds(s, 1), :]
    for g in range(N_GROUPS):
        rows = slice(g * GROUP_W, (g + 1) * GROUP_W)
        cols = slice(g * D_STATE, (g + 1) * D_STATE)
        xdt = jnp.dot(xdtt_ref[rows, :], pick, preferred_element_type=F32)
        b_row = b_all[:, cols]
        c_row = c_all[:, cols]
        upd = xdt * b_row
        parts = []
        for hh in range(HEADS_PER_GROUP):
            h = g * HEADS_PER_GROUP + hh
            r_h = slice(h * HEAD_DIM, (h + 1) * HEAD_DIM)
            parts.append(st_ref[0, 0, r_h, :] * dech[h:h + 1, :]
                         + upd[hh * HEAD_DIM:(hh + 1) * HEAD_DIM, :])
        st_new = jnp.concatenate(parts, axis=0)
        st_o[0, 0, rows, :] = st_new
        y_col = jnp.sum(st_new * c_row, axis=-1, keepdims=True)
        yt_o[rows, :] = yt_o[rows, :] + jnp.where(onehot, y_col, 0.0)


def _ssd_sample_step(state_all, layer, prev_out, decht, xdtt, b_m, c_m):
    depth = state_all.shape[0]
    s = xdtt.shape[1]
    full = lambda shape: pl.BlockSpec(shape, lambda i: (0,) * len(shape))
    in_specs = [pl.BlockSpec((1, 1, D_INNER, D_STATE), lambda i: (layer, i, 0, 0)),
                full((DT_PAD, s)), full((D_INNER, s)),
                full((s, N_GROUPS * D_STATE)), full((s, N_GROUPS * D_STATE))]
    args = [state_all, decht, xdtt, b_m, c_m]
    if prev_out is None:
        assert layer == 0
        aliases = {}
        st_out = pl.BlockSpec((depth, 1, D_INNER, D_STATE), lambda i: (0, i, 0, 0))
    else:
        in_specs.append(pl.BlockSpec(memory_space=pl.ANY))
        args.append(prev_out)
        aliases = {len(args) - 1: 0}
        st_out = pl.BlockSpec((1, 1, D_INNER, D_STATE), lambda i: (layer, i, 0, 0))
    return pl.pallas_call(
        functools.partial(_ssd_sample_step_kernel, fill_other_layers=prev_out is None),
        grid=(s,),
        in_specs=in_specs,
        out_specs=[st_out, full((D_INNER, s))],
        out_shape=[jax.ShapeDtypeStruct(state_all.shape, F32),
                   jax.ShapeDtypeStruct((D_INNER, s), F32)],
        input_output_aliases=aliases,
        compiler_params=_cparams(1),
        name="ssd_sample_step",
    )(*args)


def _ssd_sample_finish_kernel(yt_ref, xs_ref, zs_ref, dsk_ref, nw_ref, y_ref):
    y = yt_ref[...].T
    for g in range(N_GROUPS):
        cols = slice(g * GROUP_W, (g + 1) * GROUP_W)
        yv = y[:, cols] + dsk_ref[:, cols] * xs_ref[:, cols]
        v = yv * zs_ref[:, cols]
        ms = jnp.mean(v * v, axis=-1, keepdims=True)
        y_ref[:, cols] = (v * lax.rsqrt(ms + RMS_EPS) * nw_ref[:, cols]).astype(y_ref.dtype)


def _ssd_sample_finish(yt, xs, proj, lw):
    s = xs.shape[0]
    full = lambda shape: pl.BlockSpec(shape, lambda i: (0,) * len(shape))
    return pl.pallas_call(
        _ssd_sample_finish_kernel,
        grid=(1,),
        in_specs=[full((D_INNER, s)), full((s, D_INNER)),
                  pl.BlockSpec((s, D_INNER), lambda i: (0, OFF_Z // D_INNER)),
                  full((1, D_INNER)), full((1, D_INNER))],
        out_specs=full((s, D_INNER)),
        out_shape=jax.ShapeDtypeStruct((s, D_INNER), BF16),
        compiler_params=_cparams(1),
        name="ssd_sample_finish",
    )(yt, xs, proj, lw["d_skip_x"], lw["norm_w"])


POOL_TP = 1024
POOL_HALO = 32


def _pool_prompt_kernel(u_ref, w_ref, sc_ref, o_ref, ext_s, w_s, *, tiles_per_seq):
    g = pl.program_id(0)
    i = pl.program_id(1)
    tp = POOL_TP

    @pl.when(i == 0)
    def _():
        w_s[...] = w_ref[...].astype(BF16)

    @pl.when(i % tiles_per_seq == 0)
    def _():
        ext_s[...] = jnp.zeros((POOL_HALO, POOL_GC), F32)

    u = u_ref[...]
    ext = jnp.concatenate([ext_s[...], u], axis=0)
    pos = (i % tiles_per_seq) * tp + lax.broadcasted_iota(jnp.int32, (tp, 1), 0)
    for gi, win in enumerate(POOL_WINDOWS):
        @pl.when(g == gi)
        def _():
            tot, width = ext, 1
            while width < win:
                if width < SUBLANES:
                    tot = tot[SUBLANES:, :] + _rows_above(tot, width)
                else:
                    tot = tot[width:, :] + tot[:-width, :]
                width *= 2
            tot = tot[tot.shape[0] - tp:, :]
            cnt = jnp.minimum(win, pos + 1).astype(F32)
            pooled = (tot / cnt - u).astype(BF16)
            o_ref[...] = jnp.dot(pooled, w_s[...], preferred_element_type=F32) * sc_ref[...]
    ext_s[...] = ext[tp:tp + POOL_HALO, :]


def _pool_prompt(proj, w_pool, layer, scale, seq):
    m = proj.shape[0]
    ng = len(POOL_WINDOWS)
    kern = functools.partial(_pool_prompt_kernel, tiles_per_seq=seq // POOL_TP)
    return pl.pallas_call(
        kern,
        grid=(ng, m // POOL_TP),
        in_specs=[pl.BlockSpec((POOL_TP, POOL_GC), lambda g, i: (i, OFF_POOL // POOL_GC + g)),
                  pl.BlockSpec((None, None, POOL_GC, POOL_GC), lambda g, i: (layer, g, 0, 0)),
                  pl.BlockSpec((1, POOL_GC), lambda g, i: (0, g))],
        out_specs=pl.BlockSpec((POOL_TP, POOL_GC), lambda g, i: (i, g)),
        out_shape=jax.ShapeDtypeStruct((m, D_POOL), F32),
        scratch_shapes=[pltpu.VMEM((POOL_HALO, POOL_GC), F32),
                        pltpu.VMEM((POOL_GC, POOL_GC), BF16)],
        compiler_params=_cparams(2),
        name="pool_prompt",
    )(proj, w_pool, scale)


def _pool_sample_kernel(u_ref, buf_ref, o_ref):
    g = pl.program_id(0)
    for gi, win in enumerate(POOL_WINDOWS):
        @pl.when(g == gi)
        def _():
            u = u_ref[...]
            tot = u
            for k in range(1, win):
                tot = tot + buf_ref[:, POOL_BUF - k, :]
            cnt = float(min(win, PAST_LEN + 1))
            o_ref[...] = (tot / cnt - u).astype(o_ref.dtype)


def _pool_sample(proj, pool_state, layer):
    s = proj.shape[0]
    return pl.pallas_call(
        _pool_sample_kernel,
        grid=(len(POOL_WINDOWS),),
        in_specs=[pl.BlockSpec((s, POOL_GC), lambda g: (0, OFF_POOL // POOL_GC + g)),
                  pl.BlockSpec((None, s, POOL_BUF, POOL_GC), lambda g: (layer, 0, 0, g))],
        out_specs=pl.BlockSpec((s, POOL_GC), lambda g: (0, g)),
        out_shape=jax.ShapeDtypeStruct((s, D_POOL), BF16),
        compiler_params=_cparams(1),
        name="pool_sample",
    )(proj, pool_state)


def _pool_mm_kernel(p_ref, w_ref, sc_ref, o_ref, w_s):
    @pl.when(pl.program_id(1) == 0)
    def _():
        w_s[...] = w_ref[...].astype(BF16)

    acc = jnp.dot(p_ref[...], w_s[...], preferred_element_type=F32)
    o_ref[...] = acc * sc_ref[...]


def _pool_mm(pooled, w_pool, layer, scale, tm):
    m = pooled.shape[0]
    ng = len(POOL_WINDOWS)
    return pl.pallas_call(
        _pool_mm_kernel,
        grid=(ng, m // tm),
        in_specs=[pl.BlockSpec((tm, POOL_GC), lambda g, i: (i, g)),
                  pl.BlockSpec((None, None, POOL_GC, POOL_GC), lambda g, i: (layer, g, 0, 0)),
                  pl.BlockSpec((1, POOL_GC), lambda g, i: (0, g))],
        out_specs=pl.BlockSpec((tm, POOL_GC), lambda g, i: (i, g)),
        out_shape=jax.ShapeDtypeStruct((m, D_POOL), F32),
        scratch_shapes=[pltpu.VMEM((POOL_GC, POOL_GC), BF16)],
        compiler_params=_cparams(2),
        name="pool_mm",
    )(pooled, w_pool, scale)


def _branch_merge_kernel(y_ref, w_ref, ga_ref, gb_ref, yb_ref, o_ref, w_s):
    @pl.when(pl.program_id(1) == 0)
    def _():
        w_s[...] = w_ref[...].astype(BF16)

    y_a = jnp.dot(y_ref[...], w_s[...], preferred_element_type=F32)
    o_ref[...] = (ga_ref[...] * y_a + gb_ref[...] * yb_ref[...]).astype(o_ref.dtype)


def _branch_merge(y, w_br, layer, proj, y_b, tm, tn):
    m = y.shape[0]
    nj = D_MODEL // tn
    ga0 = OFF_GATE // tn
    return pl.pallas_call(
        _branch_merge_kernel,
        grid=(nj, m // tm),
        in_specs=[pl.BlockSpec((tm, D_INNER), lambda j, i: (i, 0)),
                  pl.BlockSpec((None, D_INNER, tn), lambda j, i: (layer, 0, j),
                               pipeline_mode=pl.Buffered(1)),
                  pl.BlockSpec((tm, tn), lambda j, i: (i, ga0 + j)),
                  pl.BlockSpec((tm, tn), lambda j, i: (i, ga0 + nj + j)),
                  pl.BlockSpec((tm, tn), lambda j, i: (i, j))],
        out_specs=pl.BlockSpec((tm, tn), lambda j, i: (i, j)),
        out_shape=jax.ShapeDtypeStruct((m, D_MODEL), BF16),
        scratch_shapes=[pltpu.VMEM((D_INNER, tn), BF16)],
        compiler_params=_cparams(2),
        name="branch_merge",
    )(y, w_br, proj, proj, y_b)


def _proj_ln_kernel(m_ref, w_ref, x_ref, g_ref, b_ref, o_ref, obf_ref, w_s):
    @pl.when(pl.program_id(0) == 0)
    def _():
        w_s[...] = w_ref[...].astype(BF16)

    acc = jnp.dot(m_ref[...], w_s[...], preferred_element_type=F32)
    y = _layer_norm(ALPHA * x_ref[...] + acc, g_ref[...], b_ref[...])
    o_ref[...] = y
    obf_ref[...] = y.astype(BF16)


def _proj_ln(mix, w, layer, x, g, b, tm):
    m, k = mix.shape
    return pl.pallas_call(
        _proj_ln_kernel,
        grid=(m // tm,),
        in_specs=[pl.BlockSpec((tm, k), lambda i: (i, 0)),
                  pl.BlockSpec((None, k, D_MODEL), lambda i: (layer, 0, 0),
                               pipeline_mode=pl.Buffered(1)),
                  pl.BlockSpec((tm, D_MODEL), lambda i: (i, 0)),
                  pl.BlockSpec((1, D_MODEL), lambda i: (0, 0)),
                  pl.BlockSpec((1, D_MODEL), lambda i: (0, 0))],
        out_specs=[pl.BlockSpec((tm, D_MODEL), lambda i: (i, 0)),
                   pl.BlockSpec((tm, D_MODEL), lambda i: (i, 0))],
        out_shape=[jax.ShapeDtypeStruct((m, D_MODEL), F32),
                   jax.ShapeDtypeStruct((m, D_MODEL), BF16)],
        scratch_shapes=[pltpu.VMEM((k, D_MODEL), BF16)],
        compiler_params=_cparams(1),
        name="proj_ln",
    )(mix, w, x, g, b)


FFN_HALO = SUBLANES
FFN_VAL_BLK = D_FF // FFN_TN
FFN_VAL_SHIFT = D_FF % FFN_TN
assert FFN_VAL_SHIFT % LANES == 0


def _ffn_weight_tiles(j, wg_ref, wva_ref, wvb_ref, wg_s, wv_s):
    col = j * FFN_TN + lax.broadcasted_iota(jnp.int32, (1, FFN_TN), 1)
    valid = col < D_FF
    wg_s[...] = jnp.where(valid, wg_ref[...], 0.0).astype(BF16)
    wv = jnp.concatenate([wva_ref[:, FFN_VAL_SHIFT:], wvb_ref[:, :FFN_VAL_SHIFT]], axis=1)
    wv_s[...] = jnp.where(valid, wv, 0.0).astype(BF16)


def _ffn_w_specs(layer, index_of):
    blk = (None, D_MODEL, FFN_TN)
    return [pl.BlockSpec(blk, index_of(lambda j: (layer, 0, j))),
            pl.BlockSpec(blk, index_of(lambda j: (layer, 0, FFN_VAL_BLK + j))),
            pl.BlockSpec(blk, index_of(lambda j: (layer, 0, FFN_VAL_BLK + j + 1)))]


def _ffn_conv_gate(ext_g, ext_v, cw_g, cw_v, cb_g, cb_v, tm):
    def conv(ext, cw, cb):
        acc = cb[...]
        for j in range(FFN_K):
            acc = acc + _shift_rows(ext, FFN_K - 1 - j, tm) * cw[j:j + 1, :]
        return acc
    return _silu(conv(ext_g, cw_g, cb_g)) * conv(ext_v, cw_v, cb_v)


def _ffn_up_prompt_kernel(x_ref, wg_ref, wva_ref, wvb_ref, cwg_ref, cwv_ref, cbg_ref, cbv_ref,
                          a_ref, tg_ref, tv_ref, wg_s, wv_s, hg_s, hv_s, *, tiles_per_seq):
    j = pl.program_id(0)
    i = pl.program_id(1)
    tm = x_ref.shape[0]

    @pl.when(i == 0)
    def _():
        _ffn_weight_tiles(j, wg_ref, wva_ref, wvb_ref, wg_s, wv_s)

    @pl.when(i % tiles_per_seq == 0)
    def _():
        hg_s[...] = jnp.zeros((FFN_HALO, FFN_TN), F32)
        hv_s[...] = jnp.zeros((FFN_HALO, FFN_TN), F32)

    x = x_ref[...]
    ext_g = jnp.concatenate([hg_s[...], jnp.dot(x, wg_s[...], preferred_element_type=F32)], axis=0)
    ext_v = jnp.concatenate([hv_s[...], jnp.dot(x, wv_s[...], preferred_element_type=F32)], axis=0)
    a_ref[...] = _ffn_conv_gate(ext_g, ext_v, cwg_ref, cwv_ref, cbg_ref, cbv_ref, tm).astype(a_ref.dtype)
    tail_g = ext_g[tm:tm + FFN_HALO, :]
    tail_v = ext_v[tm:tm + FFN_HALO, :]
    hg_s[...] = tail_g
    hv_s[...] = tail_v
    tg_ref[0] = tail_g
    tv_ref[0] = tail_v


def _ffn_up_prompt(x_bf, lw, layer, seq, tm):
    m = x_bf.shape[0]
    nj = D_FF_PAD // FFN_TN
    ni = m // tm
    kern = functools.partial(_ffn_up_prompt_kernel, tiles_per_seq=seq // tm)
    cspec_g = lambda r: pl.BlockSpec((r, FFN_TN), lambda j, i: (0, j))
    cspec_v = lambda r: pl.BlockSpec((r, FFN_TN), lambda j, i: (0, nj + j))
    w_specs = _ffn_w_specs(layer, lambda f: (lambda j, i: f(j)))
    return pl.pallas_call(
        kern,
        grid=(nj, ni),
        in_specs=[pl.BlockSpec((tm, D_MODEL), lambda j, i: (i, 0)), *w_specs,
                  cspec_g(FFN_K), cspec_v(FFN_K), cspec_g(1), cspec_v(1)],
        out_specs=[pl.BlockSpec((tm, FFN_TN), lambda j, i: (i, j)),
                   pl.BlockSpec((1, FFN_HALO, FFN_TN), lambda j, i: (i, 0, j)),
                   pl.BlockSpec((1, FFN_HALO, FFN_TN), lambda j, i: (i, 0, j))],
        out_shape=[jax.ShapeDtypeStruct((m, D_FF_PAD), BF16),
                   jax.ShapeDtypeStruct((ni, FFN_HALO, D_FF_PAD), F32),
                   jax.ShapeDtypeStruct((ni, FFN_HALO, D_FF_PAD), F32)],
        scratch_shapes=[pltpu.VMEM((D_MODEL, FFN_TN), BF16),
                        pltpu.VMEM((D_MODEL, FFN_TN), BF16),
                        pltpu.VMEM((FFN_HALO, FFN_TN), F32),
                        pltpu.VMEM((FFN_HALO, FFN_TN), F32)],
        compiler_params=_cparams(2),
        name="ffn_up_prompt",
    )(x_bf, lw["w_up"], lw["w_up"], lw["w_up"],
      lw["fconv_w"], lw["fconv_w"], lw["fconv_b"], lw["fconv_b"])


def _ffn_up_sample_kernel(x_ref, wg_ref, wva_ref, wvb_ref, sg_ref, sva_ref, svb_ref, cwg_ref, cwv_ref,
                          cbg_ref, cbv_ref, a_ref, hg_ref, hv_ref, wg_s, wv_s):
    j = pl.program_id(0)
    _ffn_weight_tiles(j, wg_ref, wva_ref, wvb_ref, wg_s, wv_s)
    x = x_ref[...]
    hg = jnp.dot(x, wg_s[...], preferred_element_type=F32)
    hv = jnp.dot(x, wv_s[...], preferred_element_type=F32)
    hg_ref[...] = hg
    hv_ref[...] = hv
    valid = j * FFN_TN + lax.broadcasted_iota(jnp.int32, (1, FFN_TN), 1) < D_FF

    def conv(h, rows, cw, cb):
        acc = cb[...]
        for k in range(FFN_K - 1):
            acc = acc + jnp.where(valid, rows(k), 0.0) * cw[k:k + 1, :]
        return acc + h * cw[FFN_K - 1:FFN_K, :]

    gate_rows = lambda k: sg_ref[:, k, :]
    value_rows = lambda k: jnp.concatenate([sva_ref[:, k, FFN_VAL_SHIFT:],
                                            svb_ref[:, k, :FFN_VAL_SHIFT]], axis=1)
    a_ref[...] = (_silu(conv(hg, gate_rows, cwg_ref, cbg_ref))
                  * conv(hv, value_rows, cwv_ref, cbv_ref)).astype(a_ref.dtype)


def _ffn_up_sample(x_bf, ffn_state, lw, layer):
    s = x_bf.shape[0]
    nj = D_FF_PAD // FFN_TN
    g_blk = lambda r: pl.BlockSpec((r, FFN_TN), lambda j: (0, j))
    v_blk = lambda r: pl.BlockSpec((r, FFN_TN), lambda j: (0, nj + j))
    w_specs = _ffn_w_specs(layer, lambda f: f)
    st_blk = (None, s, FFN_K - 1, FFN_TN)
    return pl.pallas_call(
        _ffn_up_sample_kernel,
        grid=(nj,),
        in_specs=[pl.BlockSpec((s, D_MODEL), lambda j: (0, 0)), *w_specs,
                  pl.BlockSpec(st_blk, lambda j: (layer, 0, 0, j)),
                  pl.BlockSpec(st_blk, lambda j: (layer, 0, 0, FFN_VAL_BLK + j)),
                  pl.BlockSpec(st_blk, lambda j: (layer, 0, 0, FFN_VAL_BLK + j + 1)),
                  g_blk(FFN_K), v_blk(FFN_K), g_blk(1), v_blk(1)],
        out_specs=[pl.BlockSpec((s, FFN_TN), lambda j: (0, j)),
                   pl.BlockSpec((s, FFN_TN), lambda j: (0, j)),
                   pl.BlockSpec((s, FFN_TN), lambda j: (0, j))],
        out_shape=[jax.ShapeDtypeStruct((s, D_FF_PAD), BF16),
                   jax.ShapeDtypeStruct((s, D_FF_PAD), F32),
                   jax.ShapeDtypeStruct((s, D_FF_PAD), F32)],
        scratch_shapes=[pltpu.VMEM((D_MODEL, FFN_TN), BF16),
                        pltpu.VMEM((D_MODEL, FFN_TN), BF16)],
        compiler_params=_cparams(1),
        name="ffn_up_sample",
    )(x_bf, lw["w_up"], lw["w_up"], lw["w_up"], ffn_state, ffn_state, ffn_state,
      lw["fconv_w"], lw["fconv_w"], lw["fconv_b"], lw["fconv_b"])


DOWN_TK = 512


def _down_ln_kernel(a_ref, w_ref, x_ref, g_ref, b_ref, o_ref, obf_ref):
    k = pl.program_id(1)

    @pl.when(k == 0)
    def _():
        o_ref[...] = ALPHA * x_ref[...]

    row = k * DOWN_TK + lax.broadcasted_iota(jnp.int32, (DOWN_TK, 1), 0)
    w = jnp.where(row < D_FF, w_ref[...], 0.0).astype(BF16)
    o_ref[...] += jnp.dot(a_ref[...], w, preferred_element_type=F32)

    @pl.when(k == pl.num_programs(1) - 1)
    def _():
        y = _layer_norm(o_ref[...], g_ref[...], b_ref[...])
        o_ref[...] = y
        obf_ref[...] = y.astype(BF16)


def _down_ln(a, w, layer, x, g, b, tm):
    m = a.shape[0]
    nk = D_FF_PAD // DOWN_TK
    return pl.pallas_call(
        _down_ln_kernel,
        grid=(m // tm, nk),
        in_specs=[pl.BlockSpec((tm, DOWN_TK), lambda i, k: (i, k)),
                  pl.BlockSpec((None, DOWN_TK, D_MODEL), lambda i, k: (layer, k, 0)),
                  pl.BlockSpec((tm, D_MODEL), lambda i, k: (i, 0), pipeline_mode=pl.Buffered(1)),
                  pl.BlockSpec((1, D_MODEL), lambda i, k: (0, 0)),
                  pl.BlockSpec((1, D_MODEL), lambda i, k: (0, 0))],
        out_specs=[pl.BlockSpec((tm, D_MODEL), lambda i, k: (i, 0)),
                   pl.BlockSpec((tm, D_MODEL), lambda i, k: (i, 0))],
        out_shape=[jax.ShapeDtypeStruct((m, D_MODEL), F32),
                   jax.ShapeDtypeStruct((m, D_MODEL), BF16)],
        compiler_params=_cparams(2),
        name="down_ln",
    )(a, w, x, g, b)


STATE_NS = 8


def _state_out_kernel(sc_ref, sp_ref, sf_ref, *rest):
    oc_ref, op_ref, of_ref = rest[-3:]
    new_rows = rest[:-3]
    layer = pl.program_id(0)
    oc_ref[0, :, 0:CONV_K - 2, :] = sc_ref[0, :, 1:CONV_K - 1, :]
    op_ref[0, :, 0:POOL_BUF - 1, :] = sp_ref[0, :, 1:POOL_BUF, :]
    of_ref[0, :, 0:FFN_K - 2, :] = sf_ref[0, :, 1:FFN_K - 1, :]
    for li in range(len(new_rows) // 5):
        xs, bc, po, hg, hv = new_rows[5 * li:5 * li + 5]

        @pl.when(layer == li)
        def _():
            oc_ref[0, :, CONV_K - 2, 0:D_INNER] = xs[...]
            oc_ref[0, :, CONV_K - 2, D_INNER:CONV_DIM] = bc[...]
            op_ref[0, :, POOL_BUF - 1, :] = po[...]
            of_ref[0, :, FFN_K - 2, 0:D_FF] = hg[:, 0:D_FF]
            of_ref[0, :, FFN_K - 2, D_FF:2 * D_FF] = hv[:, 0:D_FF]


def _state_out(s_conv, s_pool, s_ffn, per_layer):
    depth, s = s_conv.shape[:2]
    ns = STATE_NS
    blk = lambda a: pl.BlockSpec((1, ns) + a.shape[2:], lambda l, i: (l, i, 0, 0))
    rows = lambda w, cb: pl.BlockSpec((ns, w), lambda l, i: (i, cb))
    in_specs = [blk(s_conv), blk(s_pool), blk(s_ffn)]
    args = [s_conv, s_pool, s_ffn]
    for proj, h_g, h_v in per_layer:
        in_specs += [rows(D_INNER, OFF_XS // D_INNER), rows(BC_W, OFF_BC // BC_W),
                     rows(D_POOL, OFF_POOL // D_POOL), rows(D_FF_PAD, 0), rows(D_FF_PAD, 0)]
        args += [proj, proj, proj, h_g, h_v]
    return pl.pallas_call(
        _state_out_kernel,
        grid=(depth, s // ns),
        in_specs=in_specs,
        out_specs=[blk(s_conv), blk(s_pool), blk(s_ffn)],
        out_shape=[jax.ShapeDtypeStruct(a.shape, F32) for a in (s_conv, s_pool, s_ffn)],
        compiler_params=_cparams(2),
        name="state_out",
    )(*args)


def _pad_ff(v):
    pad = [(0, 0)] * (v.ndim - 1) + [(0, D_FF_PAD - D_FF)]
    return jnp.concatenate([jnp.pad(v[..., :D_FF], pad), jnp.pad(v[..., D_FF:], pad)], axis=-1)


def _unpad_ff(v):
    return jnp.concatenate([v[..., :D_FF], v[..., D_FF_PAD:D_FF_PAD + D_FF]], axis=-1)


def _prep_layer(big, b_gate, conv_w, conv_b, dt_bias, a_log, d_skip, norm_w, pool_scale,
                ln1_g, ln1_b, fconv_w, fconv_b, ln2_g, ln2_b):
    pad_h = lambda v: jnp.pad(v, (0, DT_PAD - N_HEADS)).reshape(1, DT_PAD)
    head_of_channel = jnp.arange(D_INNER) // HEAD_DIM
    head_expand = (jnp.arange(DT_PAD)[:, None] == head_of_channel[None, :]).astype(BF16)
    return dict(
        big, b_gate=b_gate.reshape(1, -1),
        conv_w=conv_w, conv_b=conv_b.reshape(1, -1),
        dt_bias=pad_h(dt_bias), a_log=pad_h(a_log),
        d_skip_x=jnp.repeat(d_skip, HEAD_DIM).reshape(1, -1),
        norm_w=norm_w.reshape(1, -1), head_expand=head_expand,
        pool_scale=pool_scale.reshape(1, -1),
        ln1_g=ln1_g.reshape(1, -1), ln1_b=ln1_b.reshape(1, -1),
        fconv_w=_pad_ff(fconv_w), fconv_b=_pad_ff(fconv_b).reshape(1, -1),
        ln2_g=ln2_g.reshape(1, -1), ln2_b=ln2_b.reshape(1, -1),
    )


def _raw_xbc(proj):
    return proj[:, OFF_XS:OFF_XS + CONV_DIM]


def _layer_prompt(x, x_bf, lw, layer, bsz, seq):
    tm_in = 1024
    proj, xbc_tail = _in_proj(x_bf, lw, layer, tm_in, seq // tm_in, conv=True)
    dt_raw = _dt_proj(x_bf, lw["w_in_t"], layer, 1024)
    y, new_ssm = _ssd_prompt(proj, dt_raw, lw, bsz, seq)
    y_b = _pool_prompt(proj, lw["w_pool"], layer, lw["pool_scale"], seq)
    mix = _branch_merge(y, lw["w_br"], layer, proj, y_b, 512, 1024)
    x1, x1_bf = _proj_ln(mix, lw["w_out"], layer, x, lw["ln1_g"], lw["ln1_b"], 512)
    tm_up = 1024
    act, tail_g, tail_v = _ffn_up_prompt(x1_bf, lw, layer, seq, tm_up)
    x2, x2_bf = _down_ln(act, lw["w_down"], layer, x1, lw["ln2_g"], lw["ln2_b"], 1024)
    p3 = proj.reshape(bsz, seq, N_MAIN)
    new_pool = p3[:, seq - POOL_BUF:, OFF_POOL:OFF_POOL + D_POOL]
    tps_in = seq // tm_in
    new_conv = xbc_tail[tps_in - 1::tps_in, CONV_HALO - (CONV_K - 1):, OFF_XS:OFF_XS + CONV_DIM]
    tps = seq // tm_up
    last = slice(tps - 1, None, tps)
    tail = jnp.concatenate([tail_g[last, :, :D_FF], tail_v[last, :, :D_FF]], axis=-1)
    new_ffn = tail[:, FFN_HALO - (FFN_K - 1):, :]
    return x2, x2_bf, new_ssm, new_conv, new_pool, new_ffn


def _layer_sample(x, x_bf, ssm_all, layer, ssm_prev_out, s_conv, s_pool, s_ffn, lw):
    s = x.shape[0]
    proj, _ = _in_proj(x_bf, lw, layer, s, 1, conv=False)
    dt_raw = _dt_proj(x_bf, lw["w_in_t"], layer, s)
    xs, b_m, c_m, decht, xdtt = _ssd_sample_prep(proj, dt_raw, jnp.swapaxes(s_conv, 0, 1), lw)
    new_ssm, yt = _ssd_sample_step(ssm_all, layer, ssm_prev_out, decht, xdtt, b_m, c_m)
    y = _ssd_sample_finish(yt, xs, proj, lw)
    pooled = _pool_sample(proj, s_pool, layer)
    y_b = _pool_mm(pooled, lw["w_pool"], layer, lw["pool_scale"], s)
    mix = _branch_merge(y, lw["w_br"], layer, proj, y_b, s, 1024)
    x1, x1_bf = _proj_ln(mix, lw["w_out"], layer, x, lw["ln1_g"], lw["ln1_b"], s)
    act, h_g, h_v = _ffn_up_sample(x1_bf, s_ffn, lw, layer)
    x2, x2_bf = _down_ln(act, lw["w_down"], layer, x1, lw["ln2_g"], lw["ln2_b"], s)
    return x2, x2_bf, new_ssm, (proj, h_g, h_v)


def kernel(x_prompt, x_sample, state_ssm, state_ssd_conv, state_pool, state_ffn_conv, w_in, b_gate, conv_w, conv_b, dt_bias, a_log, d_skip, ssd_norm_w, w_ssd_branch, w_pool, pool_scale, w_out, ln1_g, ln1_b, w_up, ffn_conv_w, ffn_conv_b, w_down, ln2_g, ln2_b):
    bsz, seq, _ = x_prompt.shape
    n_s = x_sample.shape[0]
    assert x_sample.shape[1] == 1 and seq % 1024 == 0
    xp = x_prompt.reshape(bsz * seq, D_MODEL)
    xs = x_sample.reshape(n_s, D_MODEL)
    xp_bf, xs_bf = xp.astype(BF16), xs.astype(BF16)
    outs_p, outs_s = [], []
    ssm_all = state_ssm.reshape(DEPTH, n_s, D_INNER, D_STATE)
    ssm_out = None
    big = dict(w_in_t=jnp.swapaxes(w_in, 1, 2), w_br=w_ssd_branch, w_pool=w_pool, w_out=w_out,
               w_up=w_up, w_down=w_down)
    for i in range(DEPTH):
        lw = _prep_layer(big, b_gate[i], conv_w[i], conv_b[i], dt_bias[i], a_log[i], d_skip[i],
                         ssd_norm_w[i], pool_scale[i], ln1_g[i], ln1_b[i], ffn_conv_w[i],
                         ffn_conv_b[i], ln2_g[i], ln2_b[i])
        xp, xp_bf, *op = _layer_prompt(xp, xp_bf, lw, i, bsz, seq)
        xs, xs_bf, ssm_out, new_rows = _layer_sample(xs, xs_bf, ssm_all, i, ssm_out, state_ssd_conv[i],
                                                     state_pool, state_ffn_conv, lw)
        outs_p.append(op)
        outs_s.append(new_rows)
    new_conv_s, new_pool_s, new_ffn_s = _state_out(state_ssd_conv, state_pool, state_ffn_conv, outs_s)
    stack = lambda outs, k: jnp.stack([o[k] for o in outs])
    return (xp.reshape(bsz, seq, D_MODEL), xs.reshape(n_s, 1, D_MODEL),
            stack(outs_p, 0), stack(outs_p, 1), stack(outs_p, 2), stack(outs_p, 3),
            ssm_out.reshape(state_ssm.shape), new_conv_s, new_pool_s, new_ffn_s)
```

```python
import functools

import jax
import jax.numpy as jnp
from jax import lax
from jax.experimental import pallas as pl
from jax.experimental.pallas import tpu as pltpu

F32 = jnp.float32
BF16 = jnp.bfloat16

D_MODEL = 2048
HEAD_DIM = 64
D_INNER = 2 * D_MODEL
N_HEADS = D_INNER // HEAD_DIM
N_GROUPS = 8
HEADS_PER_GROUP = N_HEADS // N_GROUPS
GROUP_W = D_INNER // N_GROUPS
D_STATE = 128
CONV_K = 4
BC_W = 2 * N_GROUPS * D_STATE
CONV_DIM = D_INNER + BC_W
CHUNK = 128
D_POOL = D_MODEL
POOL_WINDOWS = (2, 4, 8, 16)
POOL_GC = D_POOL // len(POOL_WINDOWS)
POOL_BUF = max(POOL_WINDOWS) - 1
D_FF = 5504
FFN_K = 3
DEPTH = 2
PAST_LEN = 16384
ALPHA = (2 * DEPTH) ** 0.25
LN_EPS = 1e-5
RMS_EPS = 1e-5

LANES = 128
SUBLANES = 8
D_FF_PAD = 5632
FFN_TN = 512
DT_PAD = LANES
OFF_Z = 0
OFF_XS = D_INNER
OFF_BC = 2 * D_INNER
OFF_POOL = 2 * D_INNER + BC_W
OFF_GATE = OFF_POOL + D_POOL
N_MAIN = OFF_GATE + 2 * D_MODEL
VMEM_LIMIT = 56 * 1024 * 1024
NEG_BIG = -1e30


def _cparams(n_axes):
    return pltpu.CompilerParams(dimension_semantics=("arbitrary",) * n_axes,
                                vmem_limit_bytes=VMEM_LIMIT)


NEG_LOG2E = -1.4426950408889634


def _sigmoid(x):
    return 1.0 / (1.0 + jnp.exp2(x * NEG_LOG2E))


def _silu(x):
    return x * _sigmoid(x)


def _softplus(x):
    return jnp.maximum(x, 0.0) + jnp.log(1.0 + jnp.exp(-jnp.abs(x)))


def _rows_above(e, shift):
    n, c = e.shape
    rot = pltpu.roll(e.reshape(n // SUBLANES, SUBLANES, c), shift, 1).reshape(n, c)
    sub = lax.broadcasted_iota(jnp.int32, (n - SUBLANES, c), 0) % SUBLANES
    return jnp.where(sub < shift, rot[0:n - SUBLANES, :], rot[SUBLANES:n, :])


def _shift_rows(ext, shift, rows):
    return ext[SUBLANES:SUBLANES + rows, :] if shift == 0 else _rows_above(ext, shift)


def _layer_norm(r, g, b):
    mu = jnp.mean(r, axis=-1, keepdims=True)
    d = r - mu
    var = jnp.mean(d * d, axis=-1, keepdims=True)
    return d * lax.rsqrt(var + LN_EPS) * g + b


def _dt_proj_kernel(x_ref, w_ref, o_ref):
    o_ref[...] = lax.dot_general(x_ref[...], w_ref[...].astype(BF16), (((1,), (1,)), ((), ())),
                                 preferred_element_type=F32)


def _dt_proj(x_bf, w_in_t, layer, tm):
    m = x_bf.shape[0]
    return pl.pallas_call(
        _dt_proj_kernel,
        grid=(m // tm,),
        in_specs=[pl.BlockSpec((tm, D_MODEL), lambda i: (i, 0)),
                  pl.BlockSpec((None, DT_PAD, D_MODEL), lambda i: (layer, DT_COL_BLOCK, 0))],
        out_specs=pl.BlockSpec((tm, DT_PAD), lambda i: (i, 0)),
        out_shape=jax.ShapeDtypeStruct((m, DT_PAD), F32),
        compiler_params=_cparams(1),
        name="dt_proj",
    )(x_bf, w_in_t)


IN_TN = 1024
J_XBC = OFF_XS // IN_TN
J_POOL = OFF_POOL // IN_TN
J_GATE = OFF_GATE // IN_TN
J_END = N_MAIN // IN_TN
CONV_HALO = SUBLANES
PG_SHIFT = N_HEADS
DT_COL_BLOCK = (2 * D_INNER + BC_W) // DT_PAD


def _in_proj_kernel(x_ref, wa_ref, wb_ref, cw_ref, cb_ref, bg_ref, o_ref, tail_ref, w_s, h_s,
                    *, tiles_per_seq, conv):
    j = pl.program_id(0)
    i = pl.program_id(1)
    tm = x_ref.shape[0]

    @pl.when((i == 0) & (j < J_POOL))
    def _():
        w_s[...] = wa_ref[...].astype(BF16)

    @pl.when((i == 0) & (j >= J_POOL))
    def _():
        w_s[...] = jnp.concatenate([wa_ref[PG_SHIFT:, :], wb_ref[:PG_SHIFT, :]], axis=0).astype(BF16)

    def mm():
        return lax.dot_general(x_ref[...], w_s[...], (((1,), (1,)), ((), ())),
                               preferred_element_type=F32)

    is_xbc = (j >= J_XBC) & (j < J_POOL)

    @pl.when(jnp.logical_not(is_xbc))
    def _():
        tail_ref[0] = jnp.zeros((CONV_HALO, IN_TN), F32)

    @pl.when(j < J_XBC)
    def _():
        o_ref[...] = _silu(mm())

    @pl.when(is_xbc)
    def _():
        if conv:
            @pl.when(i % tiles_per_seq == 0)
            def _():
                h_s[...] = jnp.zeros((CONV_HALO, IN_TN), F32)

            ext = jnp.concatenate([h_s[...], mm()], axis=0)
            acc = cb_ref[...]
            for k in range(CONV_K):
                acc = acc + _shift_rows(ext, CONV_K - 1 - k, tm) * cw_ref[k:k + 1, :]
            o_ref[...] = _silu(acc)
            tail = ext[tm:tm + CONV_HALO, :]
            h_s[...] = tail
            tail_ref[0] = tail
        else:
            o_ref[...] = mm()
            tail_ref[0] = jnp.zeros((CONV_HALO, IN_TN), F32)

    @pl.when((j >= J_POOL) & (j < J_GATE))
    def _():
        o_ref[...] = mm()

    @pl.when(j >= J_GATE)
    def _():
        o_ref[...] = _sigmoid(mm() + bg_ref[...])


def _in_proj(x_bf, lw, layer, tm, tiles_per_seq, conv):
    m = x_bf.shape[0]
    ni = m // tm
    wb_buffers = 2 if tm <= LANES else 1
    n_conv_tiles = CONV_DIM // IN_TN
    clamp = lambda v, lo, hi: jnp.minimum(jnp.maximum(v, lo), hi)
    conv_tile = lambda j: clamp(j - J_XBC, 0, n_conv_tiles - 1)
    kern = functools.partial(_in_proj_kernel, tiles_per_seq=tiles_per_seq, conv=conv)
    return pl.pallas_call(
        kern,
        grid=(J_END, ni),
        in_specs=[pl.BlockSpec((tm, D_MODEL), lambda j, i: (i, 0)),
                  pl.BlockSpec((None, IN_TN, D_MODEL), lambda j, i: (layer, j, 0)),
                  pl.BlockSpec((None, IN_TN, D_MODEL), lambda j, i: (layer, jnp.maximum(j + 1, J_POOL), 0),
                               pipeline_mode=pl.Buffered(wb_buffers)),
                  pl.BlockSpec((CONV_K, IN_TN), lambda j, i: (0, conv_tile(j))),
                  pl.BlockSpec((1, IN_TN), lambda j, i: (0, conv_tile(j))),
                  pl.BlockSpec((1, IN_TN), lambda j, i: (0, clamp(j - J_GATE, 0, J_END - J_GATE - 1)))],
        out_specs=[pl.BlockSpec((tm, IN_TN), lambda j, i: (i, j)),
                   pl.BlockSpec((1, CONV_HALO, IN_TN), lambda j, i: (i, 0, j))],
        out_shape=[jax.ShapeDtypeStruct((m, N_MAIN), F32),
                   jax.ShapeDtypeStruct((ni, CONV_HALO, N_MAIN), F32)],
        scratch_shapes=[pltpu.VMEM((IN_TN, D_MODEL), BF16),
                        pltpu.VMEM((CONV_HALO, IN_TN), F32)],
        compiler_params=_cparams(2),
        name="in_proj",
    )(x_bf, lw["w_in_t"], lw["w_in_t"], lw["conv_w"], lw["conv_b"], lw["b_gate"])


def _split3(v):
    hi = v.astype(BF16)
    r1 = v - hi.astype(F32)
    mid = r1.astype(BF16)
    lo = (r1 - mid.astype(F32)).astype(BF16)
    return hi, mid, lo


def _ssd_prompt_kernel(xs_ref, bc_ref, zs_ref, dt_ref, dtb_ref, alog_ref, dsk_ref, nw_ref,
                       y_ref, st_ref,
                       xs_s, bt_s, c_s, y_s, state_s, acol_s, arow_s, dtrow_s):
    c = pl.program_id(1)
    n_chunks = pl.num_programs(1)
    q = CHUNK

    @pl.when(c == 0)
    def _():
        state_s[...] = jnp.zeros(state_s.shape, F32)

    for g in range(N_GROUPS):
        xs_s[g] = xs_ref[:, g * GROUP_W:(g + 1) * GROUP_W]
        bt_s[g] = bc_ref[:, g * D_STATE:(g + 1) * D_STATE].T
        c_s[g] = bc_ref[:, (N_GROUPS + g) * D_STATE:(N_GROUPS + g + 1) * D_STATE].astype(BF16)

    dt = _softplus(dt_ref[...] + dtb_ref[...])
    a_neg = -jnp.exp(alog_ref[...])
    d_a = dt * a_neg
    row = lax.broadcasted_iota(jnp.int32, (q, q), 0)
    col = lax.broadcasted_iota(jnp.int32, (q, q), 1)
    causal = row >= col
    tril = jnp.where(causal, 1.0, 0.0).astype(BF16)
    hi, mid, lo3 = _split3(d_a)
    a_cum = (jnp.dot(tril, hi, preferred_element_type=F32)
             + jnp.dot(tril, mid, preferred_element_type=F32)
             + jnp.dot(tril, lo3, preferred_element_type=F32))
    arow_s[...] = a_cum.T
    dtrow_s[...] = dt.T
    for g in range(N_GROUPS):
        sh = (LANES - HEADS_PER_GROUP * g) % LANES
        acol_s[g] = a_cum if sh == 0 else pltpu.roll(a_cum, sh, 1)

    lane = lax.broadcasted_iota(jnp.int32, (q, LANES), 1)
    lo_half = lane < HEAD_DIM

    def group_body(g, carry):
        acol = acol_s[g]
        c_g = c_s[g]
        bt_g = bt_s[g]
        cb = jnp.dot(c_g, bt_g.astype(BF16), preferred_element_type=F32)
        y_off_g = jnp.dot(c_g, state_s[g].astype(BF16), preferred_element_type=F32)
        for k in range(HEADS_PER_GROUP // 2):
            l_parts, b_parts, a_b = [], [], []
            for e in range(2):
                hh = 2 * k + e
                head = g * HEADS_PER_GROUP + hh
                a_col = jnp.broadcast_to(acol[:, hh:hh + 1], (q, q))
                a_row = arow_s[pl.ds(head, 1), :]
                dt_row = dtrow_s[pl.ds(head, 1), :]
                seg = jnp.where(causal, a_col - a_row, NEG_BIG)
                l_parts.append((cb * jnp.exp(seg) * dt_row).astype(BF16))
                w_row = dt_row * jnp.exp(a_col[q - 1:q, :] - a_row)
                b_parts.append((bt_g * w_row).astype(BF16))
                a_b.append(a_col)
            lhs = jnp.concatenate([jnp.concatenate(l_parts, axis=1),
                                   jnp.concatenate(b_parts, axis=1)], axis=0)
            cols = slice(k * LANES, (k + 1) * LANES)
            xs_bf = xs_s[g, :, cols].astype(BF16)
            zero = jnp.zeros_like(xs_bf)
            rhs = jnp.concatenate([jnp.where(lo_half, xs_bf, zero),
                                   jnp.where(lo_half, zero, xs_bf)], axis=0)
            res = jnp.dot(lhs, rhs, preferred_element_type=F32)
            a_pair = jnp.where(lo_half, a_b[0], a_b[1])
            y_s[g, :, cols] = res[0:q] + y_off_g[:, cols] * jnp.exp(a_pair)
            cdec = jnp.exp(a_pair[q - 1:q, :])
            state_s[g, :, cols] = state_s[g, :, cols] * cdec + res[q:2 * q]
        return carry

    lax.fori_loop(0, N_GROUPS, group_body, 0)

    for g in range(N_GROUPS):
        cols = slice(g * GROUP_W, (g + 1) * GROUP_W)
        yv = y_s[g] + dsk_ref[:, cols] * xs_s[g]
        v = yv * zs_ref[:, cols]
        ms = jnp.mean(v * v, axis=-1, keepdims=True)
        y_ref[:, cols] = (v * lax.rsqrt(ms + RMS_EPS) * nw_ref[:, cols]).astype(y_ref.dtype)

    @pl.when(c == n_chunks - 1)
    def _():
        for g in range(N_GROUPS):
            st_ref[0, g] = state_s[g].T


def _ssd_prompt(proj, dt_raw, lw, bsz, seq):
    n_chunks = seq // CHUNK
    rows = lambda b, c: b * n_chunks + c
    small = lambda shape: pl.BlockSpec(shape, lambda b, c: (0, 0))
    y, st = pl.pallas_call(
        _ssd_prompt_kernel,
        grid=(bsz, n_chunks),
        in_specs=[
            pl.BlockSpec((CHUNK, D_INNER), lambda b, c: (rows(b, c), OFF_XS // D_INNER)),
            pl.BlockSpec((CHUNK, BC_W), lambda b, c: (rows(b, c), OFF_BC // BC_W)),
            pl.BlockSpec((CHUNK, D_INNER), lambda b, c: (rows(b, c), OFF_Z // D_INNER)),
            pl.BlockSpec((CHUNK, DT_PAD), lambda b, c: (rows(b, c), 0)),
            small((1, DT_PAD)), small((1, DT_PAD)), small((1, D_INNER)), small((1, D_INNER)),
        ],
        out_specs=[
            pl.BlockSpec((CHUNK, D_INNER), lambda b, c: (rows(b, c), 0)),
            pl.BlockSpec((1, N_GROUPS, GROUP_W, D_STATE), lambda b, c: (b, 0, 0, 0)),
        ],
        out_shape=[jax.ShapeDtypeStruct((bsz * seq, D_INNER), BF16),
                   jax.ShapeDtypeStruct((bsz, N_GROUPS, GROUP_W, D_STATE), F32)],
        scratch_shapes=[
            pltpu.VMEM((N_GROUPS, CHUNK, GROUP_W), F32),
            pltpu.VMEM((N_GROUPS, D_STATE, CHUNK), F32),
            pltpu.VMEM((N_GROUPS, CHUNK, D_STATE), BF16),
            pltpu.VMEM((N_GROUPS, CHUNK, GROUP_W), F32),
            pltpu.VMEM((N_GROUPS, D_STATE, GROUP_W), F32),
            pltpu.VMEM((N_GROUPS, CHUNK, LANES), F32),
            pltpu.VMEM((LANES, CHUNK), F32),
            pltpu.VMEM((LANES, CHUNK), F32),
        ],
        compiler_params=_cparams(2),
        name="ssd_prompt",
    )(proj, proj, proj, dt_raw, lw["dt_bias"], lw["a_log"], lw["d_skip_x"], lw["norm_w"])
    return y, st.reshape(bsz, N_HEADS, HEAD_DIM, D_STATE)


def _ssd_sample_prep_kernel(xs_ref, bc_ref, dt_ref, cst_ref, cw_ref, cb_ref, dtb_ref, alog_ref,
                            ex_ref, xs_o, b_o, c_o, decht_o, xdtt_o):
    def conv(u, lo, hi):
        acc = cb_ref[:, lo:hi]
        for j in range(CONV_K - 1):
            acc = acc + cst_ref[j][:, lo:hi] * cw_ref[j:j + 1, lo:hi]
        acc = acc + u * cw_ref[CONV_K - 1:CONV_K, lo:hi]
        return _silu(acc)

    xs = conv(xs_ref[...], 0, D_INNER)
    bc = conv(bc_ref[...], D_INNER, CONV_DIM)
    xs_o[...] = xs
    b_o[...] = bc[:, 0:N_GROUPS * D_STATE]
    c_o[...] = bc[:, N_GROUPS * D_STATE:BC_W]
    dt = _softplus(dt_ref[...] + dtb_ref[...])
    d_a = dt * (-jnp.exp(alog_ref[...]))
    ex = ex_ref[...]

    def expand(v):
        hi, mid, lo3 = _split3(v)
        return (jnp.dot(hi, ex, preferred_element_type=F32)
                + jnp.dot(mid, ex, preferred_element_type=F32)
                + jnp.dot(lo3, ex, preferred_element_type=F32))

    xdt = expand(dt) * xs
    decht_o[...] = jnp.exp(d_a).T
    xdtt_o[...] = xdt.T.astype(BF16)


def _ssd_sample_prep(proj, dt_raw, conv_state_t, lw):
    s = proj.shape[0]
    full = lambda shape: pl.BlockSpec(shape, lambda i: (0,) * len(shape))
    return pl.pallas_call(
        _ssd_sample_prep_kernel,
        grid=(1,),
        in_specs=[
            pl.BlockSpec((s, D_INNER), lambda i: (0, OFF_XS // D_INNER)),
            pl.BlockSpec((s, BC_W), lambda i: (0, OFF_BC // BC_W)),
            full((s, DT_PAD)), full((CONV_K - 1, s, CONV_DIM)),
            full((CONV_K, CONV_DIM)), full((1, CONV_DIM)), full((1, DT_PAD)), full((1, DT_PAD)),
            full((DT_PAD, D_INNER)),
        ],
        out_specs=[full((s, D_INNER)), full((s, N_GROUPS * D_STATE)), full((s, N_GROUPS * D_STATE)),
                   full((DT_PAD, s)), full((D_INNER, s))],
        out_shape=[jax.ShapeDtypeStruct((s, D_INNER), F32),
                   jax.ShapeDtypeStruct((s, N_GROUPS * D_STATE), F32),
                   jax.ShapeDtypeStruct((s, N_GROUPS * D_STATE), F32),
                   jax.ShapeDtypeStruct((DT_PAD, s), F32),
                   jax.ShapeDtypeStruct((D_INNER, s), BF16)],
        compiler_params=_cparams(1),
        name="ssd_sample_prep",
    )(proj, proj, dt_raw, conv_state_t, lw["conv_w"], lw["conv_b"], lw["dt_bias"], lw["a_log"],
      lw["head_expand"])


STEP_SAMPLES = 2


def _ssd_sample_step_kernel(st_ref, decht_ref, xdtt_ref, b_ref, c_ref, *rest, fill_other_layers):
    st_o, yt_o = rest[-2:]
    n_s = xdtt_ref.shape[1]

    @pl.when(pl.program_id(0) == 0)
    def _():
        yt_o[...] = jnp.zeros(yt_o.shape, F32)

    if fill_other_layers:
        st_o[1:] = jnp.zeros((st_o.shape[0] - 1,) + st_o.shape[1:], F32)

    for u in range(STEP_SAMPLES):
        s = pl.program_id(0) * STEP_SAMPLES + u
        onehot = lax.broadcasted_iota(jnp.int32, (GROUP_W, n_s), 1) == s
        onehot_h = lax.broadcasted_iota(jnp.int32, (DT_PAD, n_s), 1) == s
        pick = jnp.where(lax.broadcasted_iota(jnp.int32, (n_s, D_STATE), 0) == s, 1.0, 0.0).astype(BF16)
        dech = jnp.sum(jnp.where(onehot_h, decht_ref[...], 0.0), axis=-1, keepdims=True)
        b_all = b_ref[pl.ds(s, 1), :]
        c_all = c_ref[pl.ds(s, 1), :]
        for g in range(N_GROUPS):
            rows = slice(g * GROUP_W, (g + 1) * GROUP_W)
            cols = slice(g * D_STATE, (g + 1) * D_STATE)
            xdt = jnp.dot(xdtt_ref[rows, :], pick, preferred_element_type=F32)
            b_row = b_all[:, cols]
            c_row = c_all[:, cols]
            upd = xdt * b_row
            parts = []
            for hh in range(HEADS_PER_GROUP):
                h = g * HEADS_PER_GROUP + hh
                r_h = slice(h * HEAD_DIM, (h + 1) * HEAD_DIM)
                parts.append(st_ref[0, u, r_h, :] * dech[h:h + 1, :]
                             + upd[hh * HEAD_DIM:(hh + 1) * HEAD_DIM, :])
            st_new = jnp.concatenate(parts, axis=0)
            st_o[0, u, rows, :] = st_new
            y_col = jnp.sum(st_new * c_row, axis=-1, keepdims=True)
            yt_o[rows, :] = yt_o[rows, :] + jnp.where(onehot, y_col, 0.0)


def _ssd_sample_step(state_all, layer, prev_out, decht, xdtt, b_m, c_m):
    depth = state_all.shape[0]
    s = xdtt.shape[1]
    full = lambda shape: pl.BlockSpec(shape, lambda i: (0,) * len(shape))
    ns = STEP_SAMPLES
    in_specs = [pl.BlockSpec((1, ns, D_INNER, D_STATE), lambda i: (layer, i, 0, 0)),
                full((DT_PAD, s)), full((D_INNER, s)),
                full((s, N_GROUPS * D_STATE)), full((s, N_GROUPS * D_STATE))]
    args = [state_all, decht, xdtt, b_m, c_m]
    if prev_out is None:
        assert layer == 0
        aliases = {}
        st_out = pl.BlockSpec((depth, ns, D_INNER, D_STATE), lambda i: (0, i, 0, 0))
    else:
        in_specs.append(pl.BlockSpec(memory_space=pl.ANY))
        args.append(prev_out)
        aliases = {len(args) - 1: 0}
        st_out = pl.BlockSpec((1, ns, D_INNER, D_STATE), lambda i: (layer, i, 0, 0))
    return pl.pallas_call(
        functools.partial(_ssd_sample_step_kernel, fill_other_layers=prev_out is None),
        grid=(s // ns,),
        in_specs=in_specs,
        out_specs=[st_out, full((D_INNER, s))],
        out_shape=[jax.ShapeDtypeStruct(state_all.shape, F32),
                   jax.ShapeDtypeStruct((D_INNER, s), F32)],
        input_output_aliases=aliases,
        compiler_params=_cparams(1),
        name="ssd_sample_step",
    )(*args)


def _ssd_sample_finish_kernel(yt_ref, xs_ref, zs_ref, dsk_ref, nw_ref, y_ref):
    y = yt_ref[...].T
    for g in range(N_GROUPS):
        cols = slice(g * GROUP_W, (g + 1) * GROUP_W)
        yv = y[:, cols] + dsk_ref[:, cols] * xs_ref[:, cols]
        v = yv * zs_ref[:, cols]
        ms = jnp.mean(v * v, axis=-1, keepdims=True)
        y_ref[:, cols] = (v * lax.rsqrt(ms + RMS_EPS) * nw_ref[:, cols]).astype(y_ref.dtype)


def _ssd_sample_finish(yt, xs, proj, lw):
    s = xs.shape[0]
    full = lambda shape: pl.BlockSpec(shape, lambda i: (0,) * len(shape))
    return pl.pallas_call(
        _ssd_sample_finish_kernel,
        grid=(1,),
        in_specs=[full((D_INNER, s)), full((s, D_INNER)),
                  pl.BlockSpec((s, D_INNER), lambda i: (0, OFF_Z // D_INNER)),
                  full((1, D_INNER)), full((1, D_INNER))],
        out_specs=full((s, D_INNER)),
        out_shape=jax.ShapeDtypeStruct((s, D_INNER), BF16),
        compiler_params=_cparams(1),
        name="ssd_sample_finish",
    )(yt, xs, proj, lw["d_skip_x"], lw["norm_w"])


POOL_TP = 1024
POOL_HALO = 32


def _pool_prompt_kernel(u_ref, w_ref, sc_ref, o_ref, ext_s, w_s, *, tiles_per_seq):
    g = pl.program_id(0)
    i = pl.program_id(1)
    tp = POOL_TP

    @pl.when(i == 0)
    def _():
        w_s[...] = w_ref[...].astype(BF16)

    @pl.when(i % tiles_per_seq == 0)
    def _():
        ext_s[...] = jnp.zeros((POOL_HALO, POOL_GC), F32)

    u = u_ref[...]
    ext = jnp.concatenate([ext_s[...], u], axis=0)
    pos = (i % tiles_per_seq) * tp + lax.broadcasted_iota(jnp.int32, (tp, 1), 0)
    for gi, win in enumerate(POOL_WINDOWS):
        @pl.when(g == gi)
        def _():
            tot, width = ext, 1
            while width < win:
                if width < SUBLANES:
                    tot = tot[SUBLANES:, :] + _rows_above(tot, width)
                else:
                    tot = tot[width:, :] + tot[:-width, :]
                width *= 2
            tot = tot[tot.shape[0] - tp:, :]
            cnt = jnp.minimum(win, pos + 1).astype(F32)
            pooled = (tot / cnt - u).astype(BF16)
            o_ref[...] = jnp.dot(pooled, w_s[...], preferred_element_type=F32) * sc_ref[...]
    ext_s[...] = ext[tp:tp + POOL_HALO, :]


def _pool_prompt(proj, w_pool, layer, scale, seq):
    m = proj.shape[0]
    ng = len(POOL_WINDOWS)
    kern = functools.partial(_pool_prompt_kernel, tiles_per_seq=seq // POOL_TP)
    return pl.pallas_call(
        kern,
        grid=(ng, m // POOL_TP),
        in_specs=[pl.BlockSpec((POOL_TP, POOL_GC), lambda g, i: (i, OFF_POOL // POOL_GC + g)),
                  pl.BlockSpec((None, None, POOL_GC, POOL_GC), lambda g, i: (layer, g, 0, 0)),
                  pl.BlockSpec((1, POOL_GC), lambda g, i: (0, g))],
        out_specs=pl.BlockSpec((POOL_TP, POOL_GC), lambda g, i: (i, g)),
        out_shape=jax.ShapeDtypeStruct((m, D_POOL), F32),
        scratch_shapes=[pltpu.VMEM((POOL_HALO, POOL_GC), F32),
                        pltpu.VMEM((POOL_GC, POOL_GC), BF16)],
        compiler_params=_cparams(2),
        name="pool_prompt",
    )(proj, w_pool, scale)


def _pool_sample_kernel(u_ref, buf_ref, o_ref):
    g = pl.program_id(0)
    for gi, win in enumerate(POOL_WINDOWS):
        @pl.when(g == gi)
        def _():
            u = u_ref[...]
            tot = u
            for k in range(1, win):
                tot = tot + buf_ref[:, POOL_BUF - k, :]
            cnt = float(min(win, PAST_LEN + 1))
            o_ref[...] = (tot / cnt - u).astype(o_ref.dtype)


def _pool_sample(proj, pool_state, layer):
    s = proj.shape[0]
    return pl.pallas_call(
        _pool_sample_kernel,
        grid=(len(POOL_WINDOWS),),
        in_specs=[pl.BlockSpec((s, POOL_GC), lambda g: (0, OFF_POOL // POOL_GC + g)),
                  pl.BlockSpec((None, s, POOL_BUF, POOL_GC), lambda g: (layer, 0, 0, g))],
        out_specs=pl.BlockSpec((s, POOL_GC), lambda g: (0, g)),
        out_shape=jax.ShapeDtypeStruct((s, D_POOL), BF16),
        compiler_params=_cparams(1),
        name="pool_sample",
    )(proj, pool_state)


def _pool_mm_kernel(p_ref, w_ref, sc_ref, o_ref, w_s):
    @pl.when(pl.program_id(1) == 0)
    def _():
        w_s[...] = w_ref[...].astype(BF16)

    acc = jnp.dot(p_ref[...], w_s[...], preferred_element_type=F32)
    o_ref[...] = acc * sc_ref[...]


def _pool_mm(pooled, w_pool, layer, scale, tm):
    m = pooled.shape[0]
    ng = len(POOL_WINDOWS)
    return pl.pallas_call(
        _pool_mm_kernel,
        grid=(ng, m // tm),
        in_specs=[pl.BlockSpec((tm, POOL_GC), lambda g, i: (i, g)),
                  pl.BlockSpec((None, None, POOL_GC, POOL_GC), lambda g, i: (layer, g, 0, 0)),
                  pl.BlockSpec((1, POOL_GC), lambda g, i: (0, g))],
        out_specs=pl.BlockSpec((tm, POOL_GC), lambda g, i: (i, g)),
        out_shape=jax.ShapeDtypeStruct((m, D_POOL), F32),
        scratch_shapes=[pltpu.VMEM((POOL_GC, POOL_GC), BF16)],
        compiler_params=_cparams(2),
        name="pool_mm",
    )(pooled, w_pool, scale)


def _branch_merge_kernel(y_ref, w_ref, ga_ref, gb_ref, yb_ref, o_ref, w_s):
    @pl.when(pl.program_id(1) == 0)
    def _():
        w_s[...] = w_ref[...].astype(BF16)

    y_a = jnp.dot(y_ref[...], w_s[...], preferred_element_type=F32)
    o_ref[...] = (ga_ref[...] * y_a + gb_ref[...] * yb_ref[...]).astype(o_ref.dtype)


def _branch_merge(y, w_br, layer, proj, y_b, tm, tn):
    m = y.shape[0]
    nj = D_MODEL // tn
    ga0 = OFF_GATE // tn
    return pl.pallas_call(
        _branch_merge_kernel,
        grid=(nj, m // tm),
        in_specs=[pl.BlockSpec((tm, D_INNER), lambda j, i: (i, 0)),
                  pl.BlockSpec((None, D_INNER, tn), lambda j, i: (layer, 0, j),
                               pipeline_mode=pl.Buffered(1)),
                  pl.BlockSpec((tm, tn), lambda j, i: (i, ga0 + j)),
                  pl.BlockSpec((tm, tn), lambda j, i: (i, ga0 + nj + j)),
                  pl.BlockSpec((tm, tn), lambda j, i: (i, j))],
        out_specs=pl.BlockSpec((tm, tn), lambda j, i: (i, j)),
        out_shape=jax.ShapeDtypeStruct((m, D_MODEL), BF16),
        scratch_shapes=[pltpu.VMEM((D_INNER, tn), BF16)],
        compiler_params=_cparams(2),
        name="branch_merge",
    )(y, w_br, proj, proj, y_b)


def _proj_ln_kernel(m_ref, w_ref, x_ref, g_ref, b_ref, o_ref, obf_ref, w_s):
    @pl.when(pl.program_id(0) == 0)
    def _():
        w_s[...] = w_ref[...].astype(BF16)

    acc = jnp.dot(m_ref[...], w_s[...], preferred_element_type=F32)
    y = _layer_norm(ALPHA * x_ref[...] + acc, g_ref[...], b_ref[...])
    o_ref[...] = y
    obf_ref[...] = y.astype(BF16)


def _proj_ln(mix, w, layer, x, g, b, tm):
    m, k = mix.shape
    return pl.pallas_call(
        _proj_ln_kernel,
        grid=(m // tm,),
        in_specs=[pl.BlockSpec((tm, k), lambda i: (i, 0)),
                  pl.BlockSpec((None, k, D_MODEL), lambda i: (layer, 0, 0),
                               pipeline_mode=pl.Buffered(1)),
                  pl.BlockSpec((tm, D_MODEL), lambda i: (i, 0)),
                  pl.BlockSpec((1, D_MODEL), lambda i: (0, 0)),
                  pl.BlockSpec((1, D_MODEL), lambda i: (0, 0))],
        out_specs=[pl.BlockSpec((tm, D_MODEL), lambda i: (i, 0)),
                   pl.BlockSpec((tm, D_MODEL), lambda i: (i, 0))],
        out_shape=[jax.ShapeDtypeStruct((m, D_MODEL), F32),
                   jax.ShapeDtypeStruct((m, D_MODEL), BF16)],
        scratch_shapes=[pltpu.VMEM((k, D_MODEL), BF16)],
        compiler_params=_cparams(1),
        name="proj_ln",
    )(mix, w, x, g, b)


FFN_HALO = SUBLANES
FFN_VAL_BLK = D_FF // FFN_TN
FFN_VAL_SHIFT = D_FF % FFN_TN
assert FFN_VAL_SHIFT % LANES == 0


def _ffn_weight_tiles(j, wg_ref, wva_ref, wvb_ref, wg_s, wv_s):
    col = j * FFN_TN + lax.broadcasted_iota(jnp.int32, (1, FFN_TN), 1)
    valid = col < D_FF
    wg_s[...] = jnp.where(valid, wg_ref[...], 0.0).astype(BF16)
    wv = jnp.concatenate([wva_ref[:, FFN_VAL_SHIFT:], wvb_ref[:, :FFN_VAL_SHIFT]], axis=1)
    wv_s[...] = jnp.where(valid, wv, 0.0).astype(BF16)


def _ffn_w_specs(layer, index_of):
    blk = (None, D_MODEL, FFN_TN)
    return [pl.BlockSpec(blk, index_of(lambda j: (layer, 0, j))),
            pl.BlockSpec(blk, index_of(lambda j: (layer, 0, FFN_VAL_BLK + j))),
            pl.BlockSpec(blk, index_of(lambda j: (layer, 0, FFN_VAL_BLK + j + 1)))]


def _ffn_conv_gate(ext_g, ext_v, cw_g, cw_v, cb_g, cb_v, tm):
    def conv(ext, cw, cb):
        acc = cb[...]
        for j in range(FFN_K):
            acc = acc + _shift_rows(ext, FFN_K - 1 - j, tm) * cw[j:j + 1, :]
        return acc
    return _silu(conv(ext_g, cw_g, cb_g)) * conv(ext_v, cw_v, cb_v)


def _ffn_up_prompt_kernel(x_ref, wg_ref, wva_ref, wvb_ref, cwg_ref, cwv_ref, cbg_ref, cbv_ref,
                          a_ref, tg_ref, tv_ref, wg_s, wv_s, hg_s, hv_s, *, tiles_per_seq):
    j = pl.program_id(0)
    i = pl.program_id(1)
    tm = x_ref.shape[0]

    @pl.when(i == 0)
    def _():
        _ffn_weight_tiles(j, wg_ref, wva_ref, wvb_ref, wg_s, wv_s)

    @pl.when(i % tiles_per_seq == 0)
    def _():
        hg_s[...] = jnp.zeros((FFN_HALO, FFN_TN), F32)
        hv_s[...] = jnp.zeros((FFN_HALO, FFN_TN), F32)

    x = x_ref[...]
    ext_g = jnp.concatenate([hg_s[...], jnp.dot(x, wg_s[...], preferred_element_type=F32)], axis=0)
    ext_v = jnp.concatenate([hv_s[...], jnp.dot(x, wv_s[...], preferred_element_type=F32)], axis=0)
    a_ref[...] = _ffn_conv_gate(ext_g, ext_v, cwg_ref, cwv_ref, cbg_ref, cbv_ref, tm).astype(a_ref.dtype)
    tail_g = ext_g[tm:tm + FFN_HALO, :]
    tail_v = ext_v[tm:tm + FFN_HALO, :]
    hg_s[...] = tail_g
    hv_s[...] = tail_v
    tg_ref[0] = tail_g
    tv_ref[0] = tail_v


def _ffn_up_prompt(x_bf, lw, layer, seq, tm):
    m = x_bf.shape[0]
    nj = D_FF_PAD // FFN_TN
    ni = m // tm
    kern = functools.partial(_ffn_up_prompt_kernel, tiles_per_seq=seq // tm)
    cspec_g = lambda r: pl.BlockSpec((r, FFN_TN), lambda j, i: (0, j))
    cspec_v = lambda r: pl.BlockSpec((r, FFN_TN), lambda j, i: (0, nj + j))
    w_specs = _ffn_w_specs(layer, lambda f: (lambda j, i: f(j)))
    return pl.pallas_call(
        kern,
        grid=(nj, ni),
        in_specs=[pl.BlockSpec((tm, D_MODEL), lambda j, i: (i, 0)), *w_specs,
                  cspec_g(FFN_K), cspec_v(FFN_K), cspec_g(1), cspec_v(1)],
        out_specs=[pl.BlockSpec((tm, FFN_TN), lambda j, i: (i, j)),
                   pl.BlockSpec((1, FFN_HALO, FFN_TN), lambda j, i: (i, 0, j)),
                   pl.BlockSpec((1, FFN_HALO, FFN_TN), lambda j, i: (i, 0, j))],
        out_shape=[jax.ShapeDtypeStruct((m, D_FF_PAD), BF16),
                   jax.ShapeDtypeStruct((ni, FFN_HALO, D_FF_PAD), F32),
                   jax.ShapeDtypeStruct((ni, FFN_HALO, D_FF_PAD), F32)],
        scratch_shapes=[pltpu.VMEM((D_MODEL, FFN_TN), BF16),
                        pltpu.VMEM((D_MODEL, FFN_TN), BF16),
                        pltpu.VMEM((FFN_HALO, FFN_TN), F32),
                        pltpu.VMEM((FFN_HALO, FFN_TN), F32)],
        compiler_params=_cparams(2),
        name="ffn_up_prompt",
    )(x_bf, lw["w_up"], lw["w_up"], lw["w_up"],
      lw["fconv_w"], lw["fconv_w"], lw["fconv_b"], lw["fconv_b"])


def _ffn_up_sample_kernel(x_ref, wg_ref, wva_ref, wvb_ref, sg_ref, sva_ref, svb_ref, cwg_ref, cwv_ref,
                          cbg_ref, cbv_ref, a_ref, hg_ref, hv_ref, wg_s, wv_s):
    j = pl.program_id(0)
    _ffn_weight_tiles(j, wg_ref, wva_ref, wvb_ref, wg_s, wv_s)
    x = x_ref[...]
    hg = jnp.dot(x, wg_s[...], preferred_element_type=F32)
    hv = jnp.dot(x, wv_s[...], preferred_element_type=F32)
    hg_ref[...] = hg
    hv_ref[...] = hv
    valid = j * FFN_TN + lax.broadcasted_iota(jnp.int32, (1, FFN_TN), 1) < D_FF

    def conv(h, rows, cw, cb):
        acc = cb[...]
        for k in range(FFN_K - 1):
            acc = acc + jnp.where(valid, rows(k), 0.0) * cw[k:k + 1, :]
        return acc + h * cw[FFN_K - 1:FFN_K, :]

    gate_rows = lambda k: sg_ref[:, k, :]
    value_rows = lambda k: jnp.concatenate([sva_ref[:, k, FFN_VAL_SHIFT:],
                                            svb_ref[:, k, :FFN_VAL_SHIFT]], axis=1)
    a_ref[...] = (_silu(conv(hg, gate_rows, cwg_ref, cbg_ref))
                  * conv(hv, value_rows, cwv_ref, cbv_ref)).astype(a_ref.dtype)


def _ffn_up_sample(x_bf, ffn_state, lw, layer):
    s = x_bf.shape[0]
    nj = D_FF_PAD // FFN_TN
    g_blk = lambda r: pl.BlockSpec((r, FFN_TN), lambda j: (0, j))
    v_blk = lambda r: pl.BlockSpec((r, FFN_TN), lambda j: (0, nj + j))
    w_specs = _ffn_w_specs(layer, lambda f: f)
    st_blk = (None, s, FFN_K - 1, FFN_TN)
    return pl.pallas_call(
        _ffn_up_sample_kernel,
        grid=(nj,),
        in_specs=[pl.BlockSpec((s, D_MODEL), lambda j: (0, 0)), *w_specs,
                  pl.BlockSpec(st_blk, lambda j: (layer, 0, 0, j)),
                  pl.BlockSpec(st_blk, lambda j: (layer, 0, 0, FFN_VAL_BLK + j)),
                  pl.BlockSpec(st_blk, lambda j: (layer, 0, 0, FFN_VAL_BLK + j + 1)),
                  g_blk(FFN_K), v_blk(FFN_K), g_blk(1), v_blk(1)],
        out_specs=[pl.BlockSpec((s, FFN_TN), lambda j: (0, j)),
                   pl.BlockSpec((s, FFN_TN), lambda j: (0, j)),
                   pl.BlockSpec((s, FFN_TN), lambda j: (0, j))],
        out_shape=[jax.ShapeDtypeStruct((s, D_FF_PAD), BF16),
                   jax.ShapeDtypeStruct((s, D_FF_PAD), F32),
                   jax.ShapeDtypeStruct((s, D_FF_PAD), F32)],
        scratch_shapes=[pltpu.VMEM((D_MODEL, FFN_TN), BF16),
                        pltpu.VMEM((D_MODEL, FFN_TN), BF16)],
        compiler_params=_cparams(1),
        name="ffn_up_sample",
    )(x_bf, lw["w_up"], lw["w_up"], lw["w_up"], ffn_state, ffn_state, ffn_state,
      lw["fconv_w"], lw["fconv_w"], lw["fconv_b"], lw["fconv_b"])


DOWN_TK = 512


def _down_ln_kernel(a_ref, w_ref, x_ref, g_ref, b_ref, o_ref, obf_ref):
    k = pl.program_id(1)

    @pl.when(k == 0)
    def _():
        o_ref[...] = ALPHA * x_ref[...]

    row = k * DOWN_TK + lax.broadcasted_iota(jnp.int32, (DOWN_TK, 1), 0)
    w = jnp.where(row < D_FF, w_ref[...], 0.0).astype(BF16)
    o_ref[...] += jnp.dot(a_ref[...], w, preferred_element_type=F32)

    @pl.when(k == pl.num_programs(1) - 1)
    def _():
        y = _layer_norm(o_ref[...], g_ref[...], b_ref[...])
        o_ref[...] = y
        obf_ref[...] = y.astype(BF16)


def _down_ln(a, w, layer, x, g, b, tm):
    m = a.shape[0]
    nk = D_FF_PAD // DOWN_TK
    return pl.pallas_call(
        _down_ln_kernel,
        grid=(m // tm, nk),
        in_specs=[pl.BlockSpec((tm, DOWN_TK), lambda i, k: (i, k)),
                  pl.BlockSpec((None, DOWN_TK, D_MODEL), lambda i, k: (layer, k, 0)),
                  pl.BlockSpec((tm, D_MODEL), lambda i, k: (i, 0), pipeline_mode=pl.Buffered(1)),
                  pl.BlockSpec((1, D_MODEL), lambda i, k: (0, 0)),
                  pl.BlockSpec((1, D_MODEL), lambda i, k: (0, 0))],
        out_specs=[pl.BlockSpec((tm, D_MODEL), lambda i, k: (i, 0)),
                   pl.BlockSpec((tm, D_MODEL), lambda i, k: (i, 0))],
        out_shape=[jax.ShapeDtypeStruct((m, D_MODEL), F32),
                   jax.ShapeDtypeStruct((m, D_MODEL), BF16)],
        compiler_params=_cparams(2),
        name="down_ln",
    )(a, w, x, g, b)


STATE_NS = 8


def _state_out_kernel(sc_ref, sp_ref, sf_ref, *rest):
    oc_ref, op_ref, of_ref = rest[-3:]
    new_rows = rest[:-3]
    layer = pl.program_id(0)
    oc_ref[0, :, 0:CONV_K - 2, :] = sc_ref[0, :, 1:CONV_K - 1, :]
    op_ref[0, :, 0:POOL_BUF - 1, :] = sp_ref[0, :, 1:POOL_BUF, :]
    of_ref[0, :, 0:FFN_K - 2, :] = sf_ref[0, :, 1:FFN_K - 1, :]
    for li in range(len(new_rows) // 5):
        xs, bc, po, hg, hv = new_rows[5 * li:5 * li + 5]

        @pl.when(layer == li)
        def _():
            oc_ref[0, :, CONV_K - 2, 0:D_INNER] = xs[...]
            oc_ref[0, :, CONV_K - 2, D_INNER:CONV_DIM] = bc[...]
            op_ref[0, :, POOL_BUF - 1, :] = po[...]
            of_ref[0, :, FFN_K - 2, 0:D_FF] = hg[:, 0:D_FF]
            of_ref[0, :, FFN_K - 2, D_FF:2 * D_FF] = hv[:, 0:D_FF]


def _state_out(s_conv, s_pool, s_ffn, per_layer):
    depth, s = s_conv.shape[:2]
    ns = STATE_NS
    blk = lambda a: pl.BlockSpec((1, ns) + a.shape[2:], lambda l, i: (l, i, 0, 0))
    rows = lambda w, cb: pl.BlockSpec((ns, w), lambda l, i: (i, cb))
    in_specs = [blk(s_conv), blk(s_pool), blk(s_ffn)]
    args = [s_conv, s_pool, s_ffn]
    for proj, h_g, h_v in per_layer:
        in_specs += [rows(D_INNER, OFF_XS // D_INNER), rows(BC_W, OFF_BC // BC_W),
                     rows(D_POOL, OFF_POOL // D_POOL), rows(D_FF_PAD, 0), rows(D_FF_PAD, 0)]
        args += [proj, proj, proj, h_g, h_v]
    return pl.pallas_call(
        _state_out_kernel,
        grid=(depth, s // ns),
        in_specs=in_specs,
        out_specs=[blk(s_conv), blk(s_pool), blk(s_ffn)],
        out_shape=[jax.ShapeDtypeStruct(a.shape, F32) for a in (s_conv, s_pool, s_ffn)],
        compiler_params=_cparams(2),
        name="state_out",
    )(*args)


def _pad_ff(v):
    pad = [(0, 0)] * (v.ndim - 1) + [(0, D_FF_PAD - D_FF)]
    return jnp.concatenate([jnp.pad(v[..., :D_FF], pad), jnp.pad(v[..., D_FF:], pad)], axis=-1)


def _unpad_ff(v):
    return jnp.concatenate([v[..., :D_FF], v[..., D_FF_PAD:D_FF_PAD + D_FF]], axis=-1)


def _prep_layer(big, b_gate, conv_w, conv_b, dt_bias, a_log, d_skip, norm_w, pool_scale,
                ln1_g, ln1_b, fconv_w, fconv_b, ln2_g, ln2_b):
    pad_h = lambda v: jnp.pad(v, (0, DT_PAD - N_HEADS)).reshape(1, DT_PAD)
    head_of_channel = jnp.arange(D_INNER) // HEAD_DIM
    head_expand = (jnp.arange(DT_PAD)[:, None] == head_of_channel[None, :]).astype(BF16)
    return dict(
        big, b_gate=b_gate.reshape(1, -1),
        conv_w=conv_w, conv_b=conv_b.reshape(1, -1),
        dt_bias=pad_h(dt_bias), a_log=pad_h(a_log),
        d_skip_x=jnp.repeat(d_skip, HEAD_DIM).reshape(1, -1),
        norm_w=norm_w.reshape(1, -1), head_expand=head_expand,
        pool_scale=pool_scale.reshape(1, -1),
        ln1_g=ln1_g.reshape(1, -1), ln1_b=ln1_b.reshape(1, -1),
        fconv_w=_pad_ff(fconv_w), fconv_b=_pad_ff(fconv_b).reshape(1, -1),
        ln2_g=ln2_g.reshape(1, -1), ln2_b=ln2_b.reshape(1, -1),
    )


def _raw_xbc(proj):
    return proj[:, OFF_XS:OFF_XS + CONV_DIM]


def _layer_prompt(x, x_bf, lw, layer, bsz, seq):
    tm_in = 1024
    proj, xbc_tail = _in_proj(x_bf, lw, layer, tm_in, seq // tm_in, conv=True)
    dt_raw = _dt_proj(x_bf, lw["w_in_t"], layer, 1024)
    y, new_ssm = _ssd_prompt(proj, dt_raw, lw, bsz, seq)
    y_b = _pool_prompt(proj, lw["w_pool"], layer, lw["pool_scale"], seq)
    mix = _branch_merge(y, lw["w_br"], layer, proj, y_b, 512, 1024)
    x1, x1_bf = _proj_ln(mix, lw["w_out"], layer, x, lw["ln1_g"], lw["ln1_b"], 512)
    tm_up = 1024
    act, tail_g, tail_v = _ffn_up_prompt(x1_bf, lw, layer, seq, tm_up)
    x2, x2_bf = _down_ln(act, lw["w_down"], layer, x1, lw["ln2_g"], lw["ln2_b"], 1024)
    p3 = proj.reshape(bsz, seq, N_MAIN)
    new_pool = p3[:, seq - POOL_BUF:, OFF_POOL:OFF_POOL + D_POOL]
    tps_in = seq // tm_in
    new_conv = xbc_tail[tps_in - 1::tps_in, CONV_HALO - (CONV_K - 1):, OFF_XS:OFF_XS + CONV_DIM]
    tps = seq // tm_up
    last = slice(tps - 1, None, tps)
    tail = jnp.concatenate([tail_g[last, :, :D_FF], tail_v[last, :, :D_FF]], axis=-1)
    new_ffn = tail[:, FFN_HALO - (FFN_K - 1):, :]
    return x2, x2_bf, new_ssm, new_conv, new_pool, new_ffn


def _layer_sample(x, x_bf, ssm_all, layer, ssm_prev_out, s_conv, s_pool, s_ffn, lw):
    s = x.shape[0]
    proj, _ = _in_proj(x_bf, lw, layer, s, 1, conv=False)
    dt_raw = _dt_proj(x_bf, lw["w_in_t"], layer, s)
    xs, b_m, c_m, decht, xdtt = _ssd_sample_prep(proj, dt_raw, jnp.swapaxes(s_conv, 0, 1), lw)
    new_ssm, yt = _ssd_sample_step(ssm_all, layer, ssm_prev_out, decht, xdtt, b_m, c_m)
    y = _ssd_sample_finish(yt, xs, proj, lw)
    pooled = _pool_sample(proj, s_pool, layer)
    y_b = _pool_mm(pooled, lw["w_pool"], layer, lw["pool_scale"], s)
    mix = _branch_merge(y, lw["w_br"], layer, proj, y_b, s, 1024)
    x1, x1_bf = _proj_ln(mix, lw["w_out"], layer, x, lw["ln1_g"], lw["ln1_b"], s)
    act, h_g, h_v = _ffn_up_sample(x1_bf, s_ffn, lw, layer)
    x2, x2_bf = _down_ln(act, lw["w_down"], layer, x1, lw["ln2_g"], lw["ln2_b"], s)
    return x2, x2_bf, new_ssm, (proj, h_g, h_v)


def kernel(x_prompt, x_sample, state_ssm, state_ssd_conv, state_pool, state_ffn_conv, w_in, b_gate, conv_w, conv_b, dt_bias, a_log, d_skip, ssd_norm_w, w_ssd_branch, w_pool, pool_scale, w_out, ln1_g, ln1_b, w_up, ffn_conv_w, ffn_conv_b, w_down, ln2_g, ln2_b):
    bsz, seq, _ = x_prompt.shape
    n_s = x_sample.shape[0]
    assert x_sample.shape[1] == 1 and seq % 1024 == 0
    xp = x_prompt.reshape(bsz * seq, D_MODEL)
    xs = x_sample.reshape(n_s, D_MODEL)
    xp_bf, xs_bf = xp.astype(BF16), xs.astype(BF16)
    outs_p, outs_s = [], []
    ssm_all = state_ssm.reshape(DEPTH, n_s, D_INNER, D_STATE)
    ssm_out = None
    big = dict(w_in_t=jnp.swapaxes(w_in, 1, 2), w_br=w_ssd_branch, w_pool=w_pool, w_out=w_out,
               w_up=w_up, w_down=w_down)
    for i in range(DEPTH):
        lw = _prep_layer(big, b_gate[i], conv_w[i], conv_b[i], dt_bias[i], a_log[i], d_skip[i],
                         ssd_norm_w[i], pool_scale[i], ln1_g[i], ln1_b[i], ffn_conv_w[i],
                         ffn_conv_b[i], ln2_g[i], ln2_b[i])
        xp, xp_bf, *op = _layer_prompt(xp, xp_bf, lw, i, bsz, seq)
        xs, xs_bf, ssm_out, new_rows = _layer_sample(xs, xs_bf, ssm_all, i, ssm_out, state_ssd_conv[i],
                                                     state_pool, state_ffn_conv, lw)
        outs_p.append(op)
        outs_s.append(new_rows)
    new_conv_s, new_pool_s, new_ffn_s = _state_out(state_ssd_conv, state_pool, state_ffn_conv, outs_s)
    stack = lambda outs, k: jnp.stack([o[k] for o in outs])
    return (xp.reshape(bsz, seq, D_MODEL), xs.reshape(n_s, 1, D_MODEL),
            stack(outs_p, 0), stack(outs_p, 1), stack(outs_p, 2), stack(outs_p, 3),
            ssm_out.reshape(state_ssm.shape), new_conv_s, new_pool_s, new_ffn_s)
```

```python
import functools

import jax
import jax.numpy as jnp
from jax import lax
from jax.experimental import pallas as pl
from jax.experimental.pallas import tpu as pltpu

F32 = jnp.float32
BF16 = jnp.bfloat16

D_MODEL = 2048
HEAD_DIM = 64
D_INNER = 2 * D_MODEL
N_HEADS = D_INNER // HEAD_DIM
N_GROUPS = 8
HEADS_PER_GROUP = N_HEADS // N_GROUPS
GROUP_W = D_INNER // N_GROUPS
D_STATE = 128
CONV_K = 4
BC_W = 2 * N_GROUPS * D_STATE
CONV_DIM = D_INNER + BC_W
CHUNK = 128
D_POOL = D_MODEL
POOL_WINDOWS = (2, 4, 8, 16)
POOL_GC = D_POOL // len(POOL_WINDOWS)
POOL_BUF = max(POOL_WINDOWS) - 1
D_FF = 5504
FFN_K = 3
DEPTH = 2
PAST_LEN = 16384
ALPHA = (2 * DEPTH) ** 0.25
LN_EPS = 1e-5
RMS_EPS = 1e-5

LANES = 128
SUBLANES = 8
D_FF_PAD = 5632
FFN_TN = 512
DT_PAD = LANES
OFF_Z = 0
OFF_XS = D_INNER
OFF_BC = 2 * D_INNER
OFF_POOL = 2 * D_INNER + BC_W
OFF_GATE = OFF_POOL + D_POOL
N_MAIN = OFF_GATE + 2 * D_MODEL
VMEM_LIMIT = 56 * 1024 * 1024
NEG_BIG = -1e30


def _cparams(n_axes):
    return pltpu.CompilerParams(dimension_semantics=("arbitrary",) * n_axes,
                                vmem_limit_bytes=VMEM_LIMIT)


NEG_LOG2E = -1.4426950408889634


def _sigmoid(x):
    return 1.0 / (1.0 + jnp.exp2(x * NEG_LOG2E))


def _silu(x):
    return x * _sigmoid(x)


def _softplus(x):
    return jnp.maximum(x, 0.0) + jnp.log(1.0 + jnp.exp(-jnp.abs(x)))


def _rows_above(e, shift):
    n, c = e.shape
    rot = pltpu.roll(e.reshape(n // SUBLANES, SUBLANES, c), shift, 1).reshape(n, c)
    sub = lax.broadcasted_iota(jnp.int32, (n - SUBLANES, c), 0) % SUBLANES
    return jnp.where(sub < shift, rot[0:n - SUBLANES, :], rot[SUBLANES:n, :])


def _shift_rows(ext, shift, rows):
    return ext[SUBLANES:SUBLANES + rows, :] if shift == 0 else _rows_above(ext, shift)


def _layer_norm(r, g, b):
    mu = jnp.mean(r, axis=-1, keepdims=True)
    d = r - mu
    var = jnp.mean(d * d, axis=-1, keepdims=True)
    return d * lax.rsqrt(var + LN_EPS) * g + b


def _dt_proj_kernel(x_ref, w_ref, o_ref):
    o_ref[...] = lax.dot_general(x_ref[...], w_ref[...].astype(BF16), (((1,), (1,)), ((), ())),
                                 preferred_element_type=F32)


def _dt_proj(x_bf, w_in_t, layer, tm):
    m = x_bf.shape[0]
    return pl.pallas_call(
        _dt_proj_kernel,
        grid=(m // tm,),
        in_specs=[pl.BlockSpec((tm, D_MODEL), lambda i: (i, 0)),
                  pl.BlockSpec((None, DT_PAD, D_MODEL), lambda i: (layer, DT_COL_BLOCK, 0))],
        out_specs=pl.BlockSpec((tm, DT_PAD), lambda i: (i, 0)),
        out_shape=jax.ShapeDtypeStruct((m, DT_PAD), F32),
        compiler_params=_cparams(1),
        name="dt_proj",
    )(x_bf, w_in_t)


IN_TN = 1024
J_XBC = OFF_XS // IN_TN
J_POOL = OFF_POOL // IN_TN
J_GATE = OFF_GATE // IN_TN
J_END = N_MAIN // IN_TN
CONV_HALO = SUBLANES
PG_SHIFT = N_HEADS
DT_COL_BLOCK = (2 * D_INNER + BC_W) // DT_PAD


def _in_proj_kernel(x_ref, wa_ref, wb_ref, cw_ref, cb_ref, bg_ref, o_ref, tail_ref, w_s, h_s,
                    *, tiles_per_seq, conv):
    j = pl.program_id(0)
    i = pl.program_id(1)
    tm = x_ref.shape[0]

    @pl.when((i == 0) & (j < J_POOL))
    def _():
        w_s[...] = wa_ref[...].astype(BF16)

    @pl.when((i == 0) & (j >= J_POOL))
    def _():
        w_s[...] = jnp.concatenate([wa_ref[PG_SHIFT:, :], wb_ref[:PG_SHIFT, :]], axis=0).astype(BF16)

    def mm():
        return lax.dot_general(x_ref[...], w_s[...], (((1,), (1,)), ((), ())),
                               preferred_element_type=F32)

    is_xbc = (j >= J_XBC) & (j < J_POOL)

    @pl.when(jnp.logical_not(is_xbc))
    def _():
        tail_ref[0] = jnp.zeros((CONV_HALO, IN_TN), F32)

    @pl.when(j < J_XBC)
    def _():
        o_ref[...] = _silu(mm())

    @pl.when(is_xbc)
    def _():
        if conv:
            @pl.when(i % tiles_per_seq == 0)
            def _():
                h_s[...] = jnp.zeros((CONV_HALO, IN_TN), F32)

            ext = jnp.concatenate([h_s[...], mm()], axis=0)
            acc = cb_ref[...]
            for k in range(CONV_K):
                acc = acc + _shift_rows(ext, CONV_K - 1 - k, tm) * cw_ref[k:k + 1, :]
            o_ref[...] = _silu(acc)
            tail = ext[tm:tm + CONV_HALO, :]
            h_s[...] = tail
            tail_ref[0] = tail
        else:
            o_ref[...] = mm()
            tail_ref[0] = jnp.zeros((CONV_HALO, IN_TN), F32)

    @pl.when((j >= J_POOL) & (j < J_GATE))
    def _():
        o_ref[...] = mm()

    @pl.when(j >= J_GATE)
    def _():
        o_ref[...] = _sigmoid(mm() + bg_ref[...])


def _in_proj(x_bf, lw, layer, tm, tiles_per_seq, conv):
    m = x_bf.shape[0]
    ni = m // tm
    wb_buffers = 2 if tm <= LANES else 1
    n_conv_tiles = CONV_DIM // IN_TN
    clamp = lambda v, lo, hi: jnp.minimum(jnp.maximum(v, lo), hi)
    conv_tile = lambda j: clamp(j - J_XBC, 0, n_conv_tiles - 1)
    kern = functools.partial(_in_proj_kernel, tiles_per_seq=tiles_per_seq, conv=conv)
    return pl.pallas_call(
        kern,
        grid=(J_END, ni),
        in_specs=[pl.BlockSpec((tm, D_MODEL), lambda j, i: (i, 0)),
                  pl.BlockSpec((None, IN_TN, D_MODEL), lambda j, i: (layer, j, 0)),
                  pl.BlockSpec((None, IN_TN, D_MODEL), lambda j, i: (layer, jnp.maximum(j + 1, J_POOL), 0),
                               pipeline_mode=pl.Buffered(wb_buffers)),
                  pl.BlockSpec((CONV_K, IN_TN), lambda j, i: (0, conv_tile(j))),
                  pl.BlockSpec((1, IN_TN), lambda j, i: (0, conv_tile(j))),
                  pl.BlockSpec((1, IN_TN), lambda j, i: (0, clamp(j - J_GATE, 0, J_END - J_GATE - 1)))],
        out_specs=[pl.BlockSpec((tm, IN_TN), lambda j, i: (i, j)),
                   pl.BlockSpec((1, CONV_HALO, IN_TN), lambda j, i: (i, 0, j))],
        out_shape=[jax.ShapeDtypeStruct((m, N_MAIN), F32),
                   jax.ShapeDtypeStruct((ni, CONV_HALO, N_MAIN), F32)],
        scratch_shapes=[pltpu.VMEM((IN_TN, D_MODEL), BF16),
                        pltpu.VMEM((CONV_HALO, IN_TN), F32)],
        compiler_params=_cparams(2),
        name="in_proj",
    )(x_bf, lw["w_in_t"], lw["w_in_t"], lw["conv_w"], lw["conv_b"], lw["b_gate"])


def _split3(v):
    hi = v.astype(BF16)
    r1 = v - hi.astype(F32)
    mid = r1.astype(BF16)
    lo = (r1 - mid.astype(F32)).astype(BF16)
    return hi, mid, lo


def _ssd_prompt_kernel(xs_ref, bc_ref, zs_ref, dt_ref, dtb_ref, alog_ref, dsk_ref, nw_ref,
                       y_ref, st_ref,
                       xs_s, bt_s, c_s, y_s, state_s, acol_s, arow_s, dtrow_s):
    c = pl.program_id(1)
    n_chunks = pl.num_programs(1)
    q = CHUNK

    @pl.when(c == 0)
    def _():
        state_s[...] = jnp.zeros(state_s.shape, F32)

    for g in range(N_GROUPS):
        xs_s[g] = xs_ref[:, g * GROUP_W:(g + 1) * GROUP_W]
        bt_s[g] = bc_ref[:, g * D_STATE:(g + 1) * D_STATE].T
        c_s[g] = bc_ref[:, (N_GROUPS + g) * D_STATE:(N_GROUPS + g + 1) * D_STATE].astype(BF16)

    dt = _softplus(dt_ref[...] + dtb_ref[...])
    a_neg = -jnp.exp(alog_ref[...])
    d_a = dt * a_neg
    row = lax.broadcasted_iota(jnp.int32, (q, q), 0)
    col = lax.broadcasted_iota(jnp.int32, (q, q), 1)
    causal = row >= col
    tril = jnp.where(causal, 1.0, 0.0).astype(BF16)
    hi, mid, lo3 = _split3(d_a)
    a_cum = (jnp.dot(tril, hi, preferred_element_type=F32)
             + jnp.dot(tril, mid, preferred_element_type=F32)
             + jnp.dot(tril, lo3, preferred_element_type=F32))
    arow_s[...] = a_cum.T
    dtrow_s[...] = dt.T
    for g in range(N_GROUPS):
        sh = (LANES - HEADS_PER_GROUP * g) % LANES
        acol_s[g] = a_cum if sh == 0 else pltpu.roll(a_cum, sh, 1)

    lane = lax.broadcasted_iota(jnp.int32, (q, LANES), 1)
    lo_half = lane < HEAD_DIM

    def group_body(g, carry):
        acol = acol_s[g]
        c_g = c_s[g]
        bt_g = bt_s[g]
        cb = jnp.dot(c_g, bt_g.astype(BF16), preferred_element_type=F32)
        y_off_g = jnp.dot(c_g, state_s[g].astype(BF16), preferred_element_type=F32)
        for k in range(HEADS_PER_GROUP // 2):
            l_parts, b_parts, a_b = [], [], []
            for e in range(2):
                hh = 2 * k + e
                head = g * HEADS_PER_GROUP + hh
                a_col = jnp.broadcast_to(acol[:, hh:hh + 1], (q, q))
                a_row = arow_s[pl.ds(head, 1), :]
                dt_row = dtrow_s[pl.ds(head, 1), :]
                seg = jnp.where(causal, a_col - a_row, NEG_BIG)
                l_parts.append((cb * jnp.exp(seg) * dt_row).astype(BF16))
                w_row = dt_row * jnp.exp(a_col[q - 1:q, :] - a_row)
                b_parts.append((bt_g * w_row).astype(BF16))
                a_b.append(a_col)
            lhs = jnp.concatenate([jnp.concatenate(l_parts, axis=1),
                                   jnp.concatenate(b_parts, axis=1)], axis=0)
            cols = slice(k * LANES, (k + 1) * LANES)
            xs_bf = xs_s[g, :, cols].astype(BF16)
            zero = jnp.zeros_like(xs_bf)
            rhs = jnp.concatenate([jnp.where(lo_half, xs_bf, zero),
                                   jnp.where(lo_half, zero, xs_bf)], axis=0)
            res = jnp.dot(lhs, rhs, preferred_element_type=F32)
            a_pair = jnp.where(lo_half, a_b[0], a_b[1])
            y_s[g, :, cols] = res[0:q] + y_off_g[:, cols] * jnp.exp(a_pair)
            cdec = jnp.exp(a_pair[q - 1:q, :])
            state_s[g, :, cols] = state_s[g, :, cols] * cdec + res[q:2 * q]
        return carry

    lax.fori_loop(0, N_GROUPS, group_body, 0)

    for g in range(N_GROUPS):
        cols = slice(g * GROUP_W, (g + 1) * GROUP_W)
        yv = y_s[g] + dsk_ref[:, cols] * xs_s[g]
        v = yv * zs_ref[:, cols]
        ms = jnp.mean(v * v, axis=-1, keepdims=True)
        y_ref[:, cols] = (v * lax.rsqrt(ms + RMS_EPS) * nw_ref[:, cols]).astype(y_ref.dtype)

    @pl.when(c == n_chunks - 1)
    def _():
        for g in range(N_GROUPS):
            st_ref[0, g] = state_s[g].T


def _ssd_prompt(proj, dt_raw, lw, bsz, seq):
    n_chunks = seq // CHUNK
    rows = lambda b, c: b * n_chunks + c
    small = lambda shape: pl.BlockSpec(shape, lambda b, c: (0, 0))
    y, st = pl.pallas_call(
        _ssd_prompt_kernel,
        grid=(bsz, n_chunks),
        in_specs=[
            pl.BlockSpec((CHUNK, D_INNER), lambda b, c: (rows(b, c), OFF_XS // D_INNER)),
            pl.BlockSpec((CHUNK, BC_W), lambda b, c: (rows(b, c), OFF_BC // BC_W)),
            pl.BlockSpec((CHUNK, D_INNER), lambda b, c: (rows(b, c), OFF_Z // D_INNER)),
            pl.BlockSpec((CHUNK, DT_PAD), lambda b, c: (rows(b, c), 0)),
            small((1, DT_PAD)), small((1, DT_PAD)), small((1, D_INNER)), small((1, D_INNER)),
        ],
        out_specs=[
            pl.BlockSpec((CHUNK, D_INNER), lambda b, c: (rows(b, c), 0)),
            pl.BlockSpec((1, N_GROUPS, GROUP_W, D_STATE), lambda b, c: (b, 0, 0, 0)),
        ],
        out_shape=[jax.ShapeDtypeStruct((bsz * seq, D_INNER), BF16),
                   jax.ShapeDtypeStruct((bsz, N_GROUPS, GROUP_W, D_STATE), F32)],
        scratch_shapes=[
            pltpu.VMEM((N_GROUPS, CHUNK, GROUP_W), F32),
            pltpu.VMEM((N_GROUPS, D_STATE, CHUNK), F32),
            pltpu.VMEM((N_GROUPS, CHUNK, D_STATE), BF16),
            pltpu.VMEM((N_GROUPS, CHUNK, GROUP_W), F32),
            pltpu.VMEM((N_GROUPS, D_STATE, GROUP_W), F32),
            pltpu.VMEM((N_GROUPS, CHUNK, LANES), F32),
            pltpu.VMEM((LANES, CHUNK), F32),
            pltpu.VMEM((LANES, CHUNK), F32),
        ],
        compiler_params=_cparams(2),
        name="ssd_prompt",
    )(proj, proj, proj, dt_raw, lw["dt_bias"], lw["a_log"], lw["d_skip_x"], lw["norm_w"])
    return y, st.reshape(bsz, N_HEADS, HEAD_DIM, D_STATE)


def _ssd_sample_prep_kernel(xs_ref, bc_ref, dt_ref, cst_ref, cw_ref, cb_ref, dtb_ref, alog_ref,
                            ex_ref, xs_o, b_o, c_o, decht_o, xdtt_o):
    def conv(u, lo, hi):
        acc = cb_ref[:, lo:hi]
        for j in range(CONV_K - 1):
            acc = acc + cst_ref[j][:, lo:hi] * cw_ref[j:j + 1, lo:hi]
        acc = acc + u * cw_ref[CONV_K - 1:CONV_K, lo:hi]
        return _silu(acc)

    xs = conv(xs_ref[...], 0, D_INNER)
    bc = conv(bc_ref[...], D_INNER, CONV_DIM)
    xs_o[...] = xs
    b_o[...] = bc[:, 0:N_GROUPS * D_STATE]
    c_o[...] = bc[:, N_GROUPS * D_STATE:BC_W]
    dt = _softplus(dt_ref[...] + dtb_ref[...])
    d_a = dt * (-jnp.exp(alog_ref[...]))
    ex = ex_ref[...]

    def expand(v):
        hi, mid, lo3 = _split3(v)
        return (jnp.dot(hi, ex, preferred_element_type=F32)
                + jnp.dot(mid, ex, preferred_element_type=F32)
                + jnp.dot(lo3, ex, preferred_element_type=F32))

    xdt = expand(dt) * xs
    decht_o[...] = jnp.exp(d_a).T
    xdtt_o[...] = xdt.T.astype(BF16)


def _ssd_sample_prep(proj, dt_raw, conv_state_t, lw):
    s = proj.shape[0]
    full = lambda shape: pl.BlockSpec(shape, lambda i: (0,) * len(shape))
    return pl.pallas_call(
        _ssd_sample_prep_kernel,
        grid=(1,),
        in_specs=[
            pl.BlockSpec((s, D_INNER), lambda i: (0, OFF_XS // D_INNER)),
            pl.BlockSpec((s, BC_W), lambda i: (0, OFF_BC // BC_W)),
            full((s, DT_PAD)), full((CONV_K - 1, s, CONV_DIM)),
            full((CONV_K, CONV_DIM)), full((1, CONV_DIM)), full((1, DT_PAD)), full((1, DT_PAD)),
            full((DT_PAD, D_INNER)),
        ],
        out_specs=[full((s, D_INNER)), full((s, N_GROUPS * D_STATE)), full((s, N_GROUPS * D_STATE)),
                   full((DT_PAD, s)), full((D_INNER, s))],
        out_shape=[jax.ShapeDtypeStruct((s, D_INNER), F32),
                   jax.ShapeDtypeStruct((s, N_GROUPS * D_STATE), F32),
                   jax.ShapeDtypeStruct((s, N_GROUPS * D_STATE), F32),
                   jax.ShapeDtypeStruct((DT_PAD, s), F32),
                   jax.ShapeDtypeStruct((D_INNER, s), BF16)],
        compiler_params=_cparams(1),
        name="ssd_sample_prep",
    )(proj, proj, dt_raw, conv_state_t, lw["conv_w"], lw["conv_b"], lw["dt_bias"], lw["a_log"],
      lw["head_expand"])


STEP_SAMPLES = 2


def _ssd_sample_step_kernel(st_ref, decht_ref, xdtt_ref, b_ref, c_ref, *rest, fill_other_layers):
    st_o, yt_o = rest[-2:]
    n_s = xdtt_ref.shape[1]

    @pl.when(pl.program_id(0) == 0)
    def _():
        yt_o[...] = jnp.zeros(yt_o.shape, F32)

    if fill_other_layers:
        st_o[1:] = jnp.zeros((st_o.shape[0] - 1,) + st_o.shape[1:], F32)

    for u in range(STEP_SAMPLES):
        s = pl.program_id(0) * STEP_SAMPLES + u
        onehot = lax.broadcasted_iota(jnp.int32, (GROUP_W, n_s), 1) == s
        onehot_h = lax.broadcasted_iota(jnp.int32, (DT_PAD, n_s), 1) == s
        pick = jnp.where(lax.broadcasted_iota(jnp.int32, (n_s, D_STATE), 0) == s, 1.0, 0.0).astype(BF16)
        dech = jnp.sum(jnp.where(onehot_h, decht_ref[...], 0.0), axis=-1, keepdims=True)
        b_all = b_ref[pl.ds(s, 1), :]
        c_all = c_ref[pl.ds(s, 1), :]
        for g in range(N_GROUPS):
            rows = slice(g * GROUP_W, (g + 1) * GROUP_W)
            cols = slice(g * D_STATE, (g + 1) * D_STATE)
            xdt = jnp.dot(xdtt_ref[rows, :], pick, preferred_element_type=F32)
            b_row = b_all[:, cols]
            c_row = c_all[:, cols]
            upd = xdt * b_row
            parts = []
            for hh in range(HEADS_PER_GROUP):
                h = g * HEADS_PER_GROUP + hh
                r_h = slice(h * HEAD_DIM, (h + 1) * HEAD_DIM)
                parts.append(st_ref[0, u, r_h, :] * dech[h:h + 1, :]
                             + upd[hh * HEAD_DIM:(hh + 1) * HEAD_DIM, :])
            st_new = jnp.concatenate(parts, axis=0)
            st_o[0, u, rows, :] = st_new
            y_col = jnp.sum(st_new * c_row, axis=-1, keepdims=True)
            yt_o[rows, :] = yt_o[rows, :] + jnp.where(onehot, y_col, 0.0)


def _ssd_sample_step(state_all, layer, prev_out, decht, xdtt, b_m, c_m):
    depth = state_all.shape[0]
    s = xdtt.shape[1]
    full = lambda shape: pl.BlockSpec(shape, lambda i: (0,) * len(shape))
    ns = STEP_SAMPLES
    in_specs = [pl.BlockSpec((1, ns, D_INNER, D_STATE), lambda i: (layer, i, 0, 0)),
                full((DT_PAD, s)), full((D_INNER, s)),
                full((s, N_GROUPS * D_STATE)), full((s, N_GROUPS * D_STATE))]
    args = [state_all, decht, xdtt, b_m, c_m]
    if prev_out is None:
        assert layer == 0
        aliases = {}
        st_out = pl.BlockSpec((depth, ns, D_INNER, D_STATE), lambda i: (0, i, 0, 0))
    else:
        in_specs.append(pl.BlockSpec(memory_space=pl.ANY))
        args.append(prev_out)
        aliases = {len(args) - 1: 0}
        st_out = pl.BlockSpec((1, ns, D_INNER, D_STATE), lambda i: (layer, i, 0, 0))
    return pl.pallas_call(
        functools.partial(_ssd_sample_step_kernel, fill_other_layers=prev_out is None),
        grid=(s // ns,),
        in_specs=in_specs,
        out_specs=[st_out, full((D_INNER, s))],
        out_shape=[jax.ShapeDtypeStruct(state_all.shape, F32),
                   jax.ShapeDtypeStruct((D_INNER, s), F32)],
        input_output_aliases=aliases,
        compiler_params=_cparams(1),
        name="ssd_sample_step",
    )(*args)


def _ssd_sample_finish_kernel(yt_ref, xs_ref, zs_ref, dsk_ref, nw_ref, y_ref):
    y = yt_ref[...].T
    for g in range(N_GROUPS):
        cols = slice(g * GROUP_W, (g + 1) * GROUP_W)
        yv = y[:, cols] + dsk_ref[:, cols] * xs_ref[:, cols]
        v = yv * zs_ref[:, cols]
        ms = jnp.mean(v * v, axis=-1, keepdims=True)
        y_ref[:, cols] = (v * lax.rsqrt(ms + RMS_EPS) * nw_ref[:, cols]).astype(y_ref.dtype)


def _ssd_sample_finish(yt, xs, proj, lw):
    s = xs.shape[0]
    full = lambda shape: pl.BlockSpec(shape, lambda i: (0,) * len(shape))
    return pl.pallas_call(
        _ssd_sample_finish_kernel,
        grid=(1,),
        in_specs=[full((D_INNER, s)), full((s, D_INNER)),
                  pl.BlockSpec((s, D_INNER), lambda i: (0, OFF_Z // D_INNER)),
                  full((1, D_INNER)), full((1, D_INNER))],
        out_specs=full((s, D_INNER)),
        out_shape=jax.ShapeDtypeStruct((s, D_INNER), BF16),
        compiler_params=_cparams(1),
        name="ssd_sample_finish",
    )(yt, xs, proj, lw["d_skip_x"], lw["norm_w"])


POOL_TP = 1024
POOL_HALO = 32


def _pool_prompt_kernel(u_ref, w_ref, sc_ref, o_ref, ext_s, w_s, *, tiles_per_seq):
    g = pl.program_id(0)
    i = pl.program_id(1)
    tp = POOL_TP

    @pl.when(i == 0)
    def _():
        w_s[...] = w_ref[...].astype(BF16)

    @pl.when(i % tiles_per_seq == 0)
    def _():
        ext_s[...] = jnp.zeros((POOL_HALO, POOL_GC), F32)

    u = u_ref[...]
    ext = jnp.concatenate([ext_s[...], u], axis=0)
    pos = (i % tiles_per_seq) * tp + lax.broadcasted_iota(jnp.int32, (tp, 1), 0)
    for gi, win in enumerate(POOL_WINDOWS):
        @pl.when(g == gi)
        def _():
            tot, width = ext, 1
            while width < win:
                if width < SUBLANES:
                    tot = tot[SUBLANES:, :] + _rows_above(tot, width)
                else:
                    tot = tot[width:, :] + tot[:-width, :]
                width *= 2
            tot = tot[tot.shape[0] - tp:, :]
            cnt = jnp.minimum(win, pos + 1).astype(F32)
            pooled = (tot / cnt - u).astype(BF16)
            o_ref[...] = jnp.dot(pooled, w_s[...], preferred_element_type=F32) * sc_ref[...]
    ext_s[...] = ext[tp:tp + POOL_HALO, :]


def _pool_prompt(proj, w_pool, layer, scale, seq):
    m = proj.shape[0]
    ng = len(POOL_WINDOWS)
    kern = functools.partial(_pool_prompt_kernel, tiles_per_seq=seq // POOL_TP)
    return pl.pallas_call(
        kern,
        grid=(ng, m // POOL_TP),
        in_specs=[pl.BlockSpec((POOL_TP, POOL_GC), lambda g, i: (i, OFF_POOL // POOL_GC + g)),
                  pl.BlockSpec((None, None, POOL_GC, POOL_GC), lambda g, i: (layer, g, 0, 0)),
                  pl.BlockSpec((1, POOL_GC), lambda g, i: (0, g))],
        out_specs=pl.BlockSpec((POOL_TP, POOL_GC), lambda g, i: (i, g)),
        out_shape=jax.ShapeDtypeStruct((m, D_POOL), F32),
        scratch_shapes=[pltpu.VMEM((POOL_HALO, POOL_GC), F32),
                        pltpu.VMEM((POOL_GC, POOL_GC), BF16)],
        compiler_params=_cparams(2),
        name="pool_prompt",
    )(proj, w_pool, scale)


def _pool_sample_kernel(u_ref, buf_ref, o_ref):
    g = pl.program_id(0)
    for gi, win in enumerate(POOL_WINDOWS):
        @pl.when(g == gi)
        def _():
            u = u_ref[...]
            tot = u
            for k in range(1, win):
                tot = tot + buf_ref[:, POOL_BUF - k, :]
            cnt = float(min(win, PAST_LEN + 1))
            o_ref[...] = (tot / cnt - u).astype(o_ref.dtype)


def _pool_sample(proj, pool_state, layer):
    s = proj.shape[0]
    return pl.pallas_call(
        _pool_sample_kernel,
        grid=(len(POOL_WINDOWS),),
        in_specs=[pl.BlockSpec((s, POOL_GC), lambda g: (0, OFF_POOL // POOL_GC + g)),
                  pl.BlockSpec((None, s, POOL_BUF, POOL_GC), lambda g: (layer, 0, 0, g))],
        out_specs=pl.BlockSpec((s, POOL_GC), lambda g: (0, g)),
        out_shape=jax.ShapeDtypeStruct((s, D_POOL), BF16),
        compiler_params=_cparams(1),
        name="pool_sample",
    )(proj, pool_state)


def _pool_mm_kernel(p_ref, w_ref, sc_ref, o_ref, w_s):
    @pl.when(pl.program_id(1) == 0)
    def _():
        w_s[...] = w_ref[...].astype(BF16)

    acc = jnp.dot(p_ref[...], w_s[...], preferred_element_type=F32)
    o_ref[...] = acc * sc_ref[...]


def _pool_mm(pooled, w_pool, layer, scale, tm):
    m = pooled.shape[0]
    ng = len(POOL_WINDOWS)
    return pl.pallas_call(
        _pool_mm_kernel,
        grid=(ng, m // tm),
        in_specs=[pl.BlockSpec((tm, POOL_GC), lambda g, i: (i, g)),
                  pl.BlockSpec((None, None, POOL_GC, POOL_GC), lambda g, i: (layer, g, 0, 0)),
                  pl.BlockSpec((1, POOL_GC), lambda g, i: (0, g))],
        out_specs=pl.BlockSpec((tm, POOL_GC), lambda g, i: (i, g)),
        out_shape=jax.ShapeDtypeStruct((m, D_POOL), F32),
        scratch_shapes=[pltpu.VMEM((POOL_GC, POOL_GC), BF16)],
        compiler_params=_cparams(2),
        name="pool_mm",
    )(pooled, w_pool, scale)


def _branch_merge_kernel(y_ref, w_ref, ga_ref, gb_ref, yb_ref, o_ref, w_s):
    @pl.when(pl.program_id(1) == 0)
    def _():
        w_s[...] = w_ref[...].astype(BF16)

    y_a = jnp.dot(y_ref[...], w_s[...], preferred_element_type=F32)
    o_ref[...] = (ga_ref[...] * y_a + gb_ref[...] * yb_ref[...]).astype(o_ref.dtype)


def _branch_merge(y, w_br, layer, proj, y_b, tm, tn):
    m = y.shape[0]
    nj = D_MODEL // tn
    ga0 = OFF_GATE // tn
    return pl.pallas_call(
        _branch_merge_kernel,
        grid=(nj, m // tm),
        in_specs=[pl.BlockSpec((tm, D_INNER), lambda j, i: (i, 0)),
                  pl.BlockSpec((None, D_INNER, tn), lambda j, i: (layer, 0, j),
                               pipeline_mode=pl.Buffered(1)),
                  pl.BlockSpec((tm, tn), lambda j, i: (i, ga0 + j)),
                  pl.BlockSpec((tm, tn), lambda j, i: (i, ga0 + nj + j)),
                  pl.BlockSpec((tm, tn), lambda j, i: (i, j))],
        out_specs=pl.BlockSpec((tm, tn), lambda j, i: (i, j)),
        out_shape=jax.ShapeDtypeStruct((m, D_MODEL), BF16),
        scratch_shapes=[pltpu.VMEM((D_INNER, tn), BF16)],
        compiler_params=_cparams(2),
        name="branch_merge",
    )(y, w_br, proj, proj, y_b)


def _proj_ln_kernel(m_ref, w_ref, x_ref, g_ref, b_ref, o_ref, obf_ref, w_s):
    @pl.when(pl.program_id(0) == 0)
    def _():
        w_s[...] = w_ref[...].astype(BF16)

    acc = jnp.dot(m_ref[...], w_s[...], preferred_element_type=F32)
    y = _layer_norm(ALPHA * x_ref[...] + acc, g_ref[...], b_ref[...])
    o_ref[...] = y
    obf_ref[...] = y.astype(BF16)


def _proj_ln(mix, w, layer, x, g, b, tm):
    m, k = mix.shape
    return pl.pallas_call(
        _proj_ln_kernel,
        grid=(m // tm,),
        in_specs=[pl.BlockSpec((tm, k), lambda i: (i, 0)),
                  pl.BlockSpec((None, k, D_MODEL), lambda i: (layer, 0, 0),
                               pipeline_mode=pl.Buffered(1)),
                  pl.BlockSpec((tm, D_MODEL), lambda i: (i, 0)),
                  pl.BlockSpec((1, D_MODEL), lambda i: (0, 0)),
                  pl.BlockSpec((1, D_MODEL), lambda i: (0, 0))],
        out_specs=[pl.BlockSpec((tm, D_MODEL), lambda i: (i, 0)),
                   pl.BlockSpec((tm, D_MODEL), lambda i: (i, 0))],
        out_shape=[jax.ShapeDtypeStruct((m, D_MODEL), F32),
                   jax.ShapeDtypeStruct((m, D_MODEL), BF16)],
        scratch_shapes=[pltpu.VMEM((k, D_MODEL), BF16)],
        compiler_params=_cparams(1),
        name="proj_ln",
    )(mix, w, x, g, b)


FFN_HALO = SUBLANES
FFN_VAL_BLK = D_FF // FFN_TN
FFN_VAL_SHIFT = D_FF % FFN_TN
assert FFN_VAL_SHIFT % LANES == 0


def _ffn_weight_tiles(j, wg_ref, wva_ref, wvb_ref, wg_s, wv_s):
    col = j * FFN_TN + lax.broadcasted_iota(jnp.int32, (1, FFN_TN), 1)
    valid = col < D_FF
    wg_s[...] = jnp.where(valid, wg_ref[...], 0.0).astype(BF16)
    wv = jnp.concatenate([wva_ref[:, FFN_VAL_SHIFT:], wvb_ref[:, :FFN_VAL_SHIFT]], axis=1)
    wv_s[...] = jnp.where(valid, wv, 0.0).astype(BF16)


def _ffn_w_specs(layer, index_of):
    blk = (None, D_MODEL, FFN_TN)
    return [pl.BlockSpec(blk, index_of(lambda j: (layer, 0, j))),
            pl.BlockSpec(blk, index_of(lambda j: (layer, 0, FFN_VAL_BLK + j))),
            pl.BlockSpec(blk, index_of(lambda j: (layer, 0, FFN_VAL_BLK + j + 1)))]


def _ffn_conv_gate(ext_g, ext_v, cw_g, cw_v, cb_g, cb_v, tm):
    def conv(ext, cw, cb):
        acc = cb[...]
        for j in range(FFN_K):
            acc = acc + _shift_rows(ext, FFN_K - 1 - j, tm) * cw[j:j + 1, :]
        return acc
    return _silu(conv(ext_g, cw_g, cb_g)) * conv(ext_v, cw_v, cb_v)


def _ffn_up_prompt_kernel(x_ref, wg_ref, wva_ref, wvb_ref, cwg_ref, cwv_ref, cbg_ref, cbv_ref,
                          a_ref, tg_ref, tv_ref, wg_s, wv_s, hg_s, hv_s, *, tiles_per_seq):
    j = pl.program_id(0)
    i = pl.program_id(1)
    tm = x_ref.shape[0]

    @pl.when(i == 0)
    def _():
        _ffn_weight_tiles(j, wg_ref, wva_ref, wvb_ref, wg_s, wv_s)

    @pl.when(i % tiles_per_seq == 0)
    def _():
        hg_s[...] = jnp.zeros((FFN_HALO, FFN_TN), F32)
        hv_s[...] = jnp.zeros((FFN_HALO, FFN_TN), F32)

    x = x_ref[...]
    ext_g = jnp.concatenate([hg_s[...], jnp.dot(x, wg_s[...], preferred_element_type=F32)], axis=0)
    ext_v = jnp.concatenate([hv_s[...], jnp.dot(x, wv_s[...], preferred_element_type=F32)], axis=0)
    a_ref[...] = _ffn_conv_gate(ext_g, ext_v, cwg_ref, cwv_ref, cbg_ref, cbv_ref, tm).astype(a_ref.dtype)
    tail_g = ext_g[tm:tm + FFN_HALO, :]
    tail_v = ext_v[tm:tm + FFN_HALO, :]
    hg_s[...] = tail_g
    hv_s[...] = tail_v
    tg_ref[0] = tail_g
    tv_ref[0] = tail_v


def _ffn_up_prompt(x_bf, lw, layer, seq, tm):
    m = x_bf.shape[0]
    nj = D_FF_PAD // FFN_TN
    ni = m // tm
    kern = functools.partial(_ffn_up_prompt_kernel, tiles_per_seq=seq // tm)
    cspec_g = lambda r: pl.BlockSpec((r, FFN_TN), lambda j, i: (0, j))
    cspec_v = lambda r: pl.BlockSpec((r, FFN_TN), lambda j, i: (0, nj + j))
    w_specs = _ffn_w_specs(layer, lambda f: (lambda j, i: f(j)))
    return pl.pallas_call(
        kern,
        grid=(nj, ni),
        in_specs=[pl.BlockSpec((tm, D_MODEL), lambda j, i: (i, 0)), *w_specs,
                  cspec_g(FFN_K), cspec_v(FFN_K), cspec_g(1), cspec_v(1)],
        out_specs=[pl.BlockSpec((tm, FFN_TN), lambda j, i: (i, j)),
                   pl.BlockSpec((1, FFN_HALO, FFN_TN), lambda j, i: (i, 0, j)),
                   pl.BlockSpec((1, FFN_HALO, FFN_TN), lambda j, i: (i, 0, j))],
        out_shape=[jax.ShapeDtypeStruct((m, D_FF_PAD), BF16),
                   jax.ShapeDtypeStruct((ni, FFN_HALO, D_FF_PAD), F32),
                   jax.ShapeDtypeStruct((ni, FFN_HALO, D_FF_PAD), F32)],
        scratch_shapes=[pltpu.VMEM((D_MODEL, FFN_TN), BF16),
                        pltpu.VMEM((D_MODEL, FFN_TN), BF16),
                        pltpu.VMEM((FFN_HALO, FFN_TN), F32),
                        pltpu.VMEM((FFN_HALO, FFN_TN), F32)],
        compiler_params=_cparams(2),
        name="ffn_up_prompt",
    )(x_bf, lw["w_up"], lw["w_up"], lw["w_up"],
      lw["fconv_w"], lw["fconv_w"], lw["fconv_b"], lw["fconv_b"])


def _ffn_up_sample_kernel(x_ref, wg_ref, wva_ref, wvb_ref, sg_ref, sva_ref, svb_ref, cwg_ref, cwv_ref,
                          cbg_ref, cbv_ref, a_ref, hg_ref, hv_ref, wg_s, wv_s):
    j = pl.program_id(0)
    _ffn_weight_tiles(j, wg_ref, wva_ref, wvb_ref, wg_s, wv_s)
    x = x_ref[...]
    hg = jnp.dot(x, wg_s[...], preferred_element_type=F32)
    hv = jnp.dot(x, wv_s[...], preferred_element_type=F32)
    hg_ref[...] = hg
    hv_ref[...] = hv
    valid = j * FFN_TN + lax.broadcasted_iota(jnp.int32, (1, FFN_TN), 1) < D_FF

    def conv(h, rows, cw, cb):
        acc = cb[...]
        for k in range(FFN_K - 1):
            acc = acc + jnp.where(valid, rows(k), 0.0) * cw[k:k + 1, :]
        return acc + h * cw[FFN_K - 1:FFN_K, :]

    gate_rows = lambda k: sg_ref[:, k, :]
    value_rows = lambda k: jnp.concatenate([sva_ref[:, k, FFN_VAL_SHIFT:],
                                            svb_ref[:, k, :FFN_VAL_SHIFT]], axis=1)
    a_ref[...] = (_silu(conv(hg, gate_rows, cwg_ref, cbg_ref))
                  * conv(hv, value_rows, cwv_ref, cbv_ref)).astype(a_ref.dtype)


def _ffn_up_sample(x_bf, ffn_state, lw, layer):
    s = x_bf.shape[0]
    nj = D_FF_PAD // FFN_TN
    g_blk = lambda r: pl.BlockSpec((r, FFN_TN), lambda j: (0, j))
    v_blk = lambda r: pl.BlockSpec((r, FFN_TN), lambda j: (0, nj + j))
    w_specs = _ffn_w_specs(layer, lambda f: f)
    st_blk = (None, s, FFN_K - 1, FFN_TN)
    return pl.pallas_call(
        _ffn_up_sample_kernel,
        grid=(nj,),
        in_specs=[pl.BlockSpec((s, D_MODEL), lambda j: (0, 0)), *w_specs,
                  pl.BlockSpec(st_blk, lambda j: (layer, 0, 0, j)),
                  pl.BlockSpec(st_blk, lambda j: (layer, 0, 0, FFN_VAL_BLK + j)),
                  pl.BlockSpec(st_blk, lambda j: (layer, 0, 0, FFN_VAL_BLK + j + 1)),
                  g_blk(FFN_K), v_blk(FFN_K), g_blk(1), v_blk(1)],
        out_specs=[pl.BlockSpec((s, FFN_TN), lambda j: (0, j)),
                   pl.BlockSpec((s, FFN_TN), lambda j: (0, j)),
                   pl.BlockSpec((s, FFN_TN), lambda j: (0, j))],
        out_shape=[jax.ShapeDtypeStruct((s, D_FF_PAD), BF16),
                   jax.ShapeDtypeStruct((s, D_FF_PAD), F32),
                   jax.ShapeDtypeStruct((s, D_FF_PAD), F32)],
        scratch_shapes=[pltpu.VMEM((D_MODEL, FFN_TN), BF16),
                        pltpu.VMEM((D_MODEL, FFN_TN), BF16)],
        compiler_params=_cparams(1),
        name="ffn_up_sample",
    )(x_bf, lw["w_up"], lw["w_up"], lw["w_up"], ffn_state, ffn_state, ffn_state,
      lw["fconv_w"], lw["fconv_w"], lw["fconv_b"], lw["fconv_b"])


DOWN_TK = 512


def _down_ln_kernel(a_ref, w_ref, x_ref, g_ref, b_ref, o_ref, obf_ref):
    k = pl.program_id(1)

    @pl.when(k == 0)
    def _():
        o_ref[...] = ALPHA * x_ref[...]

    row = k * DOWN_TK + lax.broadcasted_iota(jnp.int32, (DOWN_TK, 1), 0)
    w = jnp.where(row < D_FF, w_ref[...], 0.0).astype(BF16)
    o_ref[...] += jnp.dot(a_ref[...], w, preferred_element_type=F32)

    @pl.when(k == pl.num_programs(1) - 1)
    def _():
        y = _layer_norm(o_ref[...], g_ref[...], b_ref[...])
        o_ref[...] = y
        obf_ref[...] = y.astype(BF16)


def _down_ln(a, w, layer, x, g, b, tm):
    m = a.shape[0]
    nk = D_FF_PAD // DOWN_TK
    return pl.pallas_call(
        _down_ln_kernel,
        grid=(m // tm, nk),
        in_specs=[pl.BlockSpec((tm, DOWN_TK), lambda i, k: (i, k)),
                  pl.BlockSpec((None, DOWN_TK, D_MODEL), lambda i, k: (layer, k, 0)),
                  pl.BlockSpec((tm, D_MODEL), lambda i, k: (i, 0)),
                  pl.BlockSpec((1, D_MODEL), lambda i, k: (0, 0)),
                  pl.BlockSpec((1, D_MODEL), lambda i, k: (0, 0))],
        out_specs=[pl.BlockSpec((tm, D_MODEL), lambda i, k: (i, 0)),
                   pl.BlockSpec((tm, D_MODEL), lambda i, k: (i, 0))],
        out_shape=[jax.ShapeDtypeStruct((m, D_MODEL), F32),
                   jax.ShapeDtypeStruct((m, D_MODEL), BF16)],
        compiler_params=_cparams(2),
        name="down_ln",
    )(a, w, x, g, b)


STATE_NS = 16


def _state_out_kernel(sc_ref, sp_ref, sf_ref, *rest):
    oc_ref, op_ref, of_ref = rest[-3:]
    new_rows = rest[:-3]
    layer = pl.program_id(0)
    oc_ref[0, :, 0:CONV_K - 2, :] = sc_ref[0, :, 1:CONV_K - 1, :]
    op_ref[0, :, 0:POOL_BUF - 1, :] = sp_ref[0, :, 1:POOL_BUF, :]
    of_ref[0, :, 0:FFN_K - 2, :] = sf_ref[0, :, 1:FFN_K - 1, :]
    for li in range(len(new_rows) // 5):
        xs, bc, po, hg, hv = new_rows[5 * li:5 * li + 5]

        @pl.when(layer == li)
        def _():
            oc_ref[0, :, CONV_K - 2, 0:D_INNER] = xs[...]
            oc_ref[0, :, CONV_K - 2, D_INNER:CONV_DIM] = bc[...]
            op_ref[0, :, POOL_BUF - 1, :] = po[...]
            of_ref[0, :, FFN_K - 2, 0:D_FF] = hg[:, 0:D_FF]
            of_ref[0, :, FFN_K - 2, D_FF:2 * D_FF] = hv[:, 0:D_FF]


def _state_out(s_conv, s_pool, s_ffn, per_layer):
    depth, s = s_conv.shape[:2]
    ns = STATE_NS
    blk = lambda a: pl.BlockSpec((1, ns) + a.shape[2:], lambda l, i: (l, i, 0, 0))
    rows = lambda w, cb: pl.BlockSpec((ns, w), lambda l, i: (i, cb))
    in_specs = [blk(s_conv), blk(s_pool), blk(s_ffn)]
    args = [s_conv, s_pool, s_ffn]
    for proj, h_g, h_v in per_layer:
        in_specs += [rows(D_INNER, OFF_XS // D_INNER), rows(BC_W, OFF_BC // BC_W),
                     rows(D_POOL, OFF_POOL // D_POOL), rows(D_FF_PAD, 0), rows(D_FF_PAD, 0)]
        args += [proj, proj, proj, h_g, h_v]
    return pl.pallas_call(
        _state_out_kernel,
        grid=(depth, s // ns),
        in_specs=in_specs,
        out_specs=[blk(s_conv), blk(s_pool), blk(s_ffn)],
        out_shape=[jax.ShapeDtypeStruct(a.shape, F32) for a in (s_conv, s_pool, s_ffn)],
        compiler_params=_cparams(2),
        name="state_out",
    )(*args)


def _pad_ff(v):
    pad = [(0, 0)] * (v.ndim - 1) + [(0, D_FF_PAD - D_FF)]
    return jnp.concatenate([jnp.pad(v[..., :D_FF], pad), jnp.pad(v[..., D_FF:], pad)], axis=-1)


def _unpad_ff(v):
    return jnp.concatenate([v[..., :D_FF], v[..., D_FF_PAD:D_FF_PAD + D_FF]], axis=-1)


def _prep_layer(big, b_gate, conv_w, conv_b, dt_bias, a_log, d_skip, norm_w, pool_scale,
                ln1_g, ln1_b, fconv_w, fconv_b, ln2_g, ln2_b):
    pad_h = lambda v: jnp.pad(v, (0, DT_PAD - N_HEADS)).reshape(1, DT_PAD)
    head_of_channel = jnp.arange(D_INNER) // HEAD_DIM
    head_expand = (jnp.arange(DT_PAD)[:, None] == head_of_channel[None, :]).astype(BF16)
    return dict(
        big, b_gate=b_gate.reshape(1, -1),
        conv_w=conv_w, conv_b=conv_b.reshape(1, -1),
        dt_bias=pad_h(dt_bias), a_log=pad_h(a_log),
        d_skip_x=jnp.repeat(d_skip, HEAD_DIM).reshape(1, -1),
        norm_w=norm_w.reshape(1, -1), head_expand=head_expand,
        pool_scale=pool_scale.reshape(1, -1),
        ln1_g=ln1_g.reshape(1, -1), ln1_b=ln1_b.reshape(1, -1),
        fconv_w=_pad_ff(fconv_w), fconv_b=_pad_ff(fconv_b).reshape(1, -1),
        ln2_g=ln2_g.reshape(1, -1), ln2_b=ln2_b.reshape(1, -1),
    )


def _raw_xbc(proj):
    return proj[:, OFF_XS:OFF_XS + CONV_DIM]


def _layer_prompt(x, x_bf, lw, layer, bsz, seq):
    tm_in = 1024
    proj, xbc_tail = _in_proj(x_bf, lw, layer, tm_in, seq // tm_in, conv=True)
    dt_raw = _dt_proj(x_bf, lw["w_in_t"], layer, 1024)
    y, new_ssm = _ssd_prompt(proj, dt_raw, lw, bsz, seq)
    y_b = _pool_prompt(proj, lw["w_pool"], layer, lw["pool_scale"], seq)
    mix = _branch_merge(y, lw["w_br"], layer, proj, y_b, 512, 1024)
    x1, x1_bf = _proj_ln(mix, lw["w_out"], layer, x, lw["ln1_g"], lw["ln1_b"], 512)
    tm_up = 1024
    act, tail_g, tail_v = _ffn_up_prompt(x1_bf, lw, layer, seq, tm_up)
    x2, x2_bf = _down_ln(act, lw["w_down"], layer, x1, lw["ln2_g"], lw["ln2_b"], 1024)
    p3 = proj.reshape(bsz, seq, N_MAIN)
    new_pool = p3[:, seq - POOL_BUF:, OFF_POOL:OFF_POOL + D_POOL]
    tps_in = seq // tm_in
    new_conv = xbc_tail[tps_in - 1::tps_in, CONV_HALO - (CONV_K - 1):, OFF_XS:OFF_XS + CONV_DIM]
    tps = seq // tm_up
    last = slice(tps - 1, None, tps)
    tail = jnp.concatenate([tail_g[last, :, :D_FF], tail_v[last, :, :D_FF]], axis=-1)
    new_ffn = tail[:, FFN_HALO - (FFN_K - 1):, :]
    return x2, x2_bf, new_ssm, new_conv, new_pool, new_ffn


def _layer_sample(x, x_bf, ssm_all, layer, ssm_prev_out, s_conv, s_pool, s_ffn, lw):
    s = x.shape[0]
    proj, _ = _in_proj(x_bf, lw, layer, s, 1, conv=False)
    dt_raw = _dt_proj(x_bf, lw["w_in_t"], layer, s)
    xs, b_m, c_m, decht, xdtt = _ssd_sample_prep(proj, dt_raw, jnp.swapaxes(s_conv, 0, 1), lw)
    new_ssm, yt = _ssd_sample_step(ssm_all, layer, ssm_prev_out, decht, xdtt, b_m, c_m)
    y = _ssd_sample_finish(yt, xs, proj, lw)
    pooled = _pool_sample(proj, s_pool, layer)
    y_b = _pool_mm(pooled, lw["w_pool"], layer, lw["pool_scale"], s)
    mix = _branch_merge(y, lw["w_br"], layer, proj, y_b, s, 1024)
    x1, x1_bf = _proj_ln(mix, lw["w_out"], layer, x, lw["ln1_g"], lw["ln1_b"], s)
    act, h_g, h_v = _ffn_up_sample(x1_bf, s_ffn, lw, layer)
    x2, x2_bf = _down_ln(act, lw["w_down"], layer, x1, lw["ln2_g"], lw["ln2_b"], s)
    return x2, x2_bf, new_ssm, (proj, h_g, h_v)


def kernel(x_prompt, x_sample, state_ssm, state_ssd_conv, state_pool, state_ffn_conv, w_in, b_gate, conv_w, conv_b, dt_bias, a_log, d_skip, ssd_norm_w, w_ssd_branch, w_pool, pool_scale, w_out, ln1_g, ln1_b, w_up, ffn_conv_w, ffn_conv_b, w_down, ln2_g, ln2_b):
    bsz, seq, _ = x_prompt.shape
    n_s = x_sample.shape[0]
    assert x_sample.shape[1] == 1 and seq % 1024 == 0
    xp = x_prompt.reshape(bsz * seq, D_MODEL)
    xs = x_sample.reshape(n_s, D_MODEL)
    xp_bf, xs_bf = xp.astype(BF16), xs.astype(BF16)
    outs_p, outs_s = [], []
    ssm_all = state_ssm.reshape(DEPTH, n_s, D_INNER, D_STATE)
    ssm_out = None
    big = dict(w_in_t=jnp.swapaxes(w_in, 1, 2), w_br=w_ssd_branch, w_pool=w_pool, w_out=w_out,
               w_up=w_up, w_down=w_down)
    for i in range(DEPTH):
        lw = _prep_layer(big, b_gate[i], conv_w[i], conv_b[i], dt_bias[i], a_log[i], d_skip[i],
                         ssd_norm_w[i], pool_scale[i], ln1_g[i], ln1_b[i], ffn_conv_w[i],
                         ffn_conv_b[i], ln2_g[i], ln2_b[i])
        xp, xp_bf, *op = _layer_prompt(xp, xp_bf, lw, i, bsz, seq)
        xs, xs_bf, ssm_out, new_rows = _layer_sample(xs, xs_bf, ssm_all, i, ssm_out, state_ssd_conv[i],
                                                     state_pool, state_ffn_conv, lw)
        outs_p.append(op)
        outs_s.append(new_rows)
    new_conv_s, new_pool_s, new_ffn_s = _state_out(state_ssd_conv, state_pool, state_ffn_conv, outs_s)
    stack = lambda outs, k: jnp.stack([o[k] for o in outs])
    return (xp.reshape(bsz, seq, D_MODEL), xs.reshape(n_s, 1, D_MODEL),
            stack(outs_p, 0), stack(outs_p, 1), stack(outs_p, 2), stack(outs_p, 3),
            ssm_out.reshape(state_ssm.shape), new_conv_s, new_pool_s, new_ffn_s)
```

```python
import functools

import jax
import jax.numpy as jnp
from jax import lax
from jax.experimental import pallas as pl
from jax.experimental.pallas import tpu as pltpu

F32 = jnp.float32
BF16 = jnp.bfloat16

D_MODEL = 2048
HEAD_DIM = 64
D_INNER = 2 * D_MODEL
N_HEADS = D_INNER // HEAD_DIM
N_GROUPS = 8
HEADS_PER_GROUP = N_HEADS // N_GROUPS
GROUP_W = D_INNER // N_GROUPS
D_STATE = 128
CONV_K = 4
BC_W = 2 * N_GROUPS * D_STATE
CONV_DIM = D_INNER + BC_W
CHUNK = 128
D_POOL = D_MODEL
POOL_WINDOWS = (2, 4, 8, 16)
POOL_GC = D_POOL // len(POOL_WINDOWS)
POOL_BUF = max(POOL_WINDOWS) - 1
D_FF = 5504
FFN_K = 3
DEPTH = 2
PAST_LEN = 16384
ALPHA = (2 * DEPTH) ** 0.25
LN_EPS = 1e-5
RMS_EPS = 1e-5

LANES = 128
SUBLANES = 8
D_FF_PAD = 5632
FFN_TN = 512
DT_PAD = LANES
OFF_Z = 0
OFF_XS = D_INNER
OFF_BC = 2 * D_INNER
OFF_POOL = 2 * D_INNER + BC_W
OFF_GATE = OFF_POOL + D_POOL
N_MAIN = OFF_GATE + 2 * D_MODEL
VMEM_LIMIT = 56 * 1024 * 1024
NEG_BIG = -1e30


def _cparams(n_axes):
    return pltpu.CompilerParams(dimension_semantics=("arbitrary",) * n_axes,
                                vmem_limit_bytes=VMEM_LIMIT)


NEG_LOG2E = -1.4426950408889634


def _sigmoid(x):
    return 1.0 / (1.0 + jnp.exp2(x * NEG_LOG2E))


def _silu(x):
    return x * _sigmoid(x)


def _softplus(x):
    return jnp.maximum(x, 0.0) + jnp.log(1.0 + jnp.exp(-jnp.abs(x)))


def _rows_above(e, shift):
    n, c = e.shape
    rot = pltpu.roll(e.reshape(n // SUBLANES, SUBLANES, c), shift, 1).reshape(n, c)
    sub = lax.broadcasted_iota(jnp.int32, (n - SUBLANES, c), 0) % SUBLANES
    return jnp.where(sub < shift, rot[0:n - SUBLANES, :], rot[SUBLANES:n, :])


def _shift_rows(ext, shift, rows):
    return ext[SUBLANES:SUBLANES + rows, :] if shift == 0 else _rows_above(ext, shift)


def _layer_norm(r, g, b):
    mu = jnp.mean(r, axis=-1, keepdims=True)
    d = r - mu
    var = jnp.mean(d * d, axis=-1, keepdims=True)
    return d * lax.rsqrt(var + LN_EPS) * g + b


def _dt_proj_kernel(x_ref, w_ref, o_ref):
    o_ref[...] = lax.dot_general(x_ref[...], w_ref[...].astype(BF16), (((1,), (1,)), ((), ())),
                                 preferred_element_type=F32)


def _dt_proj(x_bf, w_in_t, layer, tm):
    m = x_bf.shape[0]
    return pl.pallas_call(
        _dt_proj_kernel,
        grid=(m // tm,),
        in_specs=[pl.BlockSpec((tm, D_MODEL), lambda i: (i, 0)),
                  pl.BlockSpec((None, DT_PAD, D_MODEL), lambda i: (layer, DT_COL_BLOCK, 0))],
        out_specs=pl.BlockSpec((tm, DT_PAD), lambda i: (i, 0)),
        out_shape=jax.ShapeDtypeStruct((m, DT_PAD), F32),
        compiler_params=_cparams(1),
        name="dt_proj",
    )(x_bf, w_in_t)


IN_TN = 1024
J_XBC = OFF_XS // IN_TN
J_POOL = OFF_POOL // IN_TN
J_GATE = OFF_GATE // IN_TN
J_END = N_MAIN // IN_TN
CONV_HALO = SUBLANES
PG_SHIFT = N_HEADS
DT_COL_BLOCK = (2 * D_INNER + BC_W) // DT_PAD


def _in_proj_kernel(x_ref, wa_ref, wb_ref, cw_ref, cb_ref, bg_ref, o_ref, tail_ref, w_s, h_s,
                    *, tiles_per_seq, conv):
    j = pl.program_id(0)
    i = pl.program_id(1)
    tm = x_ref.shape[0]

    @pl.when((i == 0) & (j < J_POOL))
    def _():
        w_s[...] = wa_ref[...].astype(BF16)

    @pl.when((i == 0) & (j >= J_POOL))
    def _():
        w_s[...] = jnp.concatenate([wa_ref[PG_SHIFT:, :], wb_ref[:PG_SHIFT, :]], axis=0).astype(BF16)

    def mm():
        return lax.dot_general(x_ref[...], w_s[...], (((1,), (1,)), ((), ())),
                               preferred_element_type=F32)

    is_xbc = (j >= J_XBC) & (j < J_POOL)

    @pl.when(jnp.logical_not(is_xbc))
    def _():
        tail_ref[0] = jnp.zeros((CONV_HALO, IN_TN), F32)

    @pl.when(j < J_XBC)
    def _():
        o_ref[...] = _silu(mm())

    @pl.when(is_xbc)
    def _():
        if conv:
            @pl.when(i % tiles_per_seq == 0)
            def _():
                h_s[...] = jnp.zeros((CONV_HALO, IN_TN), F32)

            ext = jnp.concatenate([h_s[...], mm()], axis=0)
            acc = cb_ref[...]
            for k in range(CONV_K):
                acc = acc + _shift_rows(ext, CONV_K - 1 - k, tm) * cw_ref[k:k + 1, :]
            o_ref[...] = _silu(acc)
            tail = ext[tm:tm + CONV_HALO, :]
            h_s[...] = tail
            tail_ref[0] = tail
        else:
            o_ref[...] = mm()
            tail_ref[0] = jnp.zeros((CONV_HALO, IN_TN), F32)

    @pl.when((j >= J_POOL) & (j < J_GATE))
    def _():
        o_ref[...] = mm()

    @pl.when(j >= J_GATE)
    def _():
        o_ref[...] = _sigmoid(mm() + bg_ref[...])


def _in_proj(x_bf, lw, layer, tm, tiles_per_seq, conv):
    m = x_bf.shape[0]
    ni = m // tm
    wb_buffers = 2 if tm <= LANES else 1
    n_conv_tiles = CONV_DIM // IN_TN
    clamp = lambda v, lo, hi: jnp.minimum(jnp.maximum(v, lo), hi)
    conv_tile = lambda j: clamp(j - J_XBC, 0, n_conv_tiles - 1)
    kern = functools.partial(_in_proj_kernel, tiles_per_seq=tiles_per_seq, conv=conv)
    return pl.pallas_call(
        kern,
        grid=(J_END, ni),
        in_specs=[pl.BlockSpec((tm, D_MODEL), lambda j, i: (i, 0)),
                  pl.BlockSpec((None, IN_TN, D_MODEL), lambda j, i: (layer, j, 0)),
                  pl.BlockSpec((None, IN_TN, D_MODEL), lambda j, i: (layer, jnp.maximum(j + 1, J_POOL), 0),
                               pipeline_mode=pl.Buffered(wb_buffers)),
                  pl.BlockSpec((CONV_K, IN_TN), lambda j, i: (0, conv_tile(j))),
                  pl.BlockSpec((1, IN_TN), lambda j, i: (0, conv_tile(j))),
                  pl.BlockSpec((1, IN_TN), lambda j, i: (0, clamp(j - J_GATE, 0, J_END - J_GATE - 1)))],
        out_specs=[pl.BlockSpec((tm, IN_TN), lambda j, i: (i, j)),
                   pl.BlockSpec((1, CONV_HALO, IN_TN), lambda j, i: (i, 0, j))],
        out_shape=[jax.ShapeDtypeStruct((m, N_MAIN), F32),
                   jax.ShapeDtypeStruct((ni, CONV_HALO, N_MAIN), F32)],
        scratch_shapes=[pltpu.VMEM((IN_TN, D_MODEL), BF16),
                        pltpu.VMEM((CONV_HALO, IN_TN), F32)],
        compiler_params=_cparams(2),
        name="in_proj",
    )(x_bf, lw["w_in_t"], lw["w_in_t"], lw["conv_w"], lw["conv_b"], lw["b_gate"])


def _split3(v):
    hi = v.astype(BF16)
    r1 = v - hi.astype(F32)
    mid = r1.astype(BF16)
    lo = (r1 - mid.astype(F32)).astype(BF16)
    return hi, mid, lo


def _ssd_prompt_kernel(xs_ref, bc_ref, zs_ref, dt_ref, dtb_ref, alog_ref, dsk_ref, nw_ref,
                       y_ref, st_ref,
                       xs_s, bt_s, c_s, y_s, state_s, acol_s, arow_s, dtrow_s):
    c = pl.program_id(1)
    n_chunks = pl.num_programs(1)
    q = CHUNK

    @pl.when(c == 0)
    def _():
        state_s[...] = jnp.zeros(state_s.shape, F32)

    for g in range(N_GROUPS):
        xs_s[g] = xs_ref[:, g * GROUP_W:(g + 1) * GROUP_W]
        bt_s[g] = bc_ref[:, g * D_STATE:(g + 1) * D_STATE].T
        c_s[g] = bc_ref[:, (N_GROUPS + g) * D_STATE:(N_GROUPS + g + 1) * D_STATE].astype(BF16)

    dt = _softplus(dt_ref[...] + dtb_ref[...])
    a_neg = -jnp.exp(alog_ref[...])
    d_a = dt * a_neg
    row = lax.broadcasted_iota(jnp.int32, (q, q), 0)
    col = lax.broadcasted_iota(jnp.int32, (q, q), 1)
    causal = row >= col
    tril = jnp.where(causal, 1.0, 0.0).astype(BF16)
    hi, mid, lo3 = _split3(d_a)
    a_cum = (jnp.dot(tril, hi, preferred_element_type=F32)
             + jnp.dot(tril, mid, preferred_element_type=F32)
             + jnp.dot(tril, lo3, preferred_element_type=F32))
    arow_s[...] = a_cum.T
    dtrow_s[...] = dt.T
    for g in range(N_GROUPS):
        sh = (LANES - HEADS_PER_GROUP * g) % LANES
        acol_s[g] = a_cum if sh == 0 else pltpu.roll(a_cum, sh, 1)

    lane = lax.broadcasted_iota(jnp.int32, (q, LANES), 1)
    lo_half = lane < HEAD_DIM

    def group_body(g, carry):
        acol = acol_s[g]
        c_g = c_s[g]
        bt_g = bt_s[g]
        cb = jnp.dot(c_g, bt_g.astype(BF16), preferred_element_type=F32)
        y_off_g = jnp.dot(c_g, state_s[g].astype(BF16), preferred_element_type=F32)
        for k in range(HEADS_PER_GROUP // 2):
            l_parts, b_parts, a_b = [], [], []
            for e in range(2):
                hh = 2 * k + e
                head = g * HEADS_PER_GROUP + hh
                a_col = jnp.broadcast_to(acol[:, hh:hh + 1], (q, q))
                a_row = arow_s[pl.ds(head, 1), :]
                dt_row = dtrow_s[pl.ds(head, 1), :]
                seg = jnp.where(causal, a_col - a_row, NEG_BIG)
                l_parts.append((cb * jnp.exp(seg) * dt_row).astype(BF16))
                w_row = dt_row * jnp.exp(a_col[q - 1:q, :] - a_row)
                b_parts.append((bt_g * w_row).astype(BF16))
                a_b.append(a_col)
            lhs = jnp.concatenate([jnp.concatenate(l_parts, axis=1),
                                   jnp.concatenate(b_parts, axis=1)], axis=0)
            cols = slice(k * LANES, (k + 1) * LANES)
            xs_bf = xs_s[g, :, cols].astype(BF16)
            zero = jnp.zeros_like(xs_bf)
            rhs = jnp.concatenate([jnp.where(lo_half, xs_bf, zero),
                                   jnp.where(lo_half, zero, xs_bf)], axis=0)
            res = jnp.dot(lhs, rhs, preferred_element_type=F32)
            a_pair = jnp.where(lo_half, a_b[0], a_b[1])
            y_s[g, :, cols] = res[0:q] + y_off_g[:, cols] * jnp.exp(a_pair)
            cdec = jnp.exp(a_pair[q - 1:q, :])
            state_s[g, :, cols] = state_s[g, :, cols] * cdec + res[q:2 * q]
        return carry

    for g_static in range(N_GROUPS):
        group_body(g_static, 0)

    for g in range(N_GROUPS):
        cols = slice(g * GROUP_W, (g + 1) * GROUP_W)
        yv = y_s[g] + dsk_ref[:, cols] * xs_s[g]
        v = yv * zs_ref[:, cols]
        ms = jnp.mean(v * v, axis=-1, keepdims=True)
        y_ref[:, cols] = (v * lax.rsqrt(ms + RMS_EPS) * nw_ref[:, cols]).astype(y_ref.dtype)

    @pl.when(c == n_chunks - 1)
    def _():
        for g in range(N_GROUPS):
            st_ref[0, g] = state_s[g].T


def _ssd_prompt(proj, dt_raw, lw, bsz, seq):
    n_chunks = seq // CHUNK
    rows = lambda b, c: b * n_chunks + c
    small = lambda shape: pl.BlockSpec(shape, lambda b, c: (0, 0))
    y, st = pl.pallas_call(
        _ssd_prompt_kernel,
        grid=(bsz, n_chunks),
        in_specs=[
            pl.BlockSpec((CHUNK, D_INNER), lambda b, c: (rows(b, c), OFF_XS // D_INNER)),
            pl.BlockSpec((CHUNK, BC_W), lambda b, c: (rows(b, c), OFF_BC // BC_W)),
            pl.BlockSpec((CHUNK, D_INNER), lambda b, c: (rows(b, c), OFF_Z // D_INNER)),
            pl.BlockSpec((CHUNK, DT_PAD), lambda b, c: (rows(b, c), 0)),
            small((1, DT_PAD)), small((1, DT_PAD)), small((1, D_INNER)), small((1, D_INNER)),
        ],
        out_specs=[
            pl.BlockSpec((CHUNK, D_INNER), lambda b, c: (rows(b, c), 0)),
            pl.BlockSpec((1, N_GROUPS, GROUP_W, D_STATE), lambda b, c: (b, 0, 0, 0)),
        ],
        out_shape=[jax.ShapeDtypeStruct((bsz * seq, D_INNER), BF16),
                   jax.ShapeDtypeStruct((bsz, N_GROUPS, GROUP_W, D_STATE), F32)],
        scratch_shapes=[
            pltpu.VMEM((N_GROUPS, CHUNK, GROUP_W), F32),
            pltpu.VMEM((N_GROUPS, D_STATE, CHUNK), F32),
            pltpu.VMEM((N_GROUPS, CHUNK, D_STATE), BF16),
            pltpu.VMEM((N_GROUPS, CHUNK, GROUP_W), F32),
            pltpu.VMEM((N_GROUPS, D_STATE, GROUP_W), F32),
            pltpu.VMEM((N_GROUPS, CHUNK, LANES), F32),
            pltpu.VMEM((LANES, CHUNK), F32),
            pltpu.VMEM((LANES, CHUNK), F32),
        ],
        compiler_params=_cparams(2),
        name="ssd_prompt",
    )(proj, proj, proj, dt_raw, lw["dt_bias"], lw["a_log"], lw["d_skip_x"], lw["norm_w"])
    return y, st.reshape(bsz, N_HEADS, HEAD_DIM, D_STATE)


def _ssd_sample_prep_kernel(xs_ref, bc_ref, dt_ref, cst_ref, cw_ref, cb_ref, dtb_ref, alog_ref,
                            ex_ref, xs_o, b_o, c_o, decht_o, xdtt_o):
    def conv(u, lo, hi):
        acc = cb_ref[:, lo:hi]
        for j in range(CONV_K - 1):
            acc = acc + cst_ref[j][:, lo:hi] * cw_ref[j:j + 1, lo:hi]
        acc = acc + u * cw_ref[CONV_K - 1:CONV_K, lo:hi]
        return _silu(acc)

    xs = conv(xs_ref[...], 0, D_INNER)
    bc = conv(bc_ref[...], D_INNER, CONV_DIM)
    xs_o[...] = xs
    b_o[...] = bc[:, 0:N_GROUPS * D_STATE]
    c_o[...] = bc[:, N_GROUPS * D_STATE:BC_W]
    dt = _softplus(dt_ref[...] + dtb_ref[...])
    d_a = dt * (-jnp.exp(alog_ref[...]))
    ex = ex_ref[...]

    def expand(v):
        hi, mid, lo3 = _split3(v)
        return (jnp.dot(hi, ex, preferred_element_type=F32)
                + jnp.dot(mid, ex, preferred_element_type=F32)
                + jnp.dot(lo3, ex, preferred_element_type=F32))

    xdt = expand(dt) * xs
    decht_o[...] = jnp.exp(d_a).T
    xdtt_o[...] = xdt.T.astype(BF16)


def _ssd_sample_prep(proj, dt_raw, conv_state_t, lw):
    s = proj.shape[0]
    full = lambda shape: pl.BlockSpec(shape, lambda i: (0,) * len(shape))
    return pl.pallas_call(
        _ssd_sample_prep_kernel,
        grid=(1,),
        in_specs=[
            pl.BlockSpec((s, D_INNER), lambda i: (0, OFF_XS // D_INNER)),
            pl.BlockSpec((s, BC_W), lambda i: (0, OFF_BC // BC_W)),
            full((s, DT_PAD)), full((CONV_K - 1, s, CONV_DIM)),
            full((CONV_K, CONV_DIM)), full((1, CONV_DIM)), full((1, DT_PAD)), full((1, DT_PAD)),
            full((DT_PAD, D_INNER)),
        ],
        out_specs=[full((s, D_INNER)), full((s, N_GROUPS * D_STATE)), full((s, N_GROUPS * D_STATE)),
                   full((DT_PAD, s)), full((D_INNER, s))],
        out_shape=[jax.ShapeDtypeStruct((s, D_INNER), F32),
                   jax.ShapeDtypeStruct((s, N_GROUPS * D_STATE), F32),
                   jax.ShapeDtypeStruct((s, N_GROUPS * D_STATE), F32),
                   jax.ShapeDtypeStruct((DT_PAD, s), F32),
                   jax.ShapeDtypeStruct((D_INNER, s), BF16)],
        compiler_params=_cparams(1),
        name="ssd_sample_prep",
    )(proj, proj, dt_raw, conv_state_t, lw["conv_w"], lw["conv_b"], lw["dt_bias"], lw["a_log"],
      lw["head_expand"])


STEP_SAMPLES = 2


def _ssd_sample_step_kernel(st_ref, decht_ref, xdtt_ref, b_ref, c_ref, *rest, fill_other_layers):
    st_o, yt_o = rest[-2:]
    n_s = xdtt_ref.shape[1]

    @pl.when(pl.program_id(0) == 0)
    def _():
        yt_o[...] = jnp.zeros(yt_o.shape, F32)

    if fill_other_layers:
        st_o[1:] = jnp.zeros((st_o.shape[0] - 1,) + st_o.shape[1:], F32)

    for u in range(STEP_SAMPLES):
        s = pl.program_id(0) * STEP_SAMPLES + u
        onehot = lax.broadcasted_iota(jnp.int32, (GROUP_W, n_s), 1) == s
        onehot_h = lax.broadcasted_iota(jnp.int32, (DT_PAD, n_s), 1) == s
        pick = jnp.where(lax.broadcasted_iota(jnp.int32, (n_s, D_STATE), 0) == s, 1.0, 0.0).astype(BF16)
        dech = jnp.sum(jnp.where(onehot_h, decht_ref[...], 0.0), axis=-1, keepdims=True)
        b_all = b_ref[pl.ds(s, 1), :]
        c_all = c_ref[pl.ds(s, 1), :]
        for g in range(N_GROUPS):
            rows = slice(g * GROUP_W, (g + 1) * GROUP_W)
            cols = slice(g * D_STATE, (g + 1) * D_STATE)
            xdt = jnp.dot(xdtt_ref[rows, :], pick, preferred_element_type=F32)
            b_row = b_all[:, cols]
            c_row = c_all[:, cols]
            upd = xdt * b_row
            parts = []
            for hh in range(HEADS_PER_GROUP):
                h = g * HEADS_PER_GROUP + hh
                r_h = slice(h * HEAD_DIM, (h + 1) * HEAD_DIM)
                parts.append(st_ref[0, u, r_h, :] * dech[h:h + 1, :]
                             + upd[hh * HEAD_DIM:(hh + 1) * HEAD_DIM, :])
            st_new = jnp.concatenate(parts, axis=0)
            st_o[0, u, rows, :] = st_new
            y_col = jnp.sum(st_new * c_row, axis=-1, keepdims=True)
            yt_o[rows, :] = yt_o[rows, :] + jnp.where(onehot, y_col, 0.0)


def _ssd_sample_step(state_all, layer, prev_out, decht, xdtt, b_m, c_m):
    depth = state_all.shape[0]
    s = xdtt.shape[1]
    full = lambda shape: pl.BlockSpec(shape, lambda i: (0,) * len(shape))
    ns = STEP_SAMPLES
    in_specs = [pl.BlockSpec((1, ns, D_INNER, D_STATE), lambda i: (layer, i, 0, 0)),
                full((DT_PAD, s)), full((D_INNER, s)),
                full((s, N_GROUPS * D_STATE)), full((s, N_GROUPS * D_STATE))]
    args = [state_all, decht, xdtt, b_m, c_m]
    if prev_out is None:
        assert layer == 0
        aliases = {}
        st_out = pl.BlockSpec((depth, ns, D_INNER, D_STATE), lambda i: (0, i, 0, 0))
    else:
        in_specs.append(pl.BlockSpec(memory_space=pl.ANY))
        args.append(prev_out)
        aliases = {len(args) - 1: 0}
        st_out = pl.BlockSpec((1, ns, D_INNER, D_STATE), lambda i: (layer, i, 0, 0))
    return pl.pallas_call(
        functools.partial(_ssd_sample_step_kernel, fill_other_layers=prev_out is None),
        grid=(s // ns,),
        in_specs=in_specs,
        out_specs=[st_out, full((D_INNER, s))],
        out_shape=[jax.ShapeDtypeStruct(state_all.shape, F32),
                   jax.ShapeDtypeStruct((D_INNER, s), F32)],
        input_output_aliases=aliases,
        compiler_params=_cparams(1),
        name="ssd_sample_step",
    )(*args)


def _ssd_sample_finish_kernel(yt_ref, xs_ref, zs_ref, dsk_ref, nw_ref, y_ref):
    y = yt_ref[...].T
    for g in range(N_GROUPS):
        cols = slice(g * GROUP_W, (g + 1) * GROUP_W)
        yv = y[:, cols] + dsk_ref[:, cols] * xs_ref[:, cols]
        v = yv * zs_ref[:, cols]
        ms = jnp.mean(v * v, axis=-1, keepdims=True)
        y_ref[:, cols] = (v * lax.rsqrt(ms + RMS_EPS) * nw_ref[:, cols]).astype(y_ref.dtype)


def _ssd_sample_finish(yt, xs, proj, lw):
    s = xs.shape[0]
    full = lambda shape: pl.BlockSpec(shape, lambda i: (0,) * len(shape))
    return pl.pallas_call(
        _ssd_sample_finish_kernel,
        grid=(1,),
        in_specs=[full((D_INNER, s)), full((s, D_INNER)),
                  pl.BlockSpec((s, D_INNER), lambda i: (0, OFF_Z // D_INNER)),
                  full((1, D_INNER)), full((1, D_INNER))],
        out_specs=full((s, D_INNER)),
        out_shape=jax.ShapeDtypeStruct((s, D_INNER), BF16),
        compiler_params=_cparams(1),
        name="ssd_sample_finish",
    )(yt, xs, proj, lw["d_skip_x"], lw["norm_w"])


POOL_TP = 1024
POOL_HALO = 32


def _pool_prompt_kernel(u_ref, w_ref, sc_ref, o_ref, ext_s, w_s, *, tiles_per_seq):
    g = pl.program_id(0)
    i = pl.program_id(1)
    tp = POOL_TP

    @pl.when(i == 0)
    def _():
        w_s[...] = w_ref[...].astype(BF16)

    @pl.when(i % tiles_per_seq == 0)
    def _():
        ext_s[...] = jnp.zeros((POOL_HALO, POOL_GC), F32)

    u = u_ref[...]
    ext = jnp.concatenate([ext_s[...], u], axis=0)
    pos = (i % tiles_per_seq) * tp + lax.broadcasted_iota(jnp.int32, (tp, 1), 0)
    for gi, win in enumerate(POOL_WINDOWS):
        @pl.when(g == gi)
        def _():
            tot, width = ext, 1
            while width < win:
                if width < SUBLANES:
                    tot = tot[SUBLANES:, :] + _rows_above(tot, width)
                else:
                    tot = tot[width:, :] + tot[:-width, :]
                width *= 2
            tot = tot[tot.shape[0] - tp:, :]
            cnt = jnp.minimum(win, pos + 1).astype(F32)
            pooled = (tot / cnt - u).astype(BF16)
            o_ref[...] = jnp.dot(pooled, w_s[...], preferred_element_type=F32) * sc_ref[...]
    ext_s[...] = ext[tp:tp + POOL_HALO, :]


def _pool_prompt(proj, w_pool, layer, scale, seq):
    m = proj.shape[0]
    ng = len(POOL_WINDOWS)
    kern = functools.partial(_pool_prompt_kernel, tiles_per_seq=seq // POOL_TP)
    return pl.pallas_call(
        kern,
        grid=(ng, m // POOL_TP),
        in_specs=[pl.BlockSpec((POOL_TP, POOL_GC), lambda g, i: (i, OFF_POOL // POOL_GC + g)),
                  pl.BlockSpec((None, None, POOL_GC, POOL_GC), lambda g, i: (layer, g, 0, 0)),
                  pl.BlockSpec((1, POOL_GC), lambda g, i: (0, g))],
        out_specs=pl.BlockSpec((POOL_TP, POOL_GC), lambda g, i: (i, g)),
        out_shape=jax.ShapeDtypeStruct((m, D_POOL), F32),
        scratch_shapes=[pltpu.VMEM((POOL_HALO, POOL_GC), F32),
                        pltpu.VMEM((POOL_GC, POOL_GC), BF16)],
        compiler_params=_cparams(2),
        name="pool_prompt",
    )(proj, w_pool, scale)


def _pool_sample_kernel(u_ref, buf_ref, o_ref):
    g = pl.program_id(0)
    for gi, win in enumerate(POOL_WINDOWS):
        @pl.when(g == gi)
        def _():
            u = u_ref[...]
            tot = u
            for k in range(1, win):
                tot = tot + buf_ref[:, POOL_BUF - k, :]
            cnt = float(min(win, PAST_LEN + 1))
            o_ref[...] = (tot / cnt - u).astype(o_ref.dtype)


def _pool_sample(proj, pool_state, layer):
    s = proj.shape[0]
    return pl.pallas_call(
        _pool_sample_kernel,
        grid=(len(POOL_WINDOWS),),
        in_specs=[pl.BlockSpec((s, POOL_GC), lambda g: (0, OFF_POOL // POOL_GC + g)),
                  pl.BlockSpec((None, s, POOL_BUF, POOL_GC), lambda g: (layer, 0, 0, g))],
        out_specs=pl.BlockSpec((s, POOL_GC), lambda g: (0, g)),
        out_shape=jax.ShapeDtypeStruct((s, D_POOL), BF16),
        compiler_params=_cparams(1),
        name="pool_sample",
    )(proj, pool_state)


def _pool_mm_kernel(p_ref, w_ref, sc_ref, o_ref, w_s):
    @pl.when(pl.program_id(1) == 0)
    def _():
        w_s[...] = w_ref[...].astype(BF16)

    acc = jnp.dot(p_ref[...], w_s[...], preferred_element_type=F32)
    o_ref[...] = acc * sc_ref[...]


def _pool_mm(pooled, w_pool, layer, scale, tm):
    m = pooled.shape[0]
    ng = len(POOL_WINDOWS)
    return pl.pallas_call(
        _pool_mm_kernel,
        grid=(ng, m // tm),
        in_specs=[pl.BlockSpec((tm, POOL_GC), lambda g, i: (i, g)),
                  pl.BlockSpec((None, None, POOL_GC, POOL_GC), lambda g, i: (layer, g, 0, 0)),
                  pl.BlockSpec((1, POOL_GC), lambda g, i: (0, g))],
        out_specs=pl.BlockSpec((tm, POOL_GC), lambda g, i: (i, g)),
        out_shape=jax.ShapeDtypeStruct((m, D_POOL), F32),
        scratch_shapes=[pltpu.VMEM((POOL_GC, POOL_GC), BF16)],
        compiler_params=_cparams(2),
        name="pool_mm",
    )(pooled, w_pool, scale)


def _branch_merge_kernel(y_ref, w_ref, ga_ref, gb_ref, yb_ref, o_ref, w_s):
    @pl.when(pl.program_id(1) == 0)
    def _():
        w_s[...] = w_ref[...].astype(BF16)

    y_a = jnp.dot(y_ref[...], w_s[...], preferred_element_type=F32)
    o_ref[...] = (ga_ref[...] * y_a + gb_ref[...] * yb_ref[...]).astype(o_ref.dtype)


def _branch_merge(y, w_br, layer, proj, y_b, tm, tn):
    m = y.shape[0]
    nj = D_MODEL // tn
    ga0 = OFF_GATE // tn
    return pl.pallas_call(
        _branch_merge_kernel,
        grid=(nj, m // tm),
        in_specs=[pl.BlockSpec((tm, D_INNER), lambda j, i: (i, 0)),
                  pl.BlockSpec((None, D_INNER, tn), lambda j, i: (layer, 0, j),
                               pipeline_mode=pl.Buffered(1)),
                  pl.BlockSpec((tm, tn), lambda j, i: (i, ga0 + j)),
                  pl.BlockSpec((tm, tn), lambda j, i: (i, ga0 + nj + j)),
                  pl.BlockSpec((tm, tn), lambda j, i: (i, j))],
        out_specs=pl.BlockSpec((tm, tn), lambda j, i: (i, j)),
        out_shape=jax.ShapeDtypeStruct((m, D_MODEL), BF16),
        scratch_shapes=[pltpu.VMEM((D_INNER, tn), BF16)],
        compiler_params=_cparams(2),
        name="branch_merge",
    )(y, w_br, proj, proj, y_b)


def _proj_ln_kernel(m_ref, w_ref, x_ref, g_ref, b_ref, o_ref, obf_ref, w_s):
    @pl.when(pl.program_id(0) == 0)
    def _():
        w_s[...] = w_ref[...].astype(BF16)

    acc = jnp.dot(m_ref[...], w_s[...], preferred_element_type=F32)
    y = _layer_norm(ALPHA * x_ref[...] + acc, g_ref[...], b_ref[...])
    o_ref[...] = y
    obf_ref[...] = y.astype(BF16)


def _proj_ln(mix, w, layer, x, g, b, tm):
    m, k = mix.shape
    return pl.pallas_call(
        _proj_ln_kernel,
        grid=(m // tm,),
        in_specs=[pl.BlockSpec((tm, k), lambda i: (i, 0)),
                  pl.BlockSpec((None, k, D_MODEL), lambda i: (layer, 0, 0),
                               pipeline_mode=pl.Buffered(1)),
                  pl.BlockSpec((tm, D_MODEL), lambda i: (i, 0)),
                  pl.BlockSpec((1, D_MODEL), lambda i: (0, 0)),
                  pl.BlockSpec((1, D_MODEL), lambda i: (0, 0))],
        out_specs=[pl.BlockSpec((tm, D_MODEL), lambda i: (i, 0)),
                   pl.BlockSpec((tm, D_MODEL), lambda i: (i, 0))],
        out_shape=[jax.ShapeDtypeStruct((m, D_MODEL), F32),
                   jax.ShapeDtypeStruct((m, D_MODEL), BF16)],
        scratch_shapes=[pltpu.VMEM((k, D_MODEL), BF16)],
        compiler_params=_cparams(1),
        name="proj_ln",
    )(mix, w, x, g, b)


FFN_HALO = SUBLANES
FFN_VAL_BLK = D_FF // FFN_TN
FFN_VAL_SHIFT = D_FF % FFN_TN
assert FFN_VAL_SHIFT % LANES == 0


def _ffn_weight_tiles(j, wg_ref, wva_ref, wvb_ref, wg_s, wv_s):
    col = j * FFN_TN + lax.broadcasted_iota(jnp.int32, (1, FFN_TN), 1)
    valid = col < D_FF
    wg_s[...] = jnp.where(valid, wg_ref[...], 0.0).astype(BF16)
    wv = jnp.concatenate([wva_ref[:, FFN_VAL_SHIFT:], wvb_ref[:, :FFN_VAL_SHIFT]], axis=1)
    wv_s[...] = jnp.where(valid, wv, 0.0).astype(BF16)


def _ffn_w_specs(layer, index_of):
    blk = (None, D_MODEL, FFN_TN)
    return [pl.BlockSpec(blk, index_of(lambda j: (layer, 0, j))),
            pl.BlockSpec(blk, index_of(lambda j: (layer, 0, FFN_VAL_BLK + j))),
            pl.BlockSpec(blk, index_of(lambda j: (layer, 0, FFN_VAL_BLK + j + 1)))]


def _ffn_conv_gate(ext_g, ext_v, cw_g, cw_v, cb_g, cb_v, tm):
    def conv(ext, cw, cb):
        acc = cb[...]
        for j in range(FFN_K):
            acc = acc + _shift_rows(ext, FFN_K - 1 - j, tm) * cw[j:j + 1, :]
        return acc
    return _silu(conv(ext_g, cw_g, cb_g)) * conv(ext_v, cw_v, cb_v)


def _ffn_up_prompt_kernel(x_ref, wg_ref, wva_ref, wvb_ref, cwg_ref, cwv_ref, cbg_ref, cbv_ref,
                          a_ref, tg_ref, tv_ref, wg_s, wv_s, hg_s, hv_s, *, tiles_per_seq):
    j = pl.program_id(0)
    i = pl.program_id(1)
    tm = x_ref.shape[0]

    @pl.when(i == 0)
    def _():
        _ffn_weight_tiles(j, wg_ref, wva_ref, wvb_ref, wg_s, wv_s)

    @pl.when(i % tiles_per_seq == 0)
    def _():
        hg_s[...] = jnp.zeros((FFN_HALO, FFN_TN), F32)
        hv_s[...] = jnp.zeros((FFN_HALO, FFN_TN), F32)

    x = x_ref[...]
    ext_g = jnp.concatenate([hg_s[...], jnp.dot(x, wg_s[...], preferred_element_type=F32)], axis=0)
    ext_v = jnp.concatenate([hv_s[...], jnp.dot(x, wv_s[...], preferred_element_type=F32)], axis=0)
    a_ref[...] = _ffn_conv_gate(ext_g, ext_v, cwg_ref, cwv_ref, cbg_ref, cbv_ref, tm).astype(a_ref.dtype)
    tail_g = ext_g[tm:tm + FFN_HALO, :]
    tail_v = ext_v[tm:tm + FFN_HALO, :]
    hg_s[...] = tail_g
    hv_s[...] = tail_v
    tg_ref[0] = tail_g
    tv_ref[0] = tail_v


def _ffn_up_prompt(x_bf, lw, layer, seq, tm):
    m = x_bf.shape[0]
    nj = D_FF_PAD // FFN_TN
    ni = m // tm
    kern = functools.partial(_ffn_up_prompt_kernel, tiles_per_seq=seq // tm)
    cspec_g = lambda r: pl.BlockSpec((r, FFN_TN), lambda j, i: (0, j))
    cspec_v = lambda r: pl.BlockSpec((r, FFN_TN), lambda j, i: (0, nj + j))
    w_specs = _ffn_w_specs(layer, lambda f: (lambda j, i: f(j)))
    return pl.pallas_call(
        kern,
        grid=(nj, ni),
        in_specs=[pl.BlockSpec((tm, D_MODEL), lambda j, i: (i, 0)), *w_specs,
                  cspec_g(FFN_K), cspec_v(FFN_K), cspec_g(1), cspec_v(1)],
        out_specs=[pl.BlockSpec((tm, FFN_TN), lambda j, i: (i, j)),
                   pl.BlockSpec((1, FFN_HALO, FFN_TN), lambda j, i: (i, 0, j)),
                   pl.BlockSpec((1, FFN_HALO, FFN_TN), lambda j, i: (i, 0, j))],
        out_shape=[jax.ShapeDtypeStruct((m, D_FF_PAD), BF16),
                   jax.ShapeDtypeStruct((ni, FFN_HALO, D_FF_PAD), F32),
                   jax.ShapeDtypeStruct((ni, FFN_HALO, D_FF_PAD), F32)],
        scratch_shapes=[pltpu.VMEM((D_MODEL, FFN_TN), BF16),
                        pltpu.VMEM((D_MODEL, FFN_TN), BF16),
                        pltpu.VMEM((FFN_HALO, FFN_TN), F32),
                        pltpu.VMEM((FFN_HALO, FFN_TN), F32)],
        compiler_params=_cparams(2),
        name="ffn_up_prompt",
    )(x_bf, lw["w_up"], lw["w_up"], lw["w_up"],
      lw["fconv_w"], lw["fconv_w"], lw["fconv_b"], lw["fconv_b"])


def _ffn_up_sample_kernel(x_ref, wg_ref, wva_ref, wvb_ref, sg_ref, sva_ref, svb_ref, cwg_ref, cwv_ref,
                          cbg_ref, cbv_ref, a_ref, hg_ref, hv_ref, wg_s, wv_s):
    j = pl.program_id(0)
    _ffn_weight_tiles(j, wg_ref, wva_ref, wvb_ref, wg_s, wv_s)
    x = x_ref[...]
    hg = jnp.dot(x, wg_s[...], preferred_element_type=F32)
    hv = jnp.dot(x, wv_s[...], preferred_element_type=F32)
    hg_ref[...] = hg
    hv_ref[...] = hv
    valid = j * FFN_TN + lax.broadcasted_iota(jnp.int32, (1, FFN_TN), 1) < D_FF

    def conv(h, rows, cw, cb):
        acc = cb[...]
        for k in range(FFN_K - 1):
            acc = acc + jnp.where(valid, rows(k), 0.0) * cw[k:k + 1, :]
        return acc + h * cw[FFN_K - 1:FFN_K, :]

    gate_rows = lambda k: sg_ref[:, k, :]
    value_rows = lambda k: jnp.concatenate([sva_ref[:, k, FFN_VAL_SHIFT:],
                                            svb_ref[:, k, :FFN_VAL_SHIFT]], axis=1)
    a_ref[...] = (_silu(conv(hg, gate_rows, cwg_ref, cbg_ref))
                  * conv(hv, value_rows, cwv_ref, cbv_ref)).astype(a_ref.dtype)


def _ffn_up_sample(x_bf, ffn_state, lw, layer):
    s = x_bf.shape[0]
    nj = D_FF_PAD // FFN_TN
    g_blk = lambda r: pl.BlockSpec((r, FFN_TN), lambda j: (0, j))
    v_blk = lambda r: pl.BlockSpec((r, FFN_TN), lambda j: (0, nj + j))
    w_specs = _ffn_w_specs(layer, lambda f: f)
    st_blk = (None, s, FFN_K - 1, FFN_TN)
    return pl.pallas_call(
        _ffn_up_sample_kernel,
        grid=(nj,),
        in_specs=[pl.BlockSpec((s, D_MODEL), lambda j: (0, 0)), *w_specs,
                  pl.BlockSpec(st_blk, lambda j: (layer, 0, 0, j)),
                  pl.BlockSpec(st_blk, lambda j: (layer, 0, 0, FFN_VAL_BLK + j)),
                  pl.BlockSpec(st_blk, lambda j: (layer, 0, 0, FFN_VAL_BLK + j + 1)),
                  g_blk(FFN_K), v_blk(FFN_K), g_blk(1), v_blk(1)],
        out_specs=[pl.BlockSpec((s, FFN_TN), lambda j: (0, j)),
                   pl.BlockSpec((s, FFN_TN), lambda j: (0, j)),
                   pl.BlockSpec((s, FFN_TN), lambda j: (0, j))],
        out_shape=[jax.ShapeDtypeStruct((s, D_FF_PAD), BF16),
                   jax.ShapeDtypeStruct((s, D_FF_PAD), F32),
                   jax.ShapeDtypeStruct((s, D_FF_PAD), F32)],
        scratch_shapes=[pltpu.VMEM((D_MODEL, FFN_TN), BF16),
                        pltpu.VMEM((D_MODEL, FFN_TN), BF16)],
        compiler_params=_cparams(1),
        name="ffn_up_sample",
    )(x_bf, lw["w_up"], lw["w_up"], lw["w_up"], ffn_state, ffn_state, ffn_state,
      lw["fconv_w"], lw["fconv_w"], lw["fconv_b"], lw["fconv_b"])


DOWN_TK = 512


def _down_ln_kernel(a_ref, w_ref, x_ref, g_ref, b_ref, o_ref, obf_ref):
    k = pl.program_id(1)

    @pl.when(k == 0)
    def _():
        o_ref[...] = ALPHA * x_ref[...]

    row = k * DOWN_TK + lax.broadcasted_iota(jnp.int32, (DOWN_TK, 1), 0)
    w = jnp.where(row < D_FF, w_ref[...], 0.0).astype(BF16)
    o_ref[...] += jnp.dot(a_ref[...], w, preferred_element_type=F32)

    @pl.when(k == pl.num_programs(1) - 1)
    def _():
        y = _layer_norm(o_ref[...], g_ref[...], b_ref[...])
        o_ref[...] = y
        obf_ref[...] = y.astype(BF16)


def _down_ln(a, w, layer, x, g, b, tm):
    m = a.shape[0]
    nk = D_FF_PAD // DOWN_TK
    return pl.pallas_call(
        _down_ln_kernel,
        grid=(m // tm, nk),
        in_specs=[pl.BlockSpec((tm, DOWN_TK), lambda i, k: (i, k)),
                  pl.BlockSpec((None, DOWN_TK, D_MODEL), lambda i, k: (layer, k, 0)),
                  pl.BlockSpec((tm, D_MODEL), lambda i, k: (i, 0)),
                  pl.BlockSpec((1, D_MODEL), lambda i, k: (0, 0)),
                  pl.BlockSpec((1, D_MODEL), lambda i, k: (0, 0))],
        out_specs=[pl.BlockSpec((tm, D_MODEL), lambda i, k: (i, 0)),
                   pl.BlockSpec((tm, D_MODEL), lambda i, k: (i, 0))],
        out_shape=[jax.ShapeDtypeStruct((m, D_MODEL), F32),
                   jax.ShapeDtypeStruct((m, D_MODEL), BF16)],
        compiler_params=_cparams(2),
        name="down_ln",
    )(a, w, x, g, b)


STATE_NS = 16


def _state_out_kernel(sc_ref, sp_ref, sf_ref, *rest):
    oc_ref, op_ref, of_ref = rest[-3:]
    new_rows = rest[:-3]
    layer = pl.program_id(0)
    oc_ref[0, :, 0:CONV_K - 2, :] = sc_ref[0, :, 1:CONV_K - 1, :]
    op_ref[0, :, 0:POOL_BUF - 1, :] = sp_ref[0, :, 1:POOL_BUF, :]
    of_ref[0, :, 0:FFN_K - 2, :] = sf_ref[0, :, 1:FFN_K - 1, :]
    for li in range(len(new_rows) // 5):
        xs, bc, po, hg, hv = new_rows[5 * li:5 * li + 5]

        @pl.when(layer == li)
        def _():
            oc_ref[0, :, CONV_K - 2, 0:D_INNER] = xs[...]
            oc_ref[0, :, CONV_K - 2, D_INNER:CONV_DIM] = bc[...]
            op_ref[0, :, POOL_BUF - 1, :] = po[...]
            of_ref[0, :, FFN_K - 2, 0:D_FF] = hg[:, 0:D_FF]
            of_ref[0, :, FFN_K - 2, D_FF:2 * D_FF] = hv[:, 0:D_FF]


def _state_out(s_conv, s_pool, s_ffn, per_layer):
    depth, s = s_conv.shape[:2]
    ns = STATE_NS
    blk = lambda a: pl.BlockSpec((1, ns) + a.shape[2:], lambda l, i: (l, i, 0, 0))
    rows = lambda w, cb: pl.BlockSpec((ns, w), lambda l, i: (i, cb))
    in_specs = [blk(s_conv), blk(s_pool), blk(s_ffn)]
    args = [s_conv, s_pool, s_ffn]
    for proj, h_g, h_v in per_layer:
        in_specs += [rows(D_INNER, OFF_XS // D_INNER), rows(BC_W, OFF_BC // BC_W),
                     rows(D_POOL, OFF_POOL // D_POOL), rows(D_FF_PAD, 0), rows(D_FF_PAD, 0)]
        args += [proj, proj, proj, h_g, h_v]
    return pl.pallas_call(
        _state_out_kernel,
        grid=(depth, s // ns),
        in_specs=in_specs,
        out_specs=[blk(s_conv), blk(s_pool), blk(s_ffn)],
        out_shape=[jax.ShapeDtypeStruct(a.shape, F32) for a in (s_conv, s_pool, s_ffn)],
        compiler_params=_cparams(2),
        name="state_out",
    )(*args)


def _pad_ff(v):
    pad = [(0, 0)] * (v.ndim - 1) + [(0, D_FF_PAD - D_FF)]
    return jnp.concatenate([jnp.pad(v[..., :D_FF], pad), jnp.pad(v[..., D_FF:], pad)], axis=-1)


def _unpad_ff(v):
    return jnp.concatenate([v[..., :D_FF], v[..., D_FF_PAD:D_FF_PAD + D_FF]], axis=-1)


def _prep_layer(big, b_gate, conv_w, conv_b, dt_bias, a_log, d_skip, norm_w, pool_scale,
                ln1_g, ln1_b, fconv_w, fconv_b, ln2_g, ln2_b):
    pad_h = lambda v: jnp.pad(v, (0, DT_PAD - N_HEADS)).reshape(1, DT_PAD)
    head_of_channel = jnp.arange(D_INNER) // HEAD_DIM
    head_expand = (jnp.arange(DT_PAD)[:, None] == head_of_channel[None, :]).astype(BF16)
    return dict(
        big, b_gate=b_gate.reshape(1, -1),
        conv_w=conv_w, conv_b=conv_b.reshape(1, -1),
        dt_bias=pad_h(dt_bias), a_log=pad_h(a_log),
        d_skip_x=jnp.repeat(d_skip, HEAD_DIM).reshape(1, -1),
        norm_w=norm_w.reshape(1, -1), head_expand=head_expand,
        pool_scale=pool_scale.reshape(1, -1),
        ln1_g=ln1_g.reshape(1, -1), ln1_b=ln1_b.reshape(1, -1),
        fconv_w=_pad_ff(fconv_w), fconv_b=_pad_ff(fconv_b).reshape(1, -1),
        ln2_g=ln2_g.reshape(1, -1), ln2_b=ln2_b.reshape(1, -1),
    )


def _raw_xbc(proj):
    return proj[:, OFF_XS:OFF_XS + CONV_DIM]


def _layer_prompt(x, x_bf, lw, layer, bsz, seq):
    tm_in = 1024
    proj, xbc_tail = _in_proj(x_bf, lw, layer, tm_in, seq // tm_in, conv=True)
    dt_raw = _dt_proj(x_bf, lw["w_in_t"], layer, 1024)
    y, new_ssm = _ssd_prompt(proj, dt_raw, lw, bsz, seq)
    y_b = _pool_prompt(proj, lw["w_pool"], layer, lw["pool_scale"], seq)
    mix = _branch_merge(y, lw["w_br"], layer, proj, y_b, 512, 1024)
    x1, x1_bf = _proj_ln(mix, lw["w_out"], layer, x, lw["ln1_g"], lw["ln1_b"], 512)
    tm_up = 1024
    act, tail_g, tail_v = _ffn_up_prompt(x1_bf, lw, layer, seq, tm_up)
    x2, x2_bf = _down_ln(act, lw["w_down"], layer, x1, lw["ln2_g"], lw["ln2_b"], 1024)
    p3 = proj.reshape(bsz, seq, N_MAIN)
    new_pool = p3[:, seq - POOL_BUF:, OFF_POOL:OFF_POOL + D_POOL]
    tps_in = seq // tm_in
    new_conv = xbc_tail[tps_in - 1::tps_in, CONV_HALO - (CONV_K - 1):, OFF_XS:OFF_XS + CONV_DIM]
    tps = seq // tm_up
    last = slice(tps - 1, None, tps)
    tail = jnp.concatenate([tail_g[last, :, :D_FF], tail_v[last, :, :D_FF]], axis=-1)
    new_ffn = tail[:, FFN_HALO - (FFN_K - 1):, :]
    return x2, x2_bf, new_ssm, new_conv, new_pool, new_ffn


def _layer_sample(x, x_bf, ssm_all, layer, ssm_prev_out, s_conv, s_pool, s_ffn, lw):
    s = x.shape[0]
    proj, _ = _in_proj(x_bf, lw, layer, s, 1, conv=False)
    dt_raw = _dt_proj(x_bf, lw["w_in_t"], layer, s)
    xs, b_m, c_m, decht, xdtt = _ssd_sample_prep(proj, dt_raw, jnp.swapaxes(s_conv, 0, 1), lw)
    new_ssm, yt = _ssd_sample_step(ssm_all, layer, ssm_prev_out, decht, xdtt, b_m, c_m)
    y = _ssd_sample_finish(yt, xs, proj, lw)
    pooled = _pool_sample(proj, s_pool, layer)
    y_b = _pool_mm(pooled, lw["w_pool"], layer, lw["pool_scale"], s)
    mix = _branch_merge(y, lw["w_br"], layer, proj, y_b, s, 1024)
    x1, x1_bf = _proj_ln(mix, lw["w_out"], layer, x, lw["ln1_g"], lw["ln1_b"], s)
    act, h_g, h_v = _ffn_up_sample(x1_bf, s_ffn, lw, layer)
    x2, x2_bf = _down_ln(act, lw["w_down"], layer, x1, lw["ln2_g"], lw["ln2_b"], s)
    return x2, x2_bf, new_ssm, (proj, h_g, h_v)


def kernel(x_prompt, x_sample, state_ssm, state_ssd_conv, state_pool, state_ffn_conv, w_in, b_gate, conv_w, conv_b, dt_bias, a_log, d_skip, ssd_norm_w, w_ssd_branch, w_pool, pool_scale, w_out, ln1_g, ln1_b, w_up, ffn_conv_w, ffn_conv_b, w_down, ln2_g, ln2_b):
    bsz, seq, _ = x_prompt.shape
    n_s = x_sample.shape[0]
    assert x_sample.shape[1] == 1 and seq % 1024 == 0
    xp = x_prompt.reshape(bsz * seq, D_MODEL)
    xs = x_sample.reshape(n_s, D_MODEL)
    xp_bf, xs_bf = xp.astype(BF16), xs.astype(BF16)
    outs_p, outs_s = [], []
    ssm_all = state_ssm.reshape(DEPTH, n_s, D_INNER, D_STATE)
    ssm_out = None
    big = dict(w_in_t=jnp.swapaxes(w_in, 1, 2), w_br=w_ssd_branch, w_pool=w_pool, w_out=w_out,
               w_up=w_up, w_down=w_down)
    for i in range(DEPTH):
        lw = _prep_layer(big, b_gate[i], conv_w[i], conv_b[i], dt_bias[i], a_log[i], d_skip[i],
                         ssd_norm_w[i], pool_scale[i], ln1_g[i], ln1_b[i], ffn_conv_w[i],
                         ffn_conv_b[i], ln2_g[i], ln2_b[i])
        xp, xp_bf, *op = _layer_prompt(xp, xp_bf, lw, i, bsz, seq)
        xs, xs_bf, ssm_out, new_rows = _layer_sample(xs, xs_bf, ssm_all, i, ssm_out, state_ssd_conv[i],
                                                     state_pool, state_ffn_conv, lw)
        outs_p.append(op)
        outs_s.append(new_rows)
    new_conv_s, new_pool_s, new_ffn_s = _state_out(state_ssd_conv, state_pool, state_ffn_conv, outs_s)
    stack = lambda outs, k: jnp.stack([o[k] for o in outs])
    return (xp.reshape(bsz, seq, D_MODEL), xs.reshape(n_s, 1, D_MODEL),
            stack(outs_p, 0), stack(outs_p, 1), stack(outs_p, 2), stack(outs_p, 3),
            ssm_out.reshape(state_ssm.shape), new_conv_s, new_pool_s, new_ffn_s)
```
